```python
import jax
import jax.numpy as jnp
from jax import lax
import numpy as np

D_MODEL = 2048
BATCH = 1
SEQ = 8192
DEPTH = 2

GRID_W = 64
CTX_LEN = 256
EPS = 1e-6
N_MOD = 6
MLA_HEADS = 8
QK_NOPE = 128
QK_ROPE = 64
V_HEAD = 128
Q_LORA = 512
KV_LORA = 256
ROPE_THETA = 10000.0
AXIS_ROPE = QK_ROPE // 2
Q_BLOCK = 128
CONV_CH = D_MODEL // 2
GLA_HEADS = 4
GLA_DK = D_MODEL // 2 // GLA_HEADS
GLA_DV = D_MODEL // GLA_HEADS
GLA_KEY = GLA_HEADS * GLA_DK
GLA_VAL = GLA_HEADS * GLA_DV
GATE_RANK = 16
GATE_NORMALIZER = 16.0
CHUNK = 64
D_FF = 5632
AB_CTX_COLS = KV_LORA + QK_ROPE
AB_SPLITS = (KV_LORA, AB_CTX_COLS, AB_CTX_COLS + Q_LORA, AB_CTX_COLS + Q_LORA + CONV_CH, AB_CTX_COLS + Q_LORA + 2 * CONV_CH)
AB_COLS = AB_CTX_COLS + Q_LORA + 3 * CONV_CH
AB_OUT = CONV_CH + MLA_HEADS * V_HEAD
GLA_CTX_COLS = GLA_KEY + GLA_VAL + 2 * GATE_RANK
GLA_SPLITS = (GLA_KEY, GLA_KEY + GLA_VAL, GLA_KEY + GLA_VAL + GATE_RANK, GLA_CTX_COLS, GLA_CTX_COLS + GLA_KEY)
GLA_COLS = GLA_CTX_COLS + GLA_KEY + GLA_VAL

kernel_name = 'hybrid_shortconv_mla_gla_convffn_dit'


def rmsnorm(x, w):
    xf = x.astype(jnp.float32)
    y = xf * lax.rsqrt(jnp.mean(xf * xf, axis=-1, keepdims=True) + EPS)
    return (y * w.astype(jnp.float32)).astype(x.dtype)


def modulate(x, w, shift, scale):
    return rmsnorm(x, w) * (1.0 + scale) + shift


def dwconv3(x, w, b=None):
    xp = jnp.pad(x, ((0, 0), (1, 1), (0, 0)))
    y = xp[:, :-2] * w[0] + xp[:, 1:-1] * w[1] + xp[:, 2:] * w[2]
    return y if b is None else y + b


def axial_rope_tables(n_tokens):
    rows = n_tokens // GRID_W
    row = jnp.repeat(jnp.arange(rows, dtype=jnp.float32), GRID_W)
    col = jnp.tile(jnp.arange(GRID_W, dtype=jnp.float32), rows)
    inv = ROPE_THETA ** (-jnp.arange(0, AXIS_ROPE, 2, dtype=jnp.float32) / AXIS_ROPE)
    ar = row[:, None] * inv
    ac = col[:, None] * inv
    ang = jnp.concatenate([ar, ar, ac, ac], axis=-1)
    return jnp.cos(ang), jnp.sin(ang)


def apply_rope(x, cos, sin):
    extra = x.ndim - 3
    cos = cos.reshape(cos.shape[0], *([1] * extra), cos.shape[1])
    sin = sin.reshape(sin.shape[0], *([1] * extra), sin.shape[1])
    xf = x.astype(jnp.float32)
    a1, a2, b1, b2 = jnp.split(xf, 4, axis=-1)
    rot = jnp.concatenate([-a2, a1, -b2, b1], axis=-1)
    return (xf * cos + rot * sin).astype(x.dtype)


def attention(q, k, v):
    s = jnp.einsum('bqhd,bkhd->bhqk', q, k, preferred_element_type=jnp.float32) * (q.shape[-1] ** -0.5)
    p = jax.nn.softmax(s, axis=-1).astype(v.dtype)
    return jnp.einsum('bhqk,bkhd->bqhd', p, v)


def blocked_attention(q, k, v):
    bsz, t, nh, dk = q.shape
    nb = t // Q_BLOCK
    qb = q.reshape(bsz, nb, Q_BLOCK, nh, dk).transpose(1, 0, 2, 3, 4)
    o = lax.map(lambda qi: attention(qi, k, v), qb)
    return o.transpose(1, 0, 2, 3, 4).reshape(bsz, t, nh, v.shape[-1])


def mla_q(q_lat, q_norm, w_qb, cos_sin):
    q = (rmsnorm(q_lat, q_norm) @ w_qb).reshape(q_lat.shape[0], q_lat.shape[1], MLA_HEADS, QK_NOPE + QK_ROPE)
    if cos_sin is None:
        return q
    return jnp.concatenate([q[..., :QK_NOPE], apply_rope(q[..., QK_NOPE:], *cos_sin)], axis=-1)


def mla_kv(kv_lat, k_rope, kv_norm, w_kvb, cos_sin):
    kv = (rmsnorm(kv_lat, kv_norm) @ w_kvb).reshape(kv_lat.shape[0], kv_lat.shape[1], MLA_HEADS, QK_NOPE + V_HEAD)
    if cos_sin is not None:
        k_rope = apply_rope(k_rope, *cos_sin)
    k_rope = jnp.broadcast_to(k_rope[:, :, None, :], kv.shape[:3] + (QK_ROPE,))
    k = jnp.concatenate([kv[..., :QK_NOPE], k_rope], axis=-1)
    return k, kv[..., QK_NOPE:]


def mixer_ab(h, hc, w_in, conv_a, q_norm, w_qb, kv_norm, w_kvb, w_out, ctx_out):
    bsz, t, _ = h.shape
    cos_sin = axial_rope_tables(t)
    z = h @ w_in
    zc = hc @ (w_in if ctx_out else w_in[:, :AB_CTX_COLS])
    kv_lat, k_rope, q_lat, a_x, a_b, a_c = jnp.split(z, list(AB_SPLITS), axis=-1)
    k_l, v_l = mla_kv(kv_lat, k_rope, kv_norm, w_kvb, cos_sin)
    k_c, v_c = mla_kv(zc[..., :KV_LORA], zc[..., KV_LORA:AB_CTX_COLS], kv_norm, w_kvb, None)
    q_l = mla_q(q_lat, q_norm, w_qb, cos_sin)
    o_l = blocked_attention(q_l, jnp.concatenate([k_l, k_c], axis=1), jnp.concatenate([v_l, v_c], axis=1))
    y_a = a_b * dwconv3(a_c * a_x, conv_a)
    y = jnp.concatenate([y_a, o_l.reshape(bsz, t, MLA_HEADS * V_HEAD)], axis=-1) @ w_out
    if not ctx_out:
        return y, None
    _, _, qc_lat, c_x, c_b, c_c = jnp.split(zc, list(AB_SPLITS), axis=-1)
    o_c = attention(mla_q(qc_lat, q_norm, w_qb, None), k_c, v_c)
    yc_a = c_b * dwconv3(c_c * c_x, conv_a)
    yc = jnp.concatenate([yc_a, o_c.reshape(o_c.shape[0], o_c.shape[1], MLA_HEADS * V_HEAD)], axis=-1) @ w_out
    return y, yc


def gla_chunked(q, k, v, g, h0):
    bsz, nh, t, dk = k.shape
    n = t // CHUNK
    with_q = q is not None

    def chunks(a):
        return a.reshape(bsz, nh, n, CHUNK, a.shape[-1]).transpose(2, 0, 1, 3, 4)

    mask = jnp.tril(jnp.ones((CHUNK, CHUNK), dtype=bool))

    def step(state, xs):
        if with_q:
            qc, kc, vc, gc = xs
        else:
            kc, vc, gc = xs
        b = jnp.cumsum(gc, axis=-2)
        b_last = b[..., -1:, :]
        new_state = state * jnp.swapaxes(jnp.exp(b_last), -1, -2) + jnp.einsum('bhcd,bhcv->bhdv', kc * jnp.exp(b_last - b), vc)
        if not with_q:
            return new_state, None
        qe = qc * jnp.exp(b)
        ke = kc * jnp.exp(-b)
        att = jnp.where(mask, jnp.einsum('bhid,bhjd->bhij', qe, ke), 0.0)
        out = jnp.einsum('bhij,bhjv->bhiv', att, vc) + jnp.einsum('bhid,bhdv->bhiv', qe, state)
        return new_state, out

    xs = (chunks(k), chunks(v), chunks(g))
    if with_q:
        xs = (chunks(q),) + xs
    final, out = lax.scan(step, h0, xs)
    if with_q:
        out = out.transpose(1, 2, 0, 3, 4).reshape(bsz, nh, t, v.shape[-1])
    return out, final


def to_heads(a, d):
    bsz, t, _ = a.shape
    return a.reshape(bsz, t, GLA_HEADS, d).transpose(0, 2, 1, 3).astype(jnp.float32)


def gla_gate(lr, w, b):
    return jax.nn.log_sigmoid((lr @ w + b).astype(jnp.float32)) / GATE_NORMALIZER


def rev(a):
    return a[:, :, ::-1]


def gla_prep(z, with_q, gfw_w, gfw_b, gbw_w, gbw_b):
    parts = jnp.split(z, list(GLA_SPLITS if with_q else GLA_SPLITS[:3]), axis=-1)
    k = to_heads(parts[0], GLA_DK)
    v = to_heads(parts[1], GLA_DV)
    g_fw = to_heads(gla_gate(parts[2], gfw_w, gfw_b), GLA_DK)
    g_bw = to_heads(gla_gate(parts[3], gbw_w, gbw_b), GLA_DK)
    if not with_q:
        return None, k, v, g_fw, g_bw, None
    q = to_heads(parts[4], GLA_DK) * (GLA_DK ** -0.5)
    return q, k, v, g_fw, g_bw, parts[5]


def gla_out(o, og, o_norm, w_out):
    bsz, nh, t, _ = o.shape
    o = rmsnorm(o, o_norm).transpose(0, 2, 1, 3).reshape(bsz, t, GLA_VAL).astype(og.dtype)
    return (o * jax.nn.silu(og)) @ w_out


def mixer_gla(h, hc, w_in, gfw_w, gfw_b, gbw_w, gbw_b, o_norm, w_out, ctx_out):
    bsz = h.shape[0]
    h0 = jnp.zeros((bsz, GLA_HEADS, GLA_DK, GLA_DV), jnp.float32)
    zc = hc @ (w_in if ctx_out else w_in[:, :GLA_CTX_COLS])
    qc, kc, vc, gfc, gbc, ogc = gla_prep(zc, ctx_out, gfw_w, gfw_b, gbw_w, gbw_b)
    oc_f, sc_f = gla_chunked(qc, kc, vc, gfc, h0)
    oc_b, sc_b = gla_chunked(rev(qc) if ctx_out else None, rev(kc), rev(vc), rev(gbc), h0)
    q, k, v, gf, gb, og = gla_prep(h @ w_in, True, gfw_w, gfw_b, gbw_w, gbw_b)
    o_f, _ = gla_chunked(q, k, v, gf, sc_f)
    o_b, _ = gla_chunked(rev(q), rev(k), rev(v), rev(gb), sc_b)
    y = gla_out(o_f + rev(o_b), og, o_norm, w_out)
    if not ctx_out:
        return y, None
    return y, gla_out(oc_f + rev(oc_b), ogc, o_norm, w_out)


def conv_ffn(h, w_up, conv_w, conv_b, w_down):
    u = dwconv3(h @ w_up, conv_w, conv_b)
    gate, val = jnp.split(u, 2, axis=-1)
    return (jax.nn.silu(gate) * val) @ w_down


def block(x, xc, c_act, cc_act, ada_w, ada_b, norm1, norm2, mixer_fn, mix, ffn, ctx_out):
    m = jnp.split((c_act @ ada_w + ada_b)[:, None, :], N_MOD, axis=-1)
    mc = jnp.split((cc_act @ ada_w + ada_b)[:, None, :], N_MOD, axis=-1)
    y, yc = mixer_fn(modulate(x, norm1, m[0], m[1]), modulate(xc, norm1, mc[0], mc[1]), *mix, ctx_out)
    x = x + m[2] * y
    x = x + m[5] * conv_ffn(modulate(x, norm2, m[3], m[4]), *ffn)
    if not ctx_out:
        return x, None
    xc = xc + mc[2] * yc
    xc = xc + mc[5] * conv_ffn(modulate(xc, norm2, mc[3], mc[4]), *ffn)
    return x, xc


def setup_inputs(seed: int = 0) -> dict:
    key = jax.random.key(seed)
    keys = iter(jax.random.split(key, 16 + 24 * DEPTH))

    def rnd(shape, scale):
        return jax.random.normal(next(keys), shape, jnp.float32) * scale

    def gain(n):
        return 1.0 + rnd((n,), 0.02)

    inp = {}
    inp['x'] = rnd((BATCH, SEQ, D_MODEL), 1.0)
    inp['c'] = rnd((BATCH, D_MODEL), 1.0)
    inp['ctx'] = rnd((BATCH, CTX_LEN, D_MODEL), 1.0)
    inp['c_ctx'] = rnd((D_MODEL,), 1.0)
    for i in range(DEPTH):
        p = 'l%d_' % i
        inp[p + 'ada_w'] = rnd((D_MODEL, N_MOD * D_MODEL), 0.5 * D_MODEL ** -0.5)
        inp[p + 'ada_b'] = rnd((N_MOD * D_MODEL,), 0.02)
        inp[p + 'norm1'] = gain(D_MODEL)
        if i % 2 == 0:
            inp[p + 'w_in'] = rnd((D_MODEL, AB_COLS), D_MODEL ** -0.5)
            inp[p + 'conv_a'] = rnd((3, CONV_CH), 3 ** -0.5)
            inp[p + 'q_norm'] = gain(Q_LORA)
            inp[p + 'w_qb'] = rnd((Q_LORA, MLA_HEADS * (QK_NOPE + QK_ROPE)), Q_LORA ** -0.5)
            inp[p + 'kv_norm'] = gain(KV_LORA)
            inp[p + 'w_kvb'] = rnd((KV_LORA, MLA_HEADS * (QK_NOPE + V_HEAD)), KV_LORA ** -0.5)
            inp[p + 'w_out'] = rnd((AB_OUT, D_MODEL), AB_OUT ** -0.5)
        else:
            inp[p + 'w_in'] = rnd((D_MODEL, GLA_COLS), D_MODEL ** -0.5)
            inp[p + 'gate_fw_w'] = rnd((GATE_RANK, GLA_KEY), GATE_RANK ** -0.5)
            inp[p + 'gate_fw_b'] = rnd((GLA_KEY,), 0.02)
            inp[p + 'gate_bw_w'] = rnd((GATE_RANK, GLA_KEY), GATE_RANK ** -0.5)
            inp[p + 'gate_bw_b'] = rnd((GLA_KEY,), 0.02)
            inp[p + 'o_norm'] = gain(GLA_DV)
            inp[p + 'w_out'] = rnd((GLA_VAL, D_MODEL), GLA_VAL ** -0.5)
        inp[p + 'norm2'] = gain(D_MODEL)
        inp[p + 'ffn_up'] = rnd((D_MODEL, 2 * D_FF), D_MODEL ** -0.5)
        inp[p + 'ffn_conv_w'] = rnd((3, 2 * D_FF), 3 ** -0.5)
        inp[p + 'ffn_conv_b'] = rnd((2 * D_FF,), 0.02)
        inp[p + 'ffn_down'] = rnd((D_FF, D_MODEL), D_FF ** -0.5)
    inp['final_norm'] = gain(D_MODEL)
    return inp


def reference(x, c, ctx, c_ctx,
              l0_ada_w, l0_ada_b, l0_norm1, l0_w_in, l0_conv_a, l0_q_norm, l0_w_qb, l0_kv_norm, l0_w_kvb, l0_w_out,
              l0_norm2, l0_ffn_up, l0_ffn_conv_w, l0_ffn_conv_b, l0_ffn_down,
              l1_ada_w, l1_ada_b, l1_norm1, l1_w_in, l1_gate_fw_w, l1_gate_fw_b, l1_gate_bw_w, l1_gate_bw_b, l1_o_norm, l1_w_out,
              l1_norm2, l1_ffn_up, l1_ffn_conv_w, l1_ffn_conv_b, l1_ffn_down,
              final_norm):
    layers = (
        (l0_ada_w, l0_ada_b, l0_norm1, l0_norm2,
         (l0_w_in, l0_conv_a, l0_q_norm, l0_w_qb, l0_kv_norm, l0_w_kvb, l0_w_out),
         (l0_ffn_up, l0_ffn_conv_w, l0_ffn_conv_b, l0_ffn_down)),
        (l1_ada_w, l1_ada_b, l1_norm1, l1_norm2,
         (l1_w_in, l1_gate_fw_w, l1_gate_fw_b, l1_gate_bw_w, l1_gate_bw_b, l1_o_norm, l1_w_out),
         (l1_ffn_up, l1_ffn_conv_w, l1_ffn_conv_b, l1_ffn_down)),
    )
    c_act = jax.nn.silu(c)
    cc_act = jax.nn.silu(c_ctx)[None, :]
    xc = ctx
    for i in range(DEPTH):
        ada_w, ada_b, n1, n2, mix, ffn = layers[i]
        mixer_fn = mixer_ab if i % 2 == 0 else mixer_gla
        x, xc = block(x, xc, c_act, cc_act, ada_w, ada_b, n1, n2, mixer_fn, mix, ffn, i < DEPTH - 1)
    return rmsnorm(x, final_norm)
```

```python
import functools

import jax
import jax.numpy as jnp
from jax import lax
from jax.experimental import pallas as pl
from jax.experimental.pallas import tpu as pltpu

F32 = jnp.float32
BF16 = jnp.bfloat16

D_MODEL = 2048
SEQ = 8192
GRID_W = 64
CTX_LEN = 256
T_ALL = SEQ + CTX_LEN
EPS = 1e-6
N_MOD = 6
MLA_HEADS = 8
QK_NOPE = 128
QK_ROPE = 64
QK_DIM = QK_NOPE + QK_ROPE
V_HEAD = 128
Q_LORA = 512
KV_LORA = 256
ROPE_THETA = 10000.0
AXIS_ROPE = QK_ROPE // 2
CONV_CH = D_MODEL // 2
GLA_HEADS = 4
GLA_DK = D_MODEL // 2 // GLA_HEADS
GLA_DV = D_MODEL // GLA_HEADS
GLA_KEY = GLA_HEADS * GLA_DK
GLA_VAL = GLA_HEADS * GLA_DV
GATE_RANK = 16
GATE_NORMALIZER = 16.0
CHUNK = 64
D_FF = 5632

LANES = 128
SUBLANES_F32 = 8
SUBLANES_BF16 = 16
VMEM_LIMIT = 56 * 1024 * 1024

TM = 768
TMO = 384
TQ = 256
TKV = 768
GLA_BLOCK = 256
TF = 512


def _silu(v):
    return v / (1.0 + jnp.exp(-v))


def _row_ids(i, rows):
    return i * rows + lax.broadcasted_iota(jnp.int32, (rows, 1), 0)


def _modulated(x, norm, mod, shift_row, is_ctx):
    ms = jnp.mean(x * x, axis=-1, keepdims=True)
    y = x * lax.rsqrt(ms + EPS) * norm
    s = 2 * shift_row
    shift = jnp.where(is_ctx, mod[s + 1:s + 2, :], mod[s:s + 1, :])
    scale = jnp.where(is_ctx, mod[s + 3:s + 4, :], mod[s + 2:s + 3, :])
    return y * (1.0 + scale) + shift


def _ada_kernel(a_ref, w_ref, b_ref, o_ref):
    a = _silu(a_ref[...]).astype(BF16)
    o_ref[...] = jnp.dot(a, w_ref[...].astype(BF16), preferred_element_type=F32) + b_ref[...]


def _ada(cvec, ada_w, ada_b):
    n = ada_w.shape[1]
    tn = 1024
    return pl.pallas_call(
        _ada_kernel,
        grid=(n // tn,),
        in_specs=[
            pl.BlockSpec((SUBLANES_F32, D_MODEL), lambda j: (0, 0)),
            pl.BlockSpec((D_MODEL, tn), lambda j: (0, j)),
            pl.BlockSpec((1, tn), lambda j: (0, j)),
        ],
        out_specs=pl.BlockSpec((SUBLANES_F32, tn), lambda j: (0, j)),
        out_shape=jax.ShapeDtypeStruct((SUBLANES_F32, n), F32),
        compiler_params=pltpu.CompilerParams(
            dimension_semantics=("arbitrary",), vmem_limit_bytes=VMEM_LIMIT),
        name="ada_mod",
    )(cvec, ada_w, ada_b.reshape(1, n))


def _mod_rows(mods):
    m = mods[:2].reshape(2, N_MOD, D_MODEL)
    m = jnp.transpose(m, (1, 0, 2)).reshape(2 * N_MOD, D_MODEL)
    return jnp.concatenate([m, jnp.zeros((4, D_MODEL), F32)], axis=0)


def _inproj_kernel(x_ref, norm_ref, mod_ref, w_ref, we_ref, z_ref, ze_ref, h_ref):
    i = pl.program_id(0)
    j = pl.program_id(1)

    @pl.when(j == 0)
    def _():
        is_ctx = _row_ids(i, TM) >= SEQ
        h = _modulated(x_ref[...], norm_ref[...], mod_ref[...], 0, is_ctx).astype(BF16)
        h_ref[...] = h
        ze_ref[...] = jnp.dot(h, we_ref[...], preferred_element_type=F32)

    z_ref[...] = jnp.dot(h_ref[...], w_ref[...], preferred_element_type=F32).astype(z_ref.dtype)


def _inproj(xa, norm, mod, w, w_extra, tn):
    n = w.shape[1]
    return pl.pallas_call(
        _inproj_kernel,
        grid=(T_ALL // TM, n // tn),
        in_specs=[
            pl.BlockSpec((TM, D_MODEL), lambda i, j: (i, 0)),
            pl.BlockSpec((1, D_MODEL), lambda i, j: (0, 0)),
            pl.BlockSpec((16, D_MODEL), lambda i, j: (0, 0)),
            pl.BlockSpec((D_MODEL, tn), lambda i, j: (0, j)),
            pl.BlockSpec((D_MODEL, LANES), lambda i, j: (0, 0)),
        ],
        out_specs=[
            pl.BlockSpec((TM, tn), lambda i, j: (i, j)),
            pl.BlockSpec((TM, LANES), lambda i, j: (i, 0)),
        ],
        out_shape=[
            jax.ShapeDtypeStruct((T_ALL, n), BF16),
            jax.ShapeDtypeStruct((T_ALL, LANES), F32),
        ],
        scratch_shapes=[pltpu.VMEM((TM, D_MODEL), BF16)],
        compiler_params=pltpu.CompilerParams(
            dimension_semantics=("parallel", "arbitrary"), vmem_limit_bytes=VMEM_LIMIT),
        name="in_proj",
    )(xa, norm.reshape(1, D_MODEL), mod, w, w_extra)


def _mla_prep_kernel(ql_ref, kvl_ref, kr_ref, cs_ref, qn_ref, wqn_ref, wqr_ref, wqrr_ref,
                     kvn_ref, wk_ref, wv_ref, q_ref, k_ref, v_ref):
    cos = cs_ref[:, :QK_ROPE]
    sin = cs_ref[:, QK_ROPE:]

    ql = ql_ref[...].astype(F32)
    qn = (ql * lax.rsqrt(jnp.mean(ql * ql, axis=-1, keepdims=True) + EPS) * qn_ref[...]).astype(BF16)
    scale = QK_DIM ** -0.5
    q_nope = jnp.dot(qn, wqn_ref[...], preferred_element_type=F32) * scale
    q_r = jnp.dot(qn, wqr_ref[...], preferred_element_type=F32)
    q_rr = jnp.dot(qn, wqrr_ref[...], preferred_element_type=F32)

    kvl = kvl_ref[...].astype(F32)
    kvn = (kvl * lax.rsqrt(jnp.mean(kvl * kvl, axis=-1, keepdims=True) + EPS) * kvn_ref[...]).astype(BF16)
    k_nope = jnp.dot(kvn, wk_ref[...], preferred_element_type=F32)
    v = jnp.dot(kvn, wv_ref[...], preferred_element_type=F32)
    kr = kr_ref[...]
    k_rope = (kr[:, :QK_ROPE] * cos + kr[:, QK_ROPE:] * sin).astype(BF16)

    for h in range(MLA_HEADS):
        q_ref[h, :, :QK_NOPE] = q_nope[:, h * QK_NOPE:(h + 1) * QK_NOPE].astype(BF16)
        rope = q_r[:, h * QK_ROPE:(h + 1) * QK_ROPE] * cos + q_rr[:, h * QK_ROPE:(h + 1) * QK_ROPE] * sin
        q_ref[h, :, QK_NOPE:] = (rope * scale).astype(BF16)
        k_ref[h, :, :QK_NOPE] = k_nope[:, h * QK_NOPE:(h + 1) * QK_NOPE].astype(BF16)
        k_ref[h, :, QK_NOPE:] = k_rope
        v_ref[h] = v[:, h * V_HEAD:(h + 1) * V_HEAD].astype(BF16)


def _mla_prep(z, kr, cs, q_norm, wqn, wqr, wqrr, kv_norm, wk, wv):
    tm = TM
    const = lambda shape: pl.BlockSpec(shape, lambda i: (0,) * len(shape))
    return pl.pallas_call(
        _mla_prep_kernel,
        grid=(T_ALL // tm,),
        in_specs=[
            pl.BlockSpec((tm, Q_LORA), lambda i: (i, 3 * CONV_CH // Q_LORA)),
            pl.BlockSpec((tm, KV_LORA), lambda i: (i, (3 * CONV_CH + Q_LORA) // KV_LORA)),
            pl.BlockSpec((tm, LANES), lambda i: (i, 0)),
            pl.BlockSpec((tm, LANES), lambda i: (i, 0)),
            const((1, Q_LORA)),
            const((Q_LORA, MLA_HEADS * QK_NOPE)),
            const((Q_LORA, MLA_HEADS * QK_ROPE)),
            const((Q_LORA, MLA_HEADS * QK_ROPE)),
            const((1, KV_LORA)),
            const((KV_LORA, MLA_HEADS * QK_NOPE)),
            const((KV_LORA, MLA_HEADS * V_HEAD)),
        ],
        out_specs=[
            pl.BlockSpec((MLA_HEADS, tm, QK_DIM), lambda i: (0, i, 0)),
            pl.BlockSpec((MLA_HEADS, tm, QK_DIM), lambda i: (0, i, 0)),
            pl.BlockSpec((MLA_HEADS, tm, V_HEAD), lambda i: (0, i, 0)),
        ],
        out_shape=[
            jax.ShapeDtypeStruct((MLA_HEADS, T_ALL, QK_DIM), BF16),
            jax.ShapeDtypeStruct((MLA_HEADS, T_ALL, QK_DIM), BF16),
            jax.ShapeDtypeStruct((MLA_HEADS, T_ALL, V_HEAD), BF16),
        ],
        compiler_params=pltpu.CompilerParams(
            dimension_semantics=("parallel",), vmem_limit_bytes=VMEM_LIMIT),
        name="mla_prep",
    )(z, z, kr, cs, q_norm.reshape(1, Q_LORA), wqn, wqr, wqrr, kv_norm.reshape(1, KV_LORA), wk, wv)


def _attn_kernel(q_ref, k_ref, v_ref, o_ref):
    qi = pl.program_id(1)
    q = q_ref[0]

    def step(k, v, carry):
        m, l, acc = carry
        s = lax.dot_general(q, k, (((1,), (1,)), ((), ())), preferred_element_type=F32)
        m_new = jnp.maximum(m, jnp.max(s, axis=-1, keepdims=True))
        alpha = jnp.exp(m - m_new)
        p = jnp.exp(s - m_new)
        l = alpha * l + jnp.sum(p, axis=-1, keepdims=True)
        acc = alpha * acc + jnp.dot(p.astype(BF16), v, preferred_element_type=F32)
        return m_new, l, acc

    init = (jnp.full((TQ, 1), -jnp.inf, F32), jnp.zeros((TQ, 1), F32), jnp.zeros((TQ, V_HEAD), F32))

    @pl.when(qi < SEQ // TQ)
    def _():
        def body(c, carry):
            start = pl.multiple_of(c * TKV, TKV)
            return step(k_ref[0, pl.ds(start, TKV), :], v_ref[0, pl.ds(start, TKV), :], carry)
        _, l, acc = lax.fori_loop(0, T_ALL // TKV, body, init)
        o_ref[...] = (acc / l).astype(o_ref.dtype)

    @pl.when(qi >= SEQ // TQ)
    def _():
        _, l, acc = step(k_ref[0, SEQ:, :], v_ref[0, SEQ:, :], init)
        o_ref[...] = (acc / l).astype(o_ref.dtype)


def _attention(q, k, v):
    return pl.pallas_call(
        _attn_kernel,
        grid=(MLA_HEADS, T_ALL // TQ),
        in_specs=[
            pl.BlockSpec((1, TQ, QK_DIM), lambda h, i: (h, i, 0)),
            pl.BlockSpec((1, T_ALL, QK_DIM), lambda h, i: (h, 0, 0)),
            pl.BlockSpec((1, T_ALL, V_HEAD), lambda h, i: (h, 0, 0)),
        ],
        out_specs=pl.BlockSpec((TQ, V_HEAD), lambda h, i: (i, h)),
        out_shape=jax.ShapeDtypeStruct((T_ALL, MLA_HEADS * V_HEAD), BF16),
        compiler_params=pltpu.CompilerParams(
            dimension_semantics=("parallel", "arbitrary"), vmem_limit_bytes=VMEM_LIMIT),
        name="mla_attn",
    )(q, k, v)


def _neighbour_masks(rows):
    has_prev = (rows != 0) & (rows != SEQ)
    has_next = (rows != SEQ - 1) & (rows != T_ALL - 1)
    return has_prev, has_next


def _ab_out_kernel(x_ref, ax_ref, ab_ref, ac_ref, axp_ref, acp_ref, axn_ref, acn_ref, o_ref,
                   cw_ref, mod_ref, w_ref, out_ref, h_ref):
    i = pl.program_id(0)
    j = pl.program_id(1)
    tm = x_ref.shape[0]
    rows = _row_ids(i, tm)

    @pl.when(j == 0)
    def _():
        p = ac_ref[...].astype(F32) * ax_ref[...].astype(F32)
        halo = SUBLANES_BF16
        p_before = acp_ref[halo - 1:halo, :].astype(F32) * axp_ref[halo - 1:halo, :].astype(F32)
        p_after = acn_ref[0:1, :].astype(F32) * axn_ref[0:1, :].astype(F32)
        local = lax.broadcasted_iota(jnp.int32, (tm, 1), 0)
        p_prev = jnp.where(local == 0, p_before, pltpu.roll(p, 1, axis=0))
        p_next = jnp.where(local == tm - 1, p_after, pltpu.roll(p, tm - 1, axis=0))
        has_prev, has_next = _neighbour_masks(rows)
        cw = cw_ref[...]
        conv = (jnp.where(has_prev, p_prev, 0.0) * cw[0:1, :] + p * cw[1:2, :]
                + jnp.where(has_next, p_next, 0.0) * cw[2:3, :])
        h_ref[:, :CONV_CH] = (ab_ref[...].astype(F32) * conv).astype(BF16)
        h_ref[:, CONV_CH:] = o_ref[...]

    y = jnp.dot(h_ref[...], w_ref[...], preferred_element_type=F32)
    mod = mod_ref[...]
    gate = jnp.where(rows >= SEQ, mod[5:6, :], mod[4:5, :])
    out_ref[...] = x_ref[...] + gate * y


def _ab_out(xa, z, o, conv_a, mod, w_out):
    tn = 1024
    tm = TMO
    hb = SUBLANES_BF16
    nb = tm // hb
    last = T_ALL // hb - 1
    prev_map = lambda c: (lambda i, j: (jnp.maximum(i * nb - 1, 0), c))
    next_map = lambda c: (lambda i, j: (jnp.minimum((i + 1) * nb, last), c))
    return pl.pallas_call(
        _ab_out_kernel,
        grid=(T_ALL // tm, D_MODEL // tn),
        in_specs=[
            pl.BlockSpec((tm, tn), lambda i, j: (i, j)),
            pl.BlockSpec((tm, CONV_CH), lambda i, j: (i, 0)),
            pl.BlockSpec((tm, CONV_CH), lambda i, j: (i, 1)),
            pl.BlockSpec((tm, CONV_CH), lambda i, j: (i, 2)),
            pl.BlockSpec((hb, CONV_CH), prev_map(0)),
            pl.BlockSpec((hb, CONV_CH), prev_map(2)),
            pl.BlockSpec((hb, CONV_CH), next_map(0)),
            pl.BlockSpec((hb, CONV_CH), next_map(2)),
            pl.BlockSpec((tm, MLA_HEADS * V_HEAD), lambda i, j: (i, 0)),
            pl.BlockSpec((3, CONV_CH), lambda i, j: (0, 0)),
            pl.BlockSpec((16, tn), lambda i, j: (0, j)),
            pl.BlockSpec((D_MODEL, tn), lambda i, j: (0, j)),
        ],
        out_specs=pl.BlockSpec((tm, tn), lambda i, j: (i, j)),
        out_shape=jax.ShapeDtypeStruct((T_ALL, D_MODEL), F32),
        scratch_shapes=[pltpu.VMEM((tm, D_MODEL), BF16)],
        compiler_params=pltpu.CompilerParams(
            dimension_semantics=("parallel", "arbitrary"), vmem_limit_bytes=VMEM_LIMIT),
        name="ab_out",
    )(xa, z, z, z, z, z, z, z, o, conv_a, mod, w_out)


def _ffn_kernel(x_ref, xp_ref, xn_ref, norm_ref, mod_ref, wg_ref, wv_ref, cwg_ref, cwv_ref,
                cbg_ref, cbv_ref, wd_ref, fnorm_ref, out_ref, h_ref, hh_ref, *, final):
    i = pl.program_id(0)
    j = pl.program_id(1)
    halo = SUBLANES_F32
    rows = _row_ids(i, TM)

    @pl.when(j == 0)
    def _():
        norm = norm_ref[...]
        mod = mod_ref[...]
        h_ref[...] = _modulated(x_ref[...], norm, mod, 3, rows >= SEQ).astype(BF16)
        xh = jnp.concatenate([xp_ref[...], xn_ref[...]], axis=0)
        off = lax.broadcasted_iota(jnp.int32, (2 * halo, 1), 0)
        rows_h = jnp.where(off < halo, i * TM - halo + off, (i + 1) * TM - halo + off)
        hh_ref[...] = _modulated(xh, norm, mod, 3, rows_h >= SEQ).astype(BF16)
        out_ref[...] = jnp.zeros_like(out_ref)

    has_prev, has_next = _neighbour_masks(rows)
    local = lax.broadcasted_iota(jnp.int32, (TM, 1), 0)
    h = h_ref[...]
    hh = hh_ref[...]

    def conv(w_ref, cw_ref, cb_ref):
        w = w_ref[...]
        u = jnp.dot(h, w, preferred_element_type=F32)
        uh = jnp.dot(hh, w, preferred_element_type=F32)
        cw = cw_ref[...]
        u_prev = jnp.where(local == 0, uh[halo - 1:halo, :], pltpu.roll(u, 1, axis=0))
        u_next = jnp.where(local == TM - 1, uh[halo:halo + 1, :], pltpu.roll(u, TM - 1, axis=0))
        return (jnp.where(has_prev, u_prev, 0.0) * cw[0:1, :] + u * cw[1:2, :]
                + jnp.where(has_next, u_next, 0.0) * cw[2:3, :] + cb_ref[...])

    act = (_silu(conv(wg_ref, cwg_ref, cbg_ref)) * conv(wv_ref, cwv_ref, cbv_ref)).astype(BF16)
    out_ref[...] += jnp.dot(act, wd_ref[...], preferred_element_type=F32)

    @pl.when(j == pl.num_programs(1) - 1)
    def _():
        mod = mod_ref[...]
        gate = jnp.where(rows >= SEQ, mod[11:12, :], mod[10:11, :])
        y = x_ref[...] + gate * out_ref[...]
        if final:
            y = y * lax.rsqrt(jnp.mean(y * y, axis=-1, keepdims=True) + EPS) * fnorm_ref[...]
        out_ref[...] = y


def _ffn(xa, norm, mod, w_up, conv_w, conv_b, w_down, final_norm, final):
    nf = D_FF // TF
    hb = SUBLANES_F32
    nb = TM // hb
    last = T_ALL // hb - 1
    out_rows = SEQ if final else T_ALL
    cb = conv_b.reshape(1, 2 * D_FF)
    return pl.pallas_call(
        functools.partial(_ffn_kernel, final=final),
        grid=(T_ALL // TM, nf),
        in_specs=[
            pl.BlockSpec((TM, D_MODEL), lambda i, j: (i, 0), pipeline_mode=pl.Buffered(1)),
            pl.BlockSpec((hb, D_MODEL), lambda i, j: (jnp.maximum(i * nb - 1, 0), 0)),
            pl.BlockSpec((hb, D_MODEL), lambda i, j: (jnp.minimum((i + 1) * nb, last), 0)),
            pl.BlockSpec((1, D_MODEL), lambda i, j: (0, 0)),
            pl.BlockSpec((16, D_MODEL), lambda i, j: (0, 0)),
            pl.BlockSpec((D_MODEL, TF), lambda i, j: (0, j)),
            pl.BlockSpec((D_MODEL, TF), lambda i, j: (0, j + nf)),
            pl.BlockSpec((3, TF), lambda i, j: (0, j)),
            pl.BlockSpec((3, TF), lambda i, j: (0, j + nf)),
            pl.BlockSpec((1, TF), lambda i, j: (0, j)),
            pl.BlockSpec((1, TF), lambda i, j: (0, j + nf)),
            pl.BlockSpec((TF, D_MODEL), lambda i, j: (j, 0)),
            pl.BlockSpec((1, D_MODEL), lambda i, j: (0, 0)),
        ],
        out_specs=pl.BlockSpec((TM, D_MODEL), lambda i, j: (i, 0), pipeline_mode=pl.Buffered(1)),
        out_shape=jax.ShapeDtypeStruct((out_rows, D_MODEL), F32),
        scratch_shapes=[
            pltpu.VMEM((TM, D_MODEL), BF16),
            pltpu.VMEM((2 * hb, D_MODEL), BF16),
        ],
        compiler_params=pltpu.CompilerParams(
            dimension_semantics=("parallel", "arbitrary"), vmem_limit_bytes=VMEM_LIMIT),
        name="conv_ffn_final" if final else "conv_ffn",
    )(xa, xa, xa, norm.reshape(1, D_MODEL), mod, w_up, w_up, conv_w, conv_w, cb, cb, w_down,
      final_norm.reshape(1, D_MODEL))


def _split3(a):
    a1 = a.astype(BF16)
    r1 = a - a1.astype(F32)
    a2 = r1.astype(BF16)
    a3 = (r1 - a2.astype(F32)).astype(BF16)
    return a1, a2, a3


def _gla_direction(q_ref, k_ref, v_ref, lr_ref, gw_ref, gb_ref, o_ref, st_ref, causal):
    n = GLA_BLOCK
    nc = n // CHUNK
    r = lax.broadcasted_iota(jnp.int32, (n, n), 0)
    c = lax.broadcasted_iota(jnp.int32, (n, n), 1)
    shift = CHUNK.bit_length() - 1
    same = jnp.right_shift(r, shift) == jnp.right_shift(c, shift)
    tri = same & ((c <= r) if causal else (c >= r))
    tri_b = jnp.where(tri, 1.0, 0.0).astype(BF16)
    same_b = jnp.where(same, 1.0, 0.0).astype(BF16)

    pre = jnp.dot(lr_ref[...].astype(BF16), gw_ref[...], preferred_element_type=F32) + gb_ref[...]
    g = (jnp.minimum(pre, 0.0) - jnp.log(1.0 + jnp.exp(-jnp.abs(pre)))) / GATE_NORMALIZER
    b = jnp.zeros((n, GLA_DK), F32)
    btot = jnp.zeros((n, GLA_DK), F32)
    for part in _split3(g):
        b = b + jnp.dot(tri_b, part, preferred_element_type=F32)
        btot = btot + jnp.dot(same_b, part, preferred_element_type=F32)

    q = q_ref[...].astype(F32) * (GLA_DK ** -0.5)
    k = k_ref[...].astype(F32)
    v = v_ref[...]
    qe = (q * jnp.exp(b)).astype(BF16)
    ke = (k * jnp.exp(-b)).astype(BF16)
    kd = (k * jnp.exp(btot - b)).astype(BF16)
    att = lax.dot_general(qe, ke, (((1,), (1,)), ((), ())), preferred_element_type=F32)
    att = jnp.where(tri, att, 0.0).astype(BF16)
    intra = jnp.dot(att, v, preferred_element_type=F32)
    dec = jnp.exp(btot)

    order = range(nc) if causal else range(nc - 1, -1, -1)
    for ci in order:
        sl = slice(ci * CHUNK, (ci + 1) * CHUNK)
        st = st_ref[...]
        inter = lax.dot_general(qe[sl], st.astype(BF16), (((1,), (1,)), ((), ())),
                                preferred_element_type=F32)
        o_ref[sl, :] = intra[sl] + inter
        upd = lax.dot_general(v[sl], kd[sl], (((0,), (0,)), ((), ())), preferred_element_type=F32)
        st_ref[...] = st * dec[ci * CHUNK:ci * CHUNK + 1, :] + upd


def _gla_kernel(qf_ref, kf_ref, vf_ref, lrf_ref, qb_ref, kb_ref, vb_ref, lrb_ref,
                gwf_ref, gbf_ref, gwb_ref, gbb_ref, of_ref, ob_ref, sf_ref, sb_ref):
    @pl.when(pl.program_id(1) == 0)
    def _():
        sf_ref[...] = jnp.zeros_like(sf_ref)
        sb_ref[...] = jnp.zeros_like(sb_ref)

    _gla_direction(qf_ref, kf_ref, vf_ref, lrf_ref, gwf_ref, gbf_ref, of_ref, sf_ref, True)
    _gla_direction(qb_ref, kb_ref, vb_ref, lrb_ref, gwb_ref, gbb_ref, ob_ref, sb_ref, False)


def _gla(z, lr, gwf, gbf, gwb, gbb):
    nblk = T_ALL // GLA_BLOCK
    ctx_blk = SEQ // GLA_BLOCK
    fwd = lambda s: jnp.where(s == 0, ctx_blk, s - 1)
    bwd = lambda s: jnp.where(s == 0, ctx_blk, ctx_blk - s)
    kq = GLA_KEY // GLA_DK
    vo = 2 * GLA_KEY // GLA_DV

    def specs(order):
        return [
            pl.BlockSpec((GLA_BLOCK, GLA_DK), lambda h, s: (order(s), kq + h)),
            pl.BlockSpec((GLA_BLOCK, GLA_DK), lambda h, s: (order(s), h)),
            pl.BlockSpec((GLA_BLOCK, GLA_DV), lambda h, s: (order(s), vo + h)),
            pl.BlockSpec((GLA_BLOCK, LANES), lambda h, s: (order(s), 0)),
        ]

    gate_specs = [
        pl.BlockSpec((LANES, GLA_DK), lambda h, s: (0, h)),
        pl.BlockSpec((1, GLA_DK), lambda h, s: (0, h)),
    ]
    return pl.pallas_call(
        _gla_kernel,
        grid=(GLA_HEADS, nblk),
        in_specs=specs(fwd) + specs(bwd) + gate_specs + gate_specs,
        out_specs=[
            pl.BlockSpec((GLA_BLOCK, GLA_DV), lambda h, s: (fwd(s), h)),
            pl.BlockSpec((GLA_BLOCK, GLA_DV), lambda h, s: (bwd(s), h)),
        ],
        out_shape=[
            jax.ShapeDtypeStruct((T_ALL, GLA_VAL), F32),
            jax.ShapeDtypeStruct((T_ALL, GLA_VAL), F32),
        ],
        scratch_shapes=[
            pltpu.VMEM((GLA_DV, GLA_DK), F32),
            pltpu.VMEM((GLA_DV, GLA_DK), F32),
        ],
        compiler_params=pltpu.CompilerParams(
            dimension_semantics=("parallel", "arbitrary"), vmem_limit_bytes=VMEM_LIMIT),
        name="gla_scan",
    )(z, z, z, lr, z, z, z, lr, gwf, gbf.reshape(1, GLA_KEY), gwb, gbb.reshape(1, GLA_KEY))


def _gla_out_kernel(x_ref, of_ref, ob_ref, og_ref, on_ref, mod_ref, w_ref, out_ref, h_ref):
    i = pl.program_id(0)
    j = pl.program_id(1)

    @pl.when(j == 0)
    def _():
        o = of_ref[...] + ob_ref[...]
        on = on_ref[...]
        for hd in range(GLA_HEADS):
            sl = slice(hd * GLA_DV, (hd + 1) * GLA_DV)
            oh = o[:, sl]
            y = oh * lax.rsqrt(jnp.mean(oh * oh, axis=-1, keepdims=True) + EPS) * on
            h_ref[:, sl] = (y * _silu(og_ref[:, sl].astype(F32))).astype(BF16)

    y = jnp.dot(h_ref[...], w_ref[...], preferred_element_type=F32)
    mod = mod_ref[...]
    gate = jnp.where(_row_ids(i, x_ref.shape[0]) >= SEQ, mod[5:6, :], mod[4:5, :])
    out_ref[...] = x_ref[...] + gate * y


def _gla_out(xa, o_f, o_b, z, o_norm, mod, w_out):
    tn = 1024
    tm = TMO
    og_blk = 2 * GLA_KEY // GLA_VAL + 1
    return pl.pallas_call(
        _gla_out_kernel,
        grid=(T_ALL // tm, D_MODEL // tn),
        in_specs=[
            pl.BlockSpec((tm, tn), lambda i, j: (i, j)),
            pl.BlockSpec((tm, GLA_VAL), lambda i, j: (i, 0)),
            pl.BlockSpec((tm, GLA_VAL), lambda i, j: (i, 0)),
            pl.BlockSpec((tm, GLA_VAL), lambda i, j: (i, og_blk)),
            pl.BlockSpec((1, GLA_DV), lambda i, j: (0, 0)),
            pl.BlockSpec((16, tn), lambda i, j: (0, j)),
            pl.BlockSpec((GLA_VAL, tn), lambda i, j: (0, j)),
        ],
        out_specs=pl.BlockSpec((tm, tn), lambda i, j: (i, j)),
        out_shape=jax.ShapeDtypeStruct((T_ALL, D_MODEL), F32),
        scratch_shapes=[pltpu.VMEM((tm, GLA_VAL), BF16)],
        compiler_params=pltpu.CompilerParams(
            dimension_semantics=("parallel", "arbitrary"), vmem_limit_bytes=VMEM_LIMIT),
        name="gla_out",
    )(xa, o_f, o_b, z, o_norm.reshape(1, GLA_DV), mod, w_out)


def _rot_cols(w):
    a1, a2, b1, b2 = jnp.split(w, 4, axis=-1)
    return jnp.concatenate([-a2, a1, -b2, b1], axis=-1)


def _rope_table():
    t = jnp.arange(SEQ)
    row = (t // GRID_W).astype(F32)
    col = (t % GRID_W).astype(F32)
    inv = ROPE_THETA ** (-jnp.arange(0, AXIS_ROPE, 2, dtype=F32) / AXIS_ROPE)
    ar = row[:, None] * inv
    ac = col[:, None] * inv
    ang = jnp.concatenate([ar, ar, ac, ac], axis=-1)
    lat = jnp.concatenate([jnp.cos(ang), jnp.sin(ang)], axis=-1)
    ctx = jnp.concatenate([jnp.ones((CTX_LEN, QK_ROPE), F32), jnp.zeros((CTX_LEN, QK_ROPE), F32)], axis=-1)
    return jnp.concatenate([lat, ctx], axis=0)


def kernel(x, c, ctx, c_ctx, l0_ada_w, l0_ada_b, l0_norm1, l0_w_in, l0_conv_a, l0_q_norm, l0_w_qb, l0_kv_norm, l0_w_kvb, l0_w_out, l0_norm2, l0_ffn_up, l0_ffn_conv_w, l0_ffn_conv_b, l0_ffn_down, l1_ada_w, l1_ada_b, l1_norm1, l1_w_in, l1_gate_fw_w, l1_gate_fw_b, l1_gate_bw_w, l1_gate_bw_b, l1_o_norm, l1_w_out, l1_norm2, l1_ffn_up, l1_ffn_conv_w, l1_ffn_conv_b, l1_ffn_down, final_norm):
    assert x.shape == (1, SEQ, D_MODEL) and ctx.shape == (1, CTX_LEN, D_MODEL)
    xa = jnp.concatenate([x[0], ctx[0]], axis=0)
    cvec = jnp.concatenate([c, c_ctx[None, :], jnp.zeros((SUBLANES_F32 - 2, D_MODEL), F32)], axis=0)
    mod0 = _mod_rows(_ada(cvec, l0_ada_w, l0_ada_b))
    mod1 = _mod_rows(_ada(cvec, l1_ada_w, l1_ada_b))

    kv_lat, k_rope, q_lat, a_x, a_b, a_c = jnp.split(
        l0_w_in, [KV_LORA, KV_LORA + QK_ROPE, KV_LORA + QK_ROPE + Q_LORA,
                  KV_LORA + QK_ROPE + Q_LORA + CONV_CH, KV_LORA + QK_ROPE + Q_LORA + 2 * CONV_CH], axis=1)
    w_in0 = jnp.concatenate([a_x, a_b, a_c, q_lat, kv_lat], axis=1).astype(BF16)
    w_kr = jnp.concatenate([k_rope, _rot_cols(k_rope)], axis=1).astype(BF16)
    z0, kr = _inproj(xa, l0_norm1, mod0, w_in0, w_kr, tn=768)

    wq = l0_w_qb.reshape(Q_LORA, MLA_HEADS, QK_DIM)
    wqn = wq[:, :, :QK_NOPE].reshape(Q_LORA, MLA_HEADS * QK_NOPE).astype(BF16)
    wqr = wq[:, :, QK_NOPE:]
    wqrr = _rot_cols(wqr).reshape(Q_LORA, MLA_HEADS * QK_ROPE).astype(BF16)
    wqr = wqr.reshape(Q_LORA, MLA_HEADS * QK_ROPE).astype(BF16)
    wkv = l0_w_kvb.reshape(KV_LORA, MLA_HEADS, QK_NOPE + V_HEAD)
    wk = wkv[:, :, :QK_NOPE].reshape(KV_LORA, MLA_HEADS * QK_NOPE).astype(BF16)
    wv = wkv[:, :, QK_NOPE:].reshape(KV_LORA, MLA_HEADS * V_HEAD).astype(BF16)
    q, k, v = _mla_prep(z0, kr, _rope_table(), l0_q_norm, wqn, wqr, wqrr, l0_kv_norm, wk, wv)
    o = _attention(q, k, v)
    xa = _ab_out(xa, z0, o, l0_conv_a, mod0, l0_w_out.astype(BF16))
    xa = _ffn(xa, l0_norm2, mod0, l0_ffn_up.astype(BF16), l0_ffn_conv_w, l0_ffn_conv_b,
              l0_ffn_down.astype(BF16), final_norm, final=False)

    wk1, wv1, wlf, wlb, wq1, wog = jnp.split(
        l1_w_in, [GLA_KEY, GLA_KEY + GLA_VAL, GLA_KEY + GLA_VAL + GATE_RANK,
                  GLA_KEY + GLA_VAL + 2 * GATE_RANK, 2 * GLA_KEY + GLA_VAL + 2 * GATE_RANK], axis=1)
    w_in1 = jnp.concatenate([wk1, wq1, wv1, wog], axis=1).astype(BF16)
    w_lr = jnp.concatenate([wlf, wlb, jnp.zeros((D_MODEL, LANES - 2 * GATE_RANK), F32)], axis=1).astype(BF16)
    z1, lr = _inproj(xa, l1_norm1, mod1, w_in1, w_lr, tn=768)
    pad_f = jnp.zeros((LANES - GATE_RANK, GLA_KEY), F32)
    gwf = jnp.concatenate([l1_gate_fw_w, pad_f], axis=0).astype(BF16)
    gwb = jnp.concatenate([jnp.zeros((GATE_RANK, GLA_KEY), F32), l1_gate_bw_w,
                           jnp.zeros((LANES - 2 * GATE_RANK, GLA_KEY), F32)], axis=0).astype(BF16)
    o_f, o_b = _gla(z1, lr, gwf, l1_gate_fw_b, gwb, l1_gate_bw_b)
    xa = _gla_out(xa, o_f, o_b, z1, l1_o_norm, mod1, l1_w_out.astype(BF16))
    out = _ffn(xa, l1_norm2, mod1, l1_ffn_up.astype(BF16), l1_ffn_conv_w, l1_ffn_conv_b,
               l1_ffn_down.astype(BF16), final_norm, final=True)
    return out[None]
```

```python
import functools

import jax
import jax.numpy as jnp
from jax import lax
from jax.experimental import pallas as pl
from jax.experimental.pallas import tpu as pltpu

F32 = jnp.float32
BF16 = jnp.bfloat16

D_MODEL = 2048
SEQ = 8192
GRID_W = 64
CTX_LEN = 256
T_ALL = SEQ + CTX_LEN
EPS = 1e-6
LOG2_E = 1.4426950408889634
N_MOD = 6
MLA_HEADS = 8
QK_NOPE = 128
QK_ROPE = 64
QK_DIM = QK_NOPE + QK_ROPE
V_HEAD = 128
Q_LORA = 512
KV_LORA = 256
ROPE_THETA = 10000.0
AXIS_ROPE = QK_ROPE // 2
CONV_CH = D_MODEL // 2
GLA_HEADS = 4
GLA_DK = D_MODEL // 2 // GLA_HEADS
GLA_DV = D_MODEL // GLA_HEADS
GLA_KEY = GLA_HEADS * GLA_DK
GLA_VAL = GLA_HEADS * GLA_DV
GATE_RANK = 16
GATE_NORMALIZER = 16.0
CHUNK = 64
D_FF = 5632

LANES = 128
SUBLANES_F32 = 8
SUBLANES_BF16 = 16
VMEM_LIMIT = 56 * 1024 * 1024

TM = 768
TMO = 384
TQ = 256
GLA_BLOCK = 256
TF = 512


def _silu(v):
    return v / (1.0 + jnp.exp(-v))


def _row_ids(i, rows):
    return i * rows + lax.broadcasted_iota(jnp.int32, (rows, 1), 0)


def _modulated(x, norm, mod, shift_row, is_ctx):
    ms = jnp.mean(x * x, axis=-1, keepdims=True)
    y = x * lax.rsqrt(ms + EPS) * norm
    s = 2 * shift_row
    shift = jnp.where(is_ctx, mod[s + 1:s + 2, :], mod[s:s + 1, :])
    scale = jnp.where(is_ctx, mod[s + 3:s + 4, :], mod[s + 2:s + 3, :])
    return y * (1.0 + scale) + shift


def _ada_kernel(a_ref, w_ref, b_ref, o_ref):
    a = _silu(a_ref[...]).astype(BF16)
    o_ref[...] = jnp.dot(a, w_ref[...].astype(BF16), preferred_element_type=F32) + b_ref[...]


def _ada(cvec, ada_w, ada_b):
    n = ada_w.shape[1]
    tn = 1024
    return pl.pallas_call(
        _ada_kernel,
        grid=(n // tn,),
        in_specs=[
            pl.BlockSpec((SUBLANES_F32, D_MODEL), lambda j: (0, 0)),
            pl.BlockSpec((D_MODEL, tn), lambda j: (0, j)),
            pl.BlockSpec((1, tn), lambda j: (0, j)),
        ],
        out_specs=pl.BlockSpec((SUBLANES_F32, tn), lambda j: (0, j)),
        out_shape=jax.ShapeDtypeStruct((SUBLANES_F32, n), F32),
        compiler_params=pltpu.CompilerParams(
            dimension_semantics=("arbitrary",), vmem_limit_bytes=VMEM_LIMIT),
        name="ada_mod",
    )(cvec, ada_w, ada_b.reshape(1, n))


def _mod_rows(mods):
    m = mods[:2].reshape(2, N_MOD, D_MODEL)
    m = jnp.transpose(m, (1, 0, 2)).reshape(2 * N_MOD, D_MODEL)
    return jnp.concatenate([m, jnp.zeros((4, D_MODEL), F32)], axis=0)


def _inproj_kernel(x_ref, norm_ref, mod_ref, w_ref, we_ref, z_ref, ze_ref, h_ref):
    i = pl.program_id(0)
    j = pl.program_id(1)

    @pl.when(j == 0)
    def _():
        is_ctx = _row_ids(i, TM) >= SEQ
        h = _modulated(x_ref[...], norm_ref[...], mod_ref[...], 0, is_ctx).astype(BF16)
        h_ref[...] = h
        ze_ref[...] = jnp.dot(h, we_ref[...], preferred_element_type=F32)

    z_ref[...] = jnp.dot(h_ref[...], w_ref[...], preferred_element_type=F32).astype(z_ref.dtype)


def _inproj(xa, norm, mod, w, w_extra, tn):
    n = w.shape[1]
    return pl.pallas_call(
        _inproj_kernel,
        grid=(T_ALL // TM, n // tn),
        in_specs=[
            pl.BlockSpec((TM, D_MODEL), lambda i, j: (i, 0)),
            pl.BlockSpec((1, D_MODEL), lambda i, j: (0, 0)),
            pl.BlockSpec((16, D_MODEL), lambda i, j: (0, 0)),
            pl.BlockSpec((D_MODEL, tn), lambda i, j: (0, j)),
            pl.BlockSpec((D_MODEL, LANES), lambda i, j: (0, 0)),
        ],
        out_specs=[
            pl.BlockSpec((TM, tn), lambda i, j: (i, j)),
            pl.BlockSpec((TM, LANES), lambda i, j: (i, 0)),
        ],
        out_shape=[
            jax.ShapeDtypeStruct((T_ALL, n), BF16),
            jax.ShapeDtypeStruct((T_ALL, LANES), F32),
        ],
        scratch_shapes=[pltpu.VMEM((TM, D_MODEL), BF16)],
        compiler_params=pltpu.CompilerParams(
            dimension_semantics=("parallel", "arbitrary"), vmem_limit_bytes=VMEM_LIMIT),
        name="in_proj",
    )(xa, norm.reshape(1, D_MODEL), mod, w, w_extra)


def _mla_prep_kernel(ql_ref, kvl_ref, kr_ref, cs_ref, qn_ref, wqn_ref, wqr_ref, wqrr_ref,
                     kvn_ref, wk_ref, wv_ref, q_ref, k_ref, v_ref):
    cos = cs_ref[:, :QK_ROPE]
    sin = cs_ref[:, QK_ROPE:]

    ql = ql_ref[...].astype(F32)
    qn = (ql * lax.rsqrt(jnp.mean(ql * ql, axis=-1, keepdims=True) + EPS) * qn_ref[...]).astype(BF16)
    scale = QK_DIM ** -0.5 * LOG2_E
    q_nope = jnp.dot(qn, wqn_ref[...], preferred_element_type=F32) * scale
    q_r = jnp.dot(qn, wqr_ref[...], preferred_element_type=F32)
    q_rr = jnp.dot(qn, wqrr_ref[...], preferred_element_type=F32)

    kvl = kvl_ref[...].astype(F32)
    kvn = (kvl * lax.rsqrt(jnp.mean(kvl * kvl, axis=-1, keepdims=True) + EPS) * kvn_ref[...]).astype(BF16)
    k_nope = jnp.dot(kvn, wk_ref[...], preferred_element_type=F32)
    v = jnp.dot(kvn, wv_ref[...], preferred_element_type=F32)
    kr = kr_ref[...]
    k_rope = (kr[:, :QK_ROPE] * cos + kr[:, QK_ROPE:] * sin).astype(BF16)

    for h in range(MLA_HEADS):
        q_ref[h, :, :QK_NOPE] = q_nope[:, h * QK_NOPE:(h + 1) * QK_NOPE].astype(BF16)
        rope = q_r[:, h * QK_ROPE:(h + 1) * QK_ROPE] * cos + q_rr[:, h * QK_ROPE:(h + 1) * QK_ROPE] * sin
        q_ref[h, :, QK_NOPE:] = (rope * scale).astype(BF16)
        k_ref[h, :, :QK_NOPE] = k_nope[:, h * QK_NOPE:(h + 1) * QK_NOPE].astype(BF16)
        k_ref[h, :, QK_NOPE:] = k_rope
        v_ref[h, :, :V_HEAD] = v[:, h * V_HEAD:(h + 1) * V_HEAD].astype(BF16)
        v_ref[h, :, V_HEAD:] = jnp.ones((v.shape[0], V_HEAD), BF16)


def _mla_prep(z, kr, cs, q_norm, wqn, wqr, wqrr, kv_norm, wk, wv):
    tm = TM
    const = lambda shape: pl.BlockSpec(shape, lambda i: (0,) * len(shape))
    return pl.pallas_call(
        _mla_prep_kernel,
        grid=(T_ALL // tm,),
        in_specs=[
            pl.BlockSpec((tm, Q_LORA), lambda i: (i, 3 * CONV_CH // Q_LORA)),
            pl.BlockSpec((tm, KV_LORA), lambda i: (i, (3 * CONV_CH + Q_LORA) // KV_LORA)),
            pl.BlockSpec((tm, LANES), lambda i: (i, 0)),
            pl.BlockSpec((tm, LANES), lambda i: (i, 0)),
            const((1, Q_LORA)),
            const((Q_LORA, MLA_HEADS * QK_NOPE)),
            const((Q_LORA, MLA_HEADS * QK_ROPE)),
            const((Q_LORA, MLA_HEADS * QK_ROPE)),
            const((1, KV_LORA)),
            const((KV_LORA, MLA_HEADS * QK_NOPE)),
            const((KV_LORA, MLA_HEADS * V_HEAD)),
        ],
        out_specs=[
            pl.BlockSpec((MLA_HEADS, tm, QK_DIM), lambda i: (0, i, 0)),
            pl.BlockSpec((MLA_HEADS, tm, QK_DIM), lambda i: (0, i, 0)),
            pl.BlockSpec((MLA_HEADS, tm, 2 * V_HEAD), lambda i: (0, i, 0)),
        ],
        out_shape=[
            jax.ShapeDtypeStruct((MLA_HEADS, T_ALL, QK_DIM), BF16),
            jax.ShapeDtypeStruct((MLA_HEADS, T_ALL, QK_DIM), BF16),
            jax.ShapeDtypeStruct((MLA_HEADS, T_ALL, 2 * V_HEAD), BF16),
        ],
        compiler_params=pltpu.CompilerParams(
            dimension_semantics=("parallel",), vmem_limit_bytes=VMEM_LIMIT),
        name="mla_prep",
    )(z, z, kr, cs, q_norm.reshape(1, Q_LORA), wqn, wqr, wqrr, kv_norm.reshape(1, KV_LORA), wk, wv)


def _attn_kernel(qa_ref, qn_ref, k_ref, v_ref, o_ref, sa_ref, sb_ref):
    j = pl.program_id(1)
    n_pairs = SEQ // (2 * TQ)

    def score(q, s_ref):
        s_ref[...] = lax.dot_general(q, k_ref[0], (((1,), (1,)), ((), ())), preferred_element_type=F32)

    def attend(s, v, rows):
        p = jnp.exp2(s - jnp.max(s, axis=-1, keepdims=True)).astype(BF16)
        acc = jnp.dot(p, v, preferred_element_type=F32)
        o_ref[rows, :] = (acc[:, :V_HEAD] / acc[:, V_HEAD:V_HEAD + 1]).astype(o_ref.dtype)

    @pl.when(j == 0)
    def _():
        score(qa_ref[0, :TQ, :], sa_ref)

    @pl.when(j < n_pairs)
    def _():
        score(qa_ref[0, TQ:, :], sb_ref)
        attend(sa_ref[...], v_ref[0], slice(0, TQ))
        score(qn_ref[0], sa_ref)
        attend(sb_ref[...], v_ref[0], slice(TQ, 2 * TQ))

    @pl.when(j == n_pairs)
    def _():
        attend(sa_ref[:, SEQ:], v_ref[0, SEQ:, :], slice(0, TQ))
        o_ref[TQ:, :] = jnp.zeros((TQ, V_HEAD), o_ref.dtype)


def _attention(q, k, v):
    n_pairs = SEQ // (2 * TQ)
    return pl.pallas_call(
        _attn_kernel,
        grid=(MLA_HEADS, n_pairs + 1),
        in_specs=[
            pl.BlockSpec((1, 2 * TQ, QK_DIM), lambda h, j: (h, jnp.minimum(j, n_pairs - 1), 0)),
            pl.BlockSpec((1, TQ, QK_DIM), lambda h, j: (h, jnp.minimum(2 * j + 2, SEQ // TQ), 0)),
            pl.BlockSpec((1, T_ALL, QK_DIM), lambda h, j: (h, 0, 0)),
            pl.BlockSpec((1, T_ALL, 2 * V_HEAD), lambda h, j: (h, 0, 0)),
        ],
        out_specs=pl.BlockSpec((2 * TQ, V_HEAD), lambda h, j: (j, h)),
        out_shape=jax.ShapeDtypeStruct(((n_pairs + 1) * 2 * TQ, MLA_HEADS * V_HEAD), BF16),
        scratch_shapes=[pltpu.VMEM((TQ, T_ALL), F32), pltpu.VMEM((TQ, T_ALL), F32)],
        compiler_params=pltpu.CompilerParams(
            dimension_semantics=("parallel", "arbitrary"), vmem_limit_bytes=VMEM_LIMIT),
        name="mla_attn",
    )(q, q, k, v)


def _neighbour_masks(rows):
    has_prev = (rows != 0) & (rows != SEQ)
    has_next = (rows != SEQ - 1) & (rows != T_ALL - 1)
    return has_prev, has_next


def _ab_out_kernel(x_ref, ax_ref, ab_ref, ac_ref, axp_ref, acp_ref, axn_ref, acn_ref, o_ref,
                   cw_ref, mod_ref, w_ref, out_ref, h_ref):
    i = pl.program_id(0)
    j = pl.program_id(1)
    tm = x_ref.shape[0]
    rows = _row_ids(i, tm)

    @pl.when(j == 0)
    def _():
        p = ac_ref[...].astype(F32) * ax_ref[...].astype(F32)
        halo = SUBLANES_BF16
        p_before = acp_ref[halo - 1:halo, :].astype(F32) * axp_ref[halo - 1:halo, :].astype(F32)
        p_after = acn_ref[0:1, :].astype(F32) * axn_ref[0:1, :].astype(F32)
        local = lax.broadcasted_iota(jnp.int32, (tm, 1), 0)
        p_prev = jnp.where(local == 0, p_before, pltpu.roll(p, 1, axis=0))
        p_next = jnp.where(local == tm - 1, p_after, pltpu.roll(p, tm - 1, axis=0))
        has_prev, has_next = _neighbour_masks(rows)
        cw = cw_ref[...]
        conv = (jnp.where(has_prev, p_prev, 0.0) * cw[0:1, :] + p * cw[1:2, :]
                + jnp.where(has_next, p_next, 0.0) * cw[2:3, :])
        h_ref[:, :CONV_CH] = (ab_ref[...].astype(F32) * conv).astype(BF16)
        h_ref[:, CONV_CH:] = o_ref[...]

    y = jnp.dot(h_ref[...], w_ref[...], preferred_element_type=F32)
    mod = mod_ref[...]
    gate = jnp.where(rows >= SEQ, mod[5:6, :], mod[4:5, :])
    out_ref[...] = x_ref[...] + gate * y


def _ab_out(xa, z, o, conv_a, mod, w_out):
    tn = 1024
    tm = TMO
    hb = SUBLANES_BF16
    nb = tm // hb
    last = T_ALL // hb - 1
    prev_map = lambda c: (lambda i, j: (jnp.maximum(i * nb - 1, 0), c))
    next_map = lambda c: (lambda i, j: (jnp.minimum((i + 1) * nb, last), c))
    return pl.pallas_call(
        _ab_out_kernel,
        grid=(T_ALL // tm, D_MODEL // tn),
        in_specs=[
            pl.BlockSpec((tm, tn), lambda i, j: (i, j)),
            pl.BlockSpec((tm, CONV_CH), lambda i, j: (i, 0)),
            pl.BlockSpec((tm, CONV_CH), lambda i, j: (i, 1)),
            pl.BlockSpec((tm, CONV_CH), lambda i, j: (i, 2)),
            pl.BlockSpec((hb, CONV_CH), prev_map(0)),
            pl.BlockSpec((hb, CONV_CH), prev_map(2)),
            pl.BlockSpec((hb, CONV_CH), next_map(0)),
            pl.BlockSpec((hb, CONV_CH), next_map(2)),
            pl.BlockSpec((tm, MLA_HEADS * V_HEAD), lambda i, j: (i, 0)),
            pl.BlockSpec((3, CONV_CH), lambda i, j: (0, 0)),
            pl.BlockSpec((16, tn), lambda i, j: (0, j)),
            pl.BlockSpec((D_MODEL, tn), lambda i, j: (0, j)),
        ],
        out_specs=pl.BlockSpec((tm, tn), lambda i, j: (i, j)),
        out_shape=jax.ShapeDtypeStruct((T_ALL, D_MODEL), F32),
        scratch_shapes=[pltpu.VMEM((tm, D_MODEL), BF16)],
        compiler_params=pltpu.CompilerParams(
            dimension_semantics=("parallel", "arbitrary"), vmem_limit_bytes=VMEM_LIMIT),
        name="ab_out",
    )(xa, z, z, z, z, z, z, z, o, conv_a, mod, w_out)


def _ffn_kernel(x_ref, xp_ref, xn_ref, norm_ref, mod_ref, wg_ref, wv_ref, cwg_ref, cwv_ref,
                cbg_ref, cbv_ref, wd_ref, fnorm_ref, out_ref, h_ref, hh_ref, *, final):
    i = pl.program_id(0)
    j = pl.program_id(1)
    halo = SUBLANES_F32
    rows = _row_ids(i, TM)

    @pl.when(j == 0)
    def _():
        norm = norm_ref[...]
        mod = mod_ref[...]
        h_ref[...] = _modulated(x_ref[...], norm, mod, 3, rows >= SEQ).astype(BF16)
        xh = jnp.concatenate([xp_ref[...], xn_ref[...]], axis=0)
        off = lax.broadcasted_iota(jnp.int32, (2 * halo, 1), 0)
        rows_h = jnp.where(off < halo, i * TM - halo + off, (i + 1) * TM - halo + off)
        hh_ref[...] = _modulated(xh, norm, mod, 3, rows_h >= SEQ).astype(BF16)
        out_ref[...] = jnp.zeros_like(out_ref)

    has_prev, has_next = _neighbour_masks(rows)
    local = lax.broadcasted_iota(jnp.int32, (TM, 1), 0)
    h = h_ref[...]
    hh = hh_ref[...]

    def conv(w_ref, cw_ref, cb_ref):
        w = w_ref[...]
        u = jnp.dot(h, w, preferred_element_type=F32)
        uh = jnp.dot(hh, w, preferred_element_type=F32)
        cw = cw_ref[...]
        u_prev = jnp.where(local == 0, uh[halo - 1:halo, :], pltpu.roll(u, 1, axis=0))
        u_next = jnp.where(local == TM - 1, uh[halo:halo + 1, :], pltpu.roll(u, TM - 1, axis=0))
        return (jnp.where(has_prev, u_prev, 0.0) * cw[0:1, :] + u * cw[1:2, :]
                + jnp.where(has_next, u_next, 0.0) * cw[2:3, :] + cb_ref[...])

    act = (_silu(conv(wg_ref, cwg_ref, cbg_ref)) * conv(wv_ref, cwv_ref, cbv_ref)).astype(BF16)
    out_ref[...] += jnp.dot(act, wd_ref[...], preferred_element_type=F32)

    @pl.when(j == pl.num_programs(1) - 1)
    def _():
        mod = mod_ref[...]
        gate = jnp.where(rows >= SEQ, mod[11:12, :], mod[10:11, :])
        y = x_ref[...] + gate * out_ref[...]
        if final:
            y = y * lax.rsqrt(jnp.mean(y * y, axis=-1, keepdims=True) + EPS) * fnorm_ref[...]
        out_ref[...] = y


def _ffn(xa, norm, mod, w_up, conv_w, conv_b, w_down, final_norm, final):
    nf = D_FF // TF
    hb = SUBLANES_F32
    nb = TM // hb
    last = T_ALL // hb - 1
    out_rows = SEQ if final else T_ALL
    cb = conv_b.reshape(1, 2 * D_FF)
    return pl.pallas_call(
        functools.partial(_ffn_kernel, final=final),
        grid=(T_ALL // TM, nf),
        in_specs=[
            pl.BlockSpec((TM, D_MODEL), lambda i, j: (i, 0), pipeline_mode=pl.Buffered(1)),
            pl.BlockSpec((hb, D_MODEL), lambda i, j: (jnp.maximum(i * nb - 1, 0), 0)),
            pl.BlockSpec((hb, D_MODEL), lambda i, j: (jnp.minimum((i + 1) * nb, last), 0)),
            pl.BlockSpec((1, D_MODEL), lambda i, j: (0, 0)),
            pl.BlockSpec((16, D_MODEL), lambda i, j: (0, 0)),
            pl.BlockSpec((D_MODEL, TF), lambda i, j: (0, j)),
            pl.BlockSpec((D_MODEL, TF), lambda i, j: (0, j + nf)),
            pl.BlockSpec((3, TF), lambda i, j: (0, j)),
            pl.BlockSpec((3, TF), lambda i, j: (0, j + nf)),
            pl.BlockSpec((1, TF), lambda i, j: (0, j)),
            pl.BlockSpec((1, TF), lambda i, j: (0, j + nf)),
            pl.BlockSpec((TF, D_MODEL), lambda i, j: (j, 0)),
            pl.BlockSpec((1, D_MODEL), lambda i, j: (0, 0)),
        ],
        out_specs=pl.BlockSpec((TM, D_MODEL), lambda i, j: (i, 0), pipeline_mode=pl.Buffered(1)),
        out_shape=jax.ShapeDtypeStruct((out_rows, D_MODEL), F32),
        scratch_shapes=[
            pltpu.VMEM((TM, D_MODEL), BF16),
            pltpu.VMEM((2 * hb, D_MODEL), BF16),
        ],
        compiler_params=pltpu.CompilerParams(
            dimension_semantics=("parallel", "arbitrary"), vmem_limit_bytes=VMEM_LIMIT),
        name="conv_ffn_final" if final else "conv_ffn",
    )(xa, xa, xa, norm.reshape(1, D_MODEL), mod, w_up, w_up, conv_w, conv_w, cb, cb, w_down,
      final_norm.reshape(1, D_MODEL))


def _split3(a):
    a1 = a.astype(BF16)
    r1 = a - a1.astype(F32)
    a2 = r1.astype(BF16)
    a3 = (r1 - a2.astype(F32)).astype(BF16)
    return a1, a2, a3


def _gla_direction(q_ref, k_ref, v_ref, lr_ref, gw_ref, gb_ref, o_ref, st_ref, causal):
    n = GLA_BLOCK
    nc = n // CHUNK
    r = lax.broadcasted_iota(jnp.int32, (n, n), 0)
    c = lax.broadcasted_iota(jnp.int32, (n, n), 1)
    shift = CHUNK.bit_length() - 1
    same = jnp.right_shift(r, shift) == jnp.right_shift(c, shift)
    tri = same & ((c <= r) if causal else (c >= r))
    tri_b = jnp.where(tri, 1.0, 0.0).astype(BF16)
    same_b = jnp.where(same, 1.0, 0.0).astype(BF16)

    pre = jnp.dot(lr_ref[...].astype(BF16), gw_ref[...], preferred_element_type=F32) + gb_ref[...]
    g = (jnp.minimum(pre, 0.0) - jnp.log(1.0 + jnp.exp(-jnp.abs(pre)))) / GATE_NORMALIZER
    b = jnp.zeros((n, GLA_DK), F32)
    btot = jnp.zeros((n, GLA_DK), F32)
    for part in _split3(g):
        b = b + jnp.dot(tri_b, part, preferred_element_type=F32)
        btot = btot + jnp.dot(same_b, part, preferred_element_type=F32)

    q = q_ref[...].astype(F32) * (GLA_DK ** -0.5)
    k = k_ref[...].astype(F32)
    v = v_ref[...]
    qe = (q * jnp.exp(b)).astype(BF16)
    ke = (k * jnp.exp(-b)).astype(BF16)
    kd = (k * jnp.exp(btot - b)).astype(BF16)
    att = lax.dot_general(qe, ke, (((1,), (1,)), ((), ())), preferred_element_type=F32)
    att = jnp.where(tri, att, 0.0).astype(BF16)
    intra = jnp.dot(att, v, preferred_element_type=F32)
    dec = jnp.exp(btot)

    order = range(nc) if causal else range(nc - 1, -1, -1)
    for ci in order:
        sl = slice(ci * CHUNK, (ci + 1) * CHUNK)
        st = st_ref[...]
        inter = lax.dot_general(qe[sl], st.astype(BF16), (((1,), (1,)), ((), ())),
                                preferred_element_type=F32)
        o_ref[sl, :] = intra[sl] + inter
        upd = lax.dot_general(v[sl], kd[sl], (((0,), (0,)), ((), ())), preferred_element_type=F32)
        st_ref[...] = st * dec[ci * CHUNK:ci * CHUNK + 1, :] + upd


def _gla_kernel(qf_ref, kf_ref, vf_ref, lrf_ref, qb_ref, kb_ref, vb_ref, lrb_ref,
                gwf_ref, gbf_ref, gwb_ref, gbb_ref, of_ref, ob_ref, sf_ref, sb_ref):
    @pl.when(pl.program_id(1) == 0)
    def _():
        sf_ref[...] = jnp.zeros_like(sf_ref)
        sb_ref[...] = jnp.zeros_like(sb_ref)

    _gla_direction(qf_ref, kf_ref, vf_ref, lrf_ref, gwf_ref, gbf_ref, of_ref, sf_ref, True)
    _gla_direction(qb_ref, kb_ref, vb_ref, lrb_ref, gwb_ref, gbb_ref, ob_ref, sb_ref, False)


def _gla(z, lr, gwf, gbf, gwb, gbb):
    nblk = T_ALL // GLA_BLOCK
    ctx_blk = SEQ // GLA_BLOCK
    fwd = lambda s: jnp.where(s == 0, ctx_blk, s - 1)
    bwd = lambda s: jnp.where(s == 0, ctx_blk, ctx_blk - s)
    kq = GLA_KEY // GLA_DK
    vo = 2 * GLA_KEY // GLA_DV

    def specs(order):
        return [
            pl.BlockSpec((GLA_BLOCK, GLA_DK), lambda h, s: (order(s), kq + h)),
            pl.BlockSpec((GLA_BLOCK, GLA_DK), lambda h, s: (order(s), h)),
            pl.BlockSpec((GLA_BLOCK, GLA_DV), lambda h, s: (order(s), vo + h)),
            pl.BlockSpec((GLA_BLOCK, LANES), lambda h, s: (order(s), 0)),
        ]

    gate_specs = [
        pl.BlockSpec((LANES, GLA_DK), lambda h, s: (0, h)),
        pl.BlockSpec((1, GLA_DK), lambda h, s: (0, h)),
    ]
    return pl.pallas_call(
        _gla_kernel,
        grid=(GLA_HEADS, nblk),
        in_specs=specs(fwd) + specs(bwd) + gate_specs + gate_specs,
        out_specs=[
            pl.BlockSpec((GLA_BLOCK, GLA_DV), lambda h, s: (fwd(s), h)),
            pl.BlockSpec((GLA_BLOCK, GLA_DV), lambda h, s: (bwd(s), h)),
        ],
        out_shape=[
            jax.ShapeDtypeStruct((T_ALL, GLA_VAL), F32),
            jax.ShapeDtypeStruct((T_ALL, GLA_VAL), F32),
        ],
        scratch_shapes=[
            pltpu.VMEM((GLA_DV, GLA_DK), F32),
            pltpu.VMEM((GLA_DV, GLA_DK), F32),
        ],
        compiler_params=pltpu.CompilerParams(
            dimension_semantics=("parallel", "arbitrary"), vmem_limit_bytes=VMEM_LIMIT),
        name="gla_scan",
    )(z, z, z, lr, z, z, z, lr, gwf, gbf.reshape(1, GLA_KEY), gwb, gbb.reshape(1, GLA_KEY))


def _gla_out_kernel(x_ref, of_ref, ob_ref, og_ref, on_ref, mod_ref, w_ref, out_ref, h_ref):
    i = pl.program_id(0)
    j = pl.program_id(1)

    @pl.when(j == 0)
    def _():
        o = of_ref[...] + ob_ref[...]
        on = on_ref[...]
        for hd in range(GLA_HEADS):
            sl = slice(hd * GLA_DV, (hd + 1) * GLA_DV)
            oh = o[:, sl]
            y = oh * lax.rsqrt(jnp.mean(oh * oh, axis=-1, keepdims=True) + EPS) * on
            h_ref[:, sl] = (y * _silu(og_ref[:, sl].astype(F32))).astype(BF16)

    y = jnp.dot(h_ref[...], w_ref[...], preferred_element_type=F32)
    mod = mod_ref[...]
    gate = jnp.where(_row_ids(i, x_ref.shape[0]) >= SEQ, mod[5:6, :], mod[4:5, :])
    out_ref[...] = x_ref[...] + gate * y


def _gla_out(xa, o_f, o_b, z, o_norm, mod, w_out):
    tn = 1024
    tm = TMO
    og_blk = 2 * GLA_KEY // GLA_VAL + 1
    return pl.pallas_call(
        _gla_out_kernel,
        grid=(T_ALL // tm, D_MODEL // tn),
        in_specs=[
            pl.BlockSpec((tm, tn), lambda i, j: (i, j)),
            pl.BlockSpec((tm, GLA_VAL), lambda i, j: (i, 0)),
            pl.BlockSpec((tm, GLA_VAL), lambda i, j: (i, 0)),
            pl.BlockSpec((tm, GLA_VAL), lambda i, j: (i, og_blk)),
            pl.BlockSpec((1, GLA_DV), lambda i, j: (0, 0)),
            pl.BlockSpec((16, tn), lambda i, j: (0, j)),
            pl.BlockSpec((GLA_VAL, tn), lambda i, j: (0, j)),
        ],
        out_specs=pl.BlockSpec((tm, tn), lambda i, j: (i, j)),
        out_shape=jax.ShapeDtypeStruct((T_ALL, D_MODEL), F32),
        scratch_shapes=[pltpu.VMEM((tm, GLA_VAL), BF16)],
        compiler_params=pltpu.CompilerParams(
            dimension_semantics=("parallel", "arbitrary"), vmem_limit_bytes=VMEM_LIMIT),
        name="gla_out",
    )(xa, o_f, o_b, z, o_norm.reshape(1, GLA_DV), mod, w_out)


def _rot_cols(w):
    a1, a2, b1, b2 = jnp.split(w, 4, axis=-1)
    return jnp.concatenate([-a2, a1, -b2, b1], axis=-1)


def _rope_table():
    t = jnp.arange(SEQ)
    row = (t // GRID_W).astype(F32)
    col = (t % GRID_W).astype(F32)
    inv = ROPE_THETA ** (-jnp.arange(0, AXIS_ROPE, 2, dtype=F32) / AXIS_ROPE)
    ar = row[:, None] * inv
    ac = col[:, None] * inv
    ang = jnp.concatenate([ar, ar, ac, ac], axis=-1)
    lat = jnp.concatenate([jnp.cos(ang), jnp.sin(ang)], axis=-1)
    ctx = jnp.concatenate([jnp.ones((CTX_LEN, QK_ROPE), F32), jnp.zeros((CTX_LEN, QK_ROPE), F32)], axis=-1)
    return jnp.concatenate([lat, ctx], axis=0)


def kernel(x, c, ctx, c_ctx, l0_ada_w, l0_ada_b, l0_norm1, l0_w_in, l0_conv_a, l0_q_norm, l0_w_qb, l0_kv_norm, l0_w_kvb, l0_w_out, l0_norm2, l0_ffn_up, l0_ffn_conv_w, l0_ffn_conv_b, l0_ffn_down, l1_ada_w, l1_ada_b, l1_norm1, l1_w_in, l1_gate_fw_w, l1_gate_fw_b, l1_gate_bw_w, l1_gate_bw_b, l1_o_norm, l1_w_out, l1_norm2, l1_ffn_up, l1_ffn_conv_w, l1_ffn_conv_b, l1_ffn_down, final_norm):
    assert x.shape == (1, SEQ, D_MODEL) and ctx.shape == (1, CTX_LEN, D_MODEL)
    xa = jnp.concatenate([x[0], ctx[0]], axis=0)
    cvec = jnp.concatenate([c, c_ctx[None, :], jnp.zeros((SUBLANES_F32 - 2, D_MODEL), F32)], axis=0)
    mod0 = _mod_rows(_ada(cvec, l0_ada_w, l0_ada_b))
    mod1 = _mod_rows(_ada(cvec, l1_ada_w, l1_ada_b))

    kv_lat, k_rope, q_lat, a_x, a_b, a_c = jnp.split(
        l0_w_in, [KV_LORA, KV_LORA + QK_ROPE, KV_LORA + QK_ROPE + Q_LORA,
                  KV_LORA + QK_ROPE + Q_LORA + CONV_CH, KV_LORA + QK_ROPE + Q_LORA + 2 * CONV_CH], axis=1)
    w_in0 = jnp.concatenate([a_x, a_b, a_c, q_lat, kv_lat], axis=1).astype(BF16)
    w_kr = jnp.concatenate([k_rope, _rot_cols(k_rope)], axis=1).astype(BF16)
    z0, kr = _inproj(xa, l0_norm1, mod0, w_in0, w_kr, tn=768)

    wq = l0_w_qb.reshape(Q_LORA, MLA_HEADS, QK_DIM)
    wqn = wq[:, :, :QK_NOPE].reshape(Q_LORA, MLA_HEADS * QK_NOPE).astype(BF16)
    wqr = wq[:, :, QK_NOPE:]
    wqrr = _rot_cols(wqr).reshape(Q_LORA, MLA_HEADS * QK_ROPE).astype(BF16)
    wqr = wqr.reshape(Q_LORA, MLA_HEADS * QK_ROPE).astype(BF16)
    wkv = l0_w_kvb.reshape(KV_LORA, MLA_HEADS, QK_NOPE + V_HEAD)
    wk = wkv[:, :, :QK_NOPE].reshape(KV_LORA, MLA_HEADS * QK_NOPE).astype(BF16)
    wv = wkv[:, :, QK_NOPE:].reshape(KV_LORA, MLA_HEADS * V_HEAD).astype(BF16)
    q, k, v = _mla_prep(z0, kr, _rope_table(), l0_q_norm, wqn, wqr, wqrr, l0_kv_norm, wk, wv)
    o = _attention(q, k, v)
    xa = _ab_out(xa, z0, o, l0_conv_a, mod0, l0_w_out.astype(BF16))
    xa = _ffn(xa, l0_norm2, mod0, l0_ffn_up.astype(BF16), l0_ffn_conv_w, l0_ffn_conv_b,
              l0_ffn_down.astype(BF16), final_norm, final=False)

    wk1, wv1, wlf, wlb, wq1, wog = jnp.split(
        l1_w_in, [GLA_KEY, GLA_KEY + GLA_VAL, GLA_KEY + GLA_VAL + GATE_RANK,
                  GLA_KEY + GLA_VAL + 2 * GATE_RANK, 2 * GLA_KEY + GLA_VAL + 2 * GATE_RANK], axis=1)
    w_in1 = jnp.concatenate([wk1, wq1, wv1, wog], axis=1).astype(BF16)
    w_lr = jnp.concatenate([wlf, wlb, jnp.zeros((D_MODEL, LANES - 2 * GATE_RANK), F32)], axis=1).astype(BF16)
    z1, lr = _inproj(xa, l1_norm1, mod1, w_in1, w_lr, tn=768)
    pad_f = jnp.zeros((LANES - GATE_RANK, GLA_KEY), F32)
    gwf = jnp.concatenate([l1_gate_fw_w, pad_f], axis=0).astype(BF16)
    gwb = jnp.concatenate([jnp.zeros((GATE_RANK, GLA_KEY), F32), l1_gate_bw_w,
                           jnp.zeros((LANES - 2 * GATE_RANK, GLA_KEY), F32)], axis=0).astype(BF16)
    o_f, o_b = _gla(z1, lr, gwf, l1_gate_fw_b, gwb, l1_gate_bw_b)
    xa = _gla_out(xa, o_f, o_b, z1, l1_o_norm, mod1, l1_w_out.astype(BF16))
    out = _ffn(xa, l1_norm2, mod1, l1_ffn_up.astype(BF16), l1_ffn_conv_w, l1_ffn_conv_b,
               l1_ffn_down.astype(BF16), final_norm, final=True)
    return out[None]
```

```python
import functools

import jax
import jax.numpy as jnp
from jax import lax
from jax.experimental import pallas as pl
from jax.experimental.pallas import tpu as pltpu

F32 = jnp.float32
BF16 = jnp.bfloat16

D_MODEL = 2048
SEQ = 8192
GRID_W = 64
CTX_LEN = 256
T_ALL = SEQ + CTX_LEN
EPS = 1e-6
LOG2_E = 1.4426950408889634
N_MOD = 6
MLA_HEADS = 8
QK_NOPE = 128
QK_ROPE = 64
QK_DIM = QK_NOPE + QK_ROPE
V_HEAD = 128
Q_LORA = 512
KV_LORA = 256
ROPE_THETA = 10000.0
AXIS_ROPE = QK_ROPE // 2
CONV_CH = D_MODEL // 2
GLA_HEADS = 4
GLA_DK = D_MODEL // 2 // GLA_HEADS
GLA_DV = D_MODEL // GLA_HEADS
GLA_KEY = GLA_HEADS * GLA_DK
GLA_VAL = GLA_HEADS * GLA_DV
GATE_RANK = 16
GATE_NORMALIZER = 16.0
CHUNK = 64
D_FF = 5632

LANES = 128
SUBLANES_F32 = 8
SUBLANES_BF16 = 16
VMEM_LIMIT = 56 * 1024 * 1024

TM = 768
TMO = 384
ROW_GROUP = 16
TQ = 256
GLA_BLOCK = 256
TF = 512


def _silu(v):
    return v / (1.0 + jnp.exp(-v))


def _row_ids(i, rows):
    return i * rows + lax.broadcasted_iota(jnp.int32, (rows, 1), 0)


def _modulated(x, norm, mod, shift_row, is_ctx):
    s = 2 * shift_row
    gain_lat = norm * (1.0 + mod[s + 2:s + 3, :])
    gain_ctx = norm * (1.0 + mod[s + 3:s + 4, :])
    y = x * lax.rsqrt(jnp.mean(x * x, axis=-1, keepdims=True) + EPS)
    return (y * jnp.where(is_ctx, gain_ctx, gain_lat)
            + jnp.where(is_ctx, mod[s + 1:s + 2, :], mod[s:s + 1, :]))


def _modulate_rows(x_ref, h_ref, norm, mod, shift_row, first_row):
    for r0 in range(0, x_ref.shape[0], ROW_GROUP):
        rows = first_row + r0 + lax.broadcasted_iota(jnp.int32, (ROW_GROUP, 1), 0)
        h = _modulated(x_ref[r0:r0 + ROW_GROUP, :], norm, mod, shift_row, rows >= SEQ)
        h_ref[r0:r0 + ROW_GROUP, :] = h.astype(BF16)


def _ada_kernel(a_ref, w_ref, b_ref, o_ref):
    a = _silu(a_ref[...]).astype(BF16)
    o_ref[...] = jnp.dot(a, w_ref[...].astype(BF16), preferred_element_type=F32) + b_ref[...]


def _ada(cvec, ada_w, ada_b):
    n = ada_w.shape[1]
    tn = 1024
    return pl.pallas_call(
        _ada_kernel,
        grid=(n // tn,),
        in_specs=[
            pl.BlockSpec((SUBLANES_F32, D_MODEL), lambda j: (0, 0)),
            pl.BlockSpec((D_MODEL, tn), lambda j: (0, j)),
            pl.BlockSpec((1, tn), lambda j: (0, j)),
        ],
        out_specs=pl.BlockSpec((SUBLANES_F32, tn), lambda j: (0, j)),
        out_shape=jax.ShapeDtypeStruct((SUBLANES_F32, n), F32),
        compiler_params=pltpu.CompilerParams(
            dimension_semantics=("arbitrary",), vmem_limit_bytes=VMEM_LIMIT),
        name="ada_mod",
    )(cvec, ada_w, ada_b.reshape(1, n))


def _mod_rows(mods):
    m = mods[:2].reshape(2, N_MOD, D_MODEL)
    m = jnp.transpose(m, (1, 0, 2)).reshape(2 * N_MOD, D_MODEL)
    return jnp.concatenate([m, jnp.zeros((4, D_MODEL), F32)], axis=0)


def _inproj_kernel(x_ref, norm_ref, mod_ref, w_ref, we_ref, z_ref, ze_ref, h_ref):
    i = pl.program_id(0)
    j = pl.program_id(1)

    @pl.when(j == 0)
    def _():
        _modulate_rows(x_ref, h_ref, norm_ref[...], mod_ref[...], 0, i * TM)
        ze_ref[...] = jnp.dot(h_ref[...], we_ref[...], preferred_element_type=F32)

    z_ref[...] = jnp.dot(h_ref[...], w_ref[...], preferred_element_type=F32).astype(z_ref.dtype)


def _inproj(xa, norm, mod, w, w_extra, tn):
    n = w.shape[1]
    return pl.pallas_call(
        _inproj_kernel,
        grid=(T_ALL // TM, n // tn),
        in_specs=[
            pl.BlockSpec((TM, D_MODEL), lambda i, j: (i, 0)),
            pl.BlockSpec((1, D_MODEL), lambda i, j: (0, 0)),
            pl.BlockSpec((16, D_MODEL), lambda i, j: (0, 0)),
            pl.BlockSpec((D_MODEL, tn), lambda i, j: (0, j)),
            pl.BlockSpec((D_MODEL, LANES), lambda i, j: (0, 0)),
        ],
        out_specs=[
            pl.BlockSpec((TM, tn), lambda i, j: (i, j)),
            pl.BlockSpec((TM, LANES), lambda i, j: (i, 0)),
        ],
        out_shape=[
            jax.ShapeDtypeStruct((T_ALL, n), BF16),
            jax.ShapeDtypeStruct((T_ALL, LANES), F32),
        ],
        scratch_shapes=[pltpu.VMEM((TM, D_MODEL), BF16)],
        compiler_params=pltpu.CompilerParams(
            dimension_semantics=("parallel", "arbitrary"), vmem_limit_bytes=VMEM_LIMIT),
        name="in_proj",
    )(xa, norm.reshape(1, D_MODEL), mod, w, w_extra)


def _mla_prep_kernel(ql_ref, kvl_ref, kr_ref, cs_ref, qn_ref, wqn_ref, wqr_ref, wqrr_ref,
                     kvn_ref, wk_ref, wv_ref, q_ref, k_ref, v_ref):
    cos = cs_ref[:, :QK_ROPE]
    sin = cs_ref[:, QK_ROPE:]

    ql = ql_ref[...].astype(F32)
    qn = (ql * lax.rsqrt(jnp.mean(ql * ql, axis=-1, keepdims=True) + EPS) * qn_ref[...]).astype(BF16)
    scale = QK_DIM ** -0.5 * LOG2_E
    q_nope = jnp.dot(qn, wqn_ref[...], preferred_element_type=F32) * scale
    q_r = jnp.dot(qn, wqr_ref[...], preferred_element_type=F32)
    q_rr = jnp.dot(qn, wqrr_ref[...], preferred_element_type=F32)

    kvl = kvl_ref[...].astype(F32)
    kvn = (kvl * lax.rsqrt(jnp.mean(kvl * kvl, axis=-1, keepdims=True) + EPS) * kvn_ref[...]).astype(BF16)
    k_nope = jnp.dot(kvn, wk_ref[...], preferred_element_type=F32)
    v = jnp.dot(kvn, wv_ref[...], preferred_element_type=F32)
    kr = kr_ref[...]
    k_rope = (kr[:, :QK_ROPE] * cos + kr[:, QK_ROPE:] * sin).astype(BF16)

    for h in range(MLA_HEADS):
        q_ref[h, :, :QK_NOPE] = q_nope[:, h * QK_NOPE:(h + 1) * QK_NOPE].astype(BF16)
        rope = q_r[:, h * QK_ROPE:(h + 1) * QK_ROPE] * cos + q_rr[:, h * QK_ROPE:(h + 1) * QK_ROPE] * sin
        q_ref[h, :, QK_NOPE:] = (rope * scale).astype(BF16)
        k_ref[h, :, :QK_NOPE] = k_nope[:, h * QK_NOPE:(h + 1) * QK_NOPE].astype(BF16)
        k_ref[h, :, QK_NOPE:] = k_rope
        v_ref[h, :, :V_HEAD] = v[:, h * V_HEAD:(h + 1) * V_HEAD].astype(BF16)
        v_ref[h, :, V_HEAD:] = jnp.ones((v.shape[0], V_HEAD), BF16)


def _mla_prep(z, kr, cs, q_norm, wqn, wqr, wqrr, kv_norm, wk, wv):
    tm = TM
    const = lambda shape: pl.BlockSpec(shape, lambda i: (0,) * len(shape))
    return pl.pallas_call(
        _mla_prep_kernel,
        grid=(T_ALL // tm,),
        in_specs=[
            pl.BlockSpec((tm, Q_LORA), lambda i: (i, 3 * CONV_CH // Q_LORA)),
            pl.BlockSpec((tm, KV_LORA), lambda i: (i, (3 * CONV_CH + Q_LORA) // KV_LORA)),
            pl.BlockSpec((tm, LANES), lambda i: (i, 0)),
            pl.BlockSpec((tm, LANES), lambda i: (i, 0)),
            const((1, Q_LORA)),
            const((Q_LORA, MLA_HEADS * QK_NOPE)),
            const((Q_LORA, MLA_HEADS * QK_ROPE)),
            const((Q_LORA, MLA_HEADS * QK_ROPE)),
            const((1, KV_LORA)),
            const((KV_LORA, MLA_HEADS * QK_NOPE)),
            const((KV_LORA, MLA_HEADS * V_HEAD)),
        ],
        out_specs=[
            pl.BlockSpec((MLA_HEADS, tm, QK_DIM), lambda i: (0, i, 0)),
            pl.BlockSpec((MLA_HEADS, tm, QK_DIM), lambda i: (0, i, 0)),
            pl.BlockSpec((MLA_HEADS, tm, 2 * V_HEAD), lambda i: (0, i, 0)),
        ],
        out_shape=[
            jax.ShapeDtypeStruct((MLA_HEADS, T_ALL, QK_DIM), BF16),
            jax.ShapeDtypeStruct((MLA_HEADS, T_ALL, QK_DIM), BF16),
            jax.ShapeDtypeStruct((MLA_HEADS, T_ALL, 2 * V_HEAD), BF16),
        ],
        compiler_params=pltpu.CompilerParams(
            dimension_semantics=("parallel",), vmem_limit_bytes=VMEM_LIMIT),
        name="mla_prep",
    )(z, z, kr, cs, q_norm.reshape(1, Q_LORA), wqn, wqr, wqrr, kv_norm.reshape(1, KV_LORA), wk, wv)


def _attn_kernel(qa_ref, qn_ref, k_ref, v_ref, o_ref, sa_ref, sb_ref):
    j = pl.program_id(1)
    n_pairs = SEQ // (2 * TQ)

    def score(q, s_ref):
        s_ref[...] = lax.dot_general(q, k_ref[0], (((1,), (1,)), ((), ())), preferred_element_type=F32)

    def attend(s, v, rows):
        p = jnp.exp2(s - jnp.max(s, axis=-1, keepdims=True)).astype(BF16)
        acc = jnp.dot(p, v, preferred_element_type=F32)
        o_ref[rows, :] = (acc[:, :V_HEAD] / acc[:, V_HEAD:V_HEAD + 1]).astype(o_ref.dtype)

    @pl.when(j == 0)
    def _():
        score(qa_ref[0, :TQ, :], sa_ref)

    @pl.when(j < n_pairs)
    def _():
        score(qa_ref[0, TQ:, :], sb_ref)
        attend(sa_ref[...], v_ref[0], slice(0, TQ))
        score(qn_ref[0], sa_ref)
        attend(sb_ref[...], v_ref[0], slice(TQ, 2 * TQ))

    @pl.when(j == n_pairs)
    def _():
        attend(sa_ref[:, SEQ:], v_ref[0, SEQ:, :], slice(0, TQ))
        o_ref[TQ:, :] = jnp.zeros((TQ, V_HEAD), o_ref.dtype)


def _attention(q, k, v):
    n_pairs = SEQ // (2 * TQ)
    return pl.pallas_call(
        _attn_kernel,
        grid=(MLA_HEADS, n_pairs + 1),
        in_specs=[
            pl.BlockSpec((1, 2 * TQ, QK_DIM), lambda h, j: (h, jnp.minimum(j, n_pairs - 1), 0)),
            pl.BlockSpec((1, TQ, QK_DIM), lambda h, j: (h, jnp.minimum(2 * j + 2, SEQ // TQ), 0)),
            pl.BlockSpec((1, T_ALL, QK_DIM), lambda h, j: (h, 0, 0)),
            pl.BlockSpec((1, T_ALL, 2 * V_HEAD), lambda h, j: (h, 0, 0)),
        ],
        out_specs=pl.BlockSpec((2 * TQ, V_HEAD), lambda h, j: (j, h)),
        out_shape=jax.ShapeDtypeStruct(((n_pairs + 1) * 2 * TQ, MLA_HEADS * V_HEAD), BF16),
        scratch_shapes=[pltpu.VMEM((TQ, T_ALL), F32), pltpu.VMEM((TQ, T_ALL), F32)],
        compiler_params=pltpu.CompilerParams(
            dimension_semantics=("parallel", "arbitrary"), vmem_limit_bytes=VMEM_LIMIT),
        name="mla_attn",
    )(q, q, k, v)


def _neighbour_masks(rows):
    has_prev = (rows != 0) & (rows != SEQ)
    has_next = (rows != SEQ - 1) & (rows != T_ALL - 1)
    return has_prev, has_next


def _ab_out_kernel(x_ref, ax_ref, ab_ref, ac_ref, axp_ref, acp_ref, axn_ref, acn_ref, o_ref,
                   cw_ref, mod_ref, w_ref, out_ref, h_ref, p_ref):
    i = pl.program_id(0)
    tm = x_ref.shape[0]
    halo = SUBLANES_F32
    hb = SUBLANES_BF16
    f32 = lambda ref, r0, n: ref[r0:r0 + n, :].astype(F32)

    p_ref[halo - 1:halo, :] = f32(acp_ref, hb - 1, 1) * f32(axp_ref, hb - 1, 1)
    p_ref[halo + tm:halo + tm + 1, :] = f32(acn_ref, 0, 1) * f32(axn_ref, 0, 1)
    for r0 in range(0, tm, ROW_GROUP):
        p_ref[halo + r0:halo + r0 + ROW_GROUP, :] = f32(ac_ref, r0, ROW_GROUP) * f32(ax_ref, r0, ROW_GROUP)

    cw = cw_ref[...]
    for r0 in range(0, tm, ROW_GROUP):
        rows = i * tm + r0 + lax.broadcasted_iota(jnp.int32, (ROW_GROUP, 1), 0)
        has_prev, has_next = _neighbour_masks(rows)
        at = lambda shift: p_ref[halo + r0 + shift:halo + r0 + shift + ROW_GROUP, :]
        conv = (jnp.where(has_prev, at(-1), 0.0) * cw[0:1, :] + at(0) * cw[1:2, :]
                + jnp.where(has_next, at(1), 0.0) * cw[2:3, :])
        h_ref[r0:r0 + ROW_GROUP, :CONV_CH] = (f32(ab_ref, r0, ROW_GROUP) * conv).astype(BF16)
    h_ref[:, CONV_CH:] = o_ref[...]

    y = jnp.dot(h_ref[...], w_ref[...], preferred_element_type=F32)
    mod = mod_ref[...]
    gate = jnp.where(_row_ids(i, tm) >= SEQ, mod[5:6, :], mod[4:5, :])
    out_ref[...] = x_ref[...] + gate * y


def _ab_out(xa, z, o, conv_a, mod, w_out):
    tm = TMO
    hb = SUBLANES_BF16
    nb = tm // hb
    last = T_ALL // hb - 1
    prev_map = lambda c: (lambda i: (jnp.maximum(i * nb - 1, 0), c))
    next_map = lambda c: (lambda i: (jnp.minimum((i + 1) * nb, last), c))
    return pl.pallas_call(
        _ab_out_kernel,
        grid=(T_ALL // tm,),
        in_specs=[
            pl.BlockSpec((tm, D_MODEL), lambda i: (i, 0)),
            pl.BlockSpec((tm, CONV_CH), lambda i: (i, 0)),
            pl.BlockSpec((tm, CONV_CH), lambda i: (i, 1)),
            pl.BlockSpec((tm, CONV_CH), lambda i: (i, 2)),
            pl.BlockSpec((hb, CONV_CH), prev_map(0)),
            pl.BlockSpec((hb, CONV_CH), prev_map(2)),
            pl.BlockSpec((hb, CONV_CH), next_map(0)),
            pl.BlockSpec((hb, CONV_CH), next_map(2)),
            pl.BlockSpec((tm, MLA_HEADS * V_HEAD), lambda i: (i, 0)),
            pl.BlockSpec((3, CONV_CH), lambda i: (0, 0)),
            pl.BlockSpec((16, D_MODEL), lambda i: (0, 0)),
            pl.BlockSpec((D_MODEL, D_MODEL), lambda i: (0, 0), pipeline_mode=pl.Buffered(1)),
        ],
        out_specs=pl.BlockSpec((tm, D_MODEL), lambda i: (i, 0)),
        out_shape=jax.ShapeDtypeStruct((T_ALL, D_MODEL), F32),
        scratch_shapes=[
            pltpu.VMEM((tm, D_MODEL), BF16),
            pltpu.VMEM((tm + 2 * SUBLANES_F32, CONV_CH), F32),
        ],
        compiler_params=pltpu.CompilerParams(
            dimension_semantics=("parallel",), vmem_limit_bytes=VMEM_LIMIT),
        name="ab_out",
    )(xa, z, z, z, z, z, z, z, o, conv_a, mod, w_out)


def _ffn_kernel(x_ref, xp_ref, xn_ref, norm_ref, mod_ref, wg_ref, wv_ref, cwg_ref, cwv_ref,
                cbg_ref, cbv_ref, wd_ref, fnorm_ref, out_ref, h_ref, hh_ref, *, final):
    i = pl.program_id(0)
    j = pl.program_id(1)
    halo = SUBLANES_F32
    rows = _row_ids(i, TM)

    @pl.when(j == 0)
    def _():
        norm = norm_ref[...]
        mod = mod_ref[...]
        _modulate_rows(x_ref, h_ref, norm, mod, 3, i * TM)
        xh = jnp.concatenate([xp_ref[...], xn_ref[...]], axis=0)
        off = lax.broadcasted_iota(jnp.int32, (2 * halo, 1), 0)
        rows_h = jnp.where(off < halo, i * TM - halo + off, (i + 1) * TM - halo + off)
        hh_ref[...] = _modulated(xh, norm, mod, 3, rows_h >= SEQ).astype(BF16)
        out_ref[...] = jnp.zeros_like(out_ref)

    has_prev, has_next = _neighbour_masks(rows)
    local = lax.broadcasted_iota(jnp.int32, (TM, 1), 0)
    h = h_ref[...]
    hh = hh_ref[...]

    def conv(w_ref, cw_ref, cb_ref):
        w = w_ref[...]
        u = jnp.dot(h, w, preferred_element_type=F32)
        uh = jnp.dot(hh, w, preferred_element_type=F32)
        cw = cw_ref[...]
        u_prev = jnp.where(local == 0, uh[halo - 1:halo, :], pltpu.roll(u, 1, axis=0))
        u_next = jnp.where(local == TM - 1, uh[halo:halo + 1, :], pltpu.roll(u, TM - 1, axis=0))
        return (jnp.where(has_prev, u_prev, 0.0) * cw[0:1, :] + u * cw[1:2, :]
                + jnp.where(has_next, u_next, 0.0) * cw[2:3, :] + cb_ref[...])

    act = (_silu(conv(wg_ref, cwg_ref, cbg_ref)) * conv(wv_ref, cwv_ref, cbv_ref)).astype(BF16)
    out_ref[...] += jnp.dot(act, wd_ref[...], preferred_element_type=F32)

    @pl.when(j == pl.num_programs(1) - 1)
    def _():
        mod = mod_ref[...]
        gate = jnp.where(rows >= SEQ, mod[11:12, :], mod[10:11, :])
        y = x_ref[...] + gate * out_ref[...]
        if final:
            y = y * lax.rsqrt(jnp.mean(y * y, axis=-1, keepdims=True) + EPS) * fnorm_ref[...]
        out_ref[...] = y


def _ffn(xa, norm, mod, w_up, conv_w, conv_b, w_down, final_norm, final):
    nf = D_FF // TF
    hb = SUBLANES_F32
    nb = TM // hb
    last = T_ALL // hb - 1
    out_rows = SEQ if final else T_ALL
    cb = conv_b.reshape(1, 2 * D_FF)
    return pl.pallas_call(
        functools.partial(_ffn_kernel, final=final),
        grid=(T_ALL // TM, nf),
        in_specs=[
            pl.BlockSpec((TM, D_MODEL), lambda i, j: (i, 0), pipeline_mode=pl.Buffered(1)),
            pl.BlockSpec((hb, D_MODEL), lambda i, j: (jnp.maximum(i * nb - 1, 0), 0)),
            pl.BlockSpec((hb, D_MODEL), lambda i, j: (jnp.minimum((i + 1) * nb, last), 0)),
            pl.BlockSpec((1, D_MODEL), lambda i, j: (0, 0)),
            pl.BlockSpec((16, D_MODEL), lambda i, j: (0, 0)),
            pl.BlockSpec((D_MODEL, TF), lambda i, j: (0, j)),
            pl.BlockSpec((D_MODEL, TF), lambda i, j: (0, j + nf)),
            pl.BlockSpec((3, TF), lambda i, j: (0, j)),
            pl.BlockSpec((3, TF), lambda i, j: (0, j + nf)),
            pl.BlockSpec((1, TF), lambda i, j: (0, j)),
            pl.BlockSpec((1, TF), lambda i, j: (0, j + nf)),
            pl.BlockSpec((TF, D_MODEL), lambda i, j: (j, 0)),
            pl.BlockSpec((1, D_MODEL), lambda i, j: (0, 0)),
        ],
        out_specs=pl.BlockSpec((TM, D_MODEL), lambda i, j: (i, 0), pipeline_mode=pl.Buffered(1)),
        out_shape=jax.ShapeDtypeStruct((out_rows, D_MODEL), F32),
        scratch_shapes=[
            pltpu.VMEM((TM, D_MODEL), BF16),
            pltpu.VMEM((2 * hb, D_MODEL), BF16),
        ],
        compiler_params=pltpu.CompilerParams(
            dimension_semantics=("parallel", "arbitrary"), vmem_limit_bytes=VMEM_LIMIT),
        name="conv_ffn_final" if final else "conv_ffn",
    )(xa, xa, xa, norm.reshape(1, D_MODEL), mod, w_up, w_up, conv_w, conv_w, cb, cb, w_down,
      final_norm.reshape(1, D_MODEL))


def _per_chunk(rows):
    return jnp.concatenate([jnp.broadcast_to(t, (CHUNK, t.shape[-1])) for t in rows], axis=0)


def _gla_direction(q_ref, k_ref, v_ref, lr_ref, gw_ref, gb_ref, o_ref, st_ref, causal):
    n = GLA_BLOCK
    nc = n // CHUNK
    nt = (((1,), (1,)), ((), ()))
    r = lax.broadcasted_iota(jnp.int32, (n, n), 0)
    c = lax.broadcasted_iota(jnp.int32, (n, n), 1)
    shift = CHUNK.bit_length() - 1
    rc = jnp.right_shift(r, shift)
    cc = jnp.right_shift(c, shift)
    tri = (c <= r) if causal else (c >= r)
    delta = (rc - cc) if causal else (cc - rc)

    pre = jnp.dot(lr_ref[...].astype(BF16), gw_ref[...], preferred_element_type=F32) + gb_ref[...]
    g = (jnp.minimum(pre, 0.0) - jnp.log(1.0 + jnp.exp(-jnp.abs(pre)))) / GATE_NORMALIZER
    tri_b = jnp.where(tri, 1.0, 0.0).astype(BF16)
    g_hi = g.astype(BF16)
    g_lo = (g - g_hi.astype(F32)).astype(BF16)
    bb = (jnp.dot(tri_b, g_hi, preferred_element_type=F32)
          + jnp.dot(tri_b, g_lo, preferred_element_type=F32))

    zero = jnp.zeros((1, GLA_KEY), F32)
    if causal:
        edge = [zero] + [bb[ci * CHUNK - 1:ci * CHUNK, :] for ci in range(1, nc)]
        tot = [bb[ci * CHUNK + CHUNK - 1:ci * CHUNK + CHUNK, :] - edge[ci] for ci in range(nc)]
        ball = bb[n - 1:n, :]
        near = [zero] + tot[:nc - 1]
        far = [zero, zero] + tot[:nc - 2]
    else:
        edge = [bb[(ci + 1) * CHUNK:(ci + 1) * CHUNK + 1, :] for ci in range(nc - 1)] + [zero]
        tot = [bb[ci * CHUNK:ci * CHUNK + 1, :] - edge[ci] for ci in range(nc)]
        ball = bb[0:1, :]
        near = tot[1:] + [zero]
        far = tot[2:] + [zero, zero]
    b = bb - _per_chunk(edge)
    skip1 = _per_chunk([jnp.exp(t) for t in near])
    skip2 = _per_chunk([jnp.exp(t1 + t2) for t1, t2 in zip(near, far)])
    to_chunk_end = _per_chunk([jnp.exp(t) for t in tot])
    from_block_edge = _per_chunk([jnp.exp(e) for e in edge])
    to_block_end = _per_chunk([jnp.exp(ball - e - t) for e, t in zip(edge, tot)])
    dec_all = jnp.exp(ball)

    q = q_ref[...].astype(F32) * (GLA_DK ** -0.5)
    k = k_ref[...].astype(F32)
    qe_f = q * jnp.exp(b)
    ke_f = k * jnp.exp(-b)
    kd_f = ke_f * to_chunk_end
    qe = qe_f.astype(BF16)
    ke = ke_f.astype(BF16)
    kd = kd_f.astype(BF16)
    q1 = (qe_f * skip1).astype(BF16)
    q2 = (qe_f * skip2).astype(BF16)
    q_in = (qe_f * from_block_edge).astype(BF16)
    k_out = (kd_f * to_block_end).astype(BF16)

    for hd in range(GLA_HEADS):
        ks = slice(hd * GLA_DK, (hd + 1) * GLA_DK)
        v = v_ref[:, hd * GLA_DV:(hd + 1) * GLA_DV]
        q_far = jnp.concatenate([qe[:, ks], q1[:, ks], q2[:, ks]], axis=0)
        a0 = lax.dot_general(qe[:, ks], ke[:, ks], nt, preferred_element_type=F32)
        a123 = lax.dot_general(q_far, kd[:, ks], nt, preferred_element_type=F32)
        att = jnp.where((delta == 0) & tri, a0, 0.0)
        for d in range(1, nc):
            att = jnp.where(delta == d, a123[(d - 1) * n:d * n, :], att)
        st = st_ref[hd]
        o_ref[:, hd * GLA_DV:(hd + 1) * GLA_DV] = (
            jnp.dot(att.astype(BF16), v, preferred_element_type=F32)
            + lax.dot_general(q_in[:, ks], st.astype(BF16), nt, preferred_element_type=F32))
        upd = lax.dot_general(v, k_out[:, ks], (((0,), (0,)), ((), ())), preferred_element_type=F32)
        st_ref[hd] = st * dec_all[:, ks] + upd


def _gla_kernel(qf_ref, kf_ref, vf_ref, lrf_ref, qb_ref, kb_ref, vb_ref, lrb_ref,
                gwf_ref, gbf_ref, gwb_ref, gbb_ref, of_ref, ob_ref, sf_ref, sb_ref):
    @pl.when(pl.program_id(0) == 0)
    def _():
        sf_ref[...] = jnp.zeros_like(sf_ref)
        sb_ref[...] = jnp.zeros_like(sb_ref)

    _gla_direction(qf_ref, kf_ref, vf_ref, lrf_ref, gwf_ref, gbf_ref, of_ref, sf_ref, True)
    _gla_direction(qb_ref, kb_ref, vb_ref, lrb_ref, gwb_ref, gbb_ref, ob_ref, sb_ref, False)


def _gla(z, lr, gwf, gbf, gwb, gbb):
    nblk = T_ALL // GLA_BLOCK
    ctx_blk = SEQ // GLA_BLOCK
    fwd = lambda s: jnp.where(s == 0, ctx_blk, s - 1)
    bwd = lambda s: jnp.where(s == 0, ctx_blk, ctx_blk - s)

    def specs(order):
        return [
            pl.BlockSpec((GLA_BLOCK, GLA_KEY), lambda s: (order(s), 1)),
            pl.BlockSpec((GLA_BLOCK, GLA_KEY), lambda s: (order(s), 0)),
            pl.BlockSpec((GLA_BLOCK, GLA_VAL), lambda s: (order(s), 2 * GLA_KEY // GLA_VAL)),
            pl.BlockSpec((GLA_BLOCK, LANES), lambda s: (order(s), 0)),
        ]

    gate_specs = [
        pl.BlockSpec((LANES, GLA_KEY), lambda s: (0, 0)),
        pl.BlockSpec((1, GLA_KEY), lambda s: (0, 0)),
    ]
    return pl.pallas_call(
        _gla_kernel,
        grid=(nblk,),
        in_specs=specs(fwd) + specs(bwd) + gate_specs + gate_specs,
        out_specs=[
            pl.BlockSpec((GLA_BLOCK, GLA_VAL), lambda s: (fwd(s), 0)),
            pl.BlockSpec((GLA_BLOCK, GLA_VAL), lambda s: (bwd(s), 0)),
        ],
        out_shape=[
            jax.ShapeDtypeStruct((T_ALL, GLA_VAL), F32),
            jax.ShapeDtypeStruct((T_ALL, GLA_VAL), F32),
        ],
        scratch_shapes=[
            pltpu.VMEM((GLA_HEADS, GLA_DV, GLA_DK), F32),
            pltpu.VMEM((GLA_HEADS, GLA_DV, GLA_DK), F32),
        ],
        compiler_params=pltpu.CompilerParams(
            dimension_semantics=("arbitrary",), vmem_limit_bytes=VMEM_LIMIT),
        name="gla_scan",
    )(z, z, z, lr, z, z, z, lr, gwf, gbf.reshape(1, GLA_KEY), gwb, gbb.reshape(1, GLA_KEY))


def _gla_out_kernel(x_ref, of_ref, ob_ref, og_ref, on_ref, mod_ref, w_ref, out_ref, h_ref):
    i = pl.program_id(0)
    tm = x_ref.shape[0]
    on = on_ref[...]
    for r0 in range(0, tm, ROW_GROUP):
        rs = slice(r0, r0 + ROW_GROUP)
        for hd in range(GLA_HEADS):
            sl = slice(hd * GLA_DV, (hd + 1) * GLA_DV)
            oh = of_ref[rs, sl] + ob_ref[rs, sl]
            y = oh * lax.rsqrt(jnp.mean(oh * oh, axis=-1, keepdims=True) + EPS) * on
            h_ref[rs, sl] = (y * _silu(og_ref[rs, sl].astype(F32))).astype(BF16)

    y = jnp.dot(h_ref[...], w_ref[...], preferred_element_type=F32)
    mod = mod_ref[...]
    gate = jnp.where(_row_ids(i, tm) >= SEQ, mod[5:6, :], mod[4:5, :])
    out_ref[...] = x_ref[...] + gate * y


def _gla_out(xa, o_f, o_b, z, o_norm, mod, w_out):
    tm = TMO
    og_blk = 2 * GLA_KEY // GLA_VAL + 1
    return pl.pallas_call(
        _gla_out_kernel,
        grid=(T_ALL // tm,),
        in_specs=[
            pl.BlockSpec((tm, D_MODEL), lambda i: (i, 0)),
            pl.BlockSpec((tm, GLA_VAL), lambda i: (i, 0)),
            pl.BlockSpec((tm, GLA_VAL), lambda i: (i, 0)),
            pl.BlockSpec((tm, GLA_VAL), lambda i: (i, og_blk)),
            pl.BlockSpec((1, GLA_DV), lambda i: (0, 0)),
            pl.BlockSpec((16, D_MODEL), lambda i: (0, 0)),
            pl.BlockSpec((GLA_VAL, D_MODEL), lambda i: (0, 0), pipeline_mode=pl.Buffered(1)),
        ],
        out_specs=pl.BlockSpec((tm, D_MODEL), lambda i: (i, 0)),
        out_shape=jax.ShapeDtypeStruct((T_ALL, D_MODEL), F32),
        scratch_shapes=[pltpu.VMEM((tm, GLA_VAL), BF16)],
        compiler_params=pltpu.CompilerParams(
            dimension_semantics=("parallel",), vmem_limit_bytes=VMEM_LIMIT),
        name="gla_out",
    )(xa, o_f, o_b, z, o_norm.reshape(1, GLA_DV), mod, w_out)


def _rot_cols(w):
    a1, a2, b1, b2 = jnp.split(w, 4, axis=-1)
    return jnp.concatenate([-a2, a1, -b2, b1], axis=-1)


def _rope_table():
    t = jnp.arange(SEQ)
    row = (t // GRID_W).astype(F32)
    col = (t % GRID_W).astype(F32)
    inv = ROPE_THETA ** (-jnp.arange(0, AXIS_ROPE, 2, dtype=F32) / AXIS_ROPE)
    ar = row[:, None] * inv
    ac = col[:, None] * inv
    ang = jnp.concatenate([ar, ar, ac, ac], axis=-1)
    lat = jnp.concatenate([jnp.cos(ang), jnp.sin(ang)], axis=-1)
    ctx = jnp.concatenate([jnp.ones((CTX_LEN, QK_ROPE), F32), jnp.zeros((CTX_LEN, QK_ROPE), F32)], axis=-1)
    return jnp.concatenate([lat, ctx], axis=0)


def kernel(x, c, ctx, c_ctx, l0_ada_w, l0_ada_b, l0_norm1, l0_w_in, l0_conv_a, l0_q_norm, l0_w_qb, l0_kv_norm, l0_w_kvb, l0_w_out, l0_norm2, l0_ffn_up, l0_ffn_conv_w, l0_ffn_conv_b, l0_ffn_down, l1_ada_w, l1_ada_b, l1_norm1, l1_w_in, l1_gate_fw_w, l1_gate_fw_b, l1_gate_bw_w, l1_gate_bw_b, l1_o_norm, l1_w_out, l1_norm2, l1_ffn_up, l1_ffn_conv_w, l1_ffn_conv_b, l1_ffn_down, final_norm):
    assert x.shape == (1, SEQ, D_MODEL) and ctx.shape == (1, CTX_LEN, D_MODEL)
    xa = jnp.concatenate([x[0], ctx[0]], axis=0)
    cvec = jnp.concatenate([c, c_ctx[None, :], jnp.zeros((SUBLANES_F32 - 2, D_MODEL), F32)], axis=0)
    mod0 = _mod_rows(_ada(cvec, l0_ada_w, l0_ada_b))
    mod1 = _mod_rows(_ada(cvec, l1_ada_w, l1_ada_b))

    kv_lat, k_rope, q_lat, a_x, a_b, a_c = jnp.split(
        l0_w_in, [KV_LORA, KV_LORA + QK_ROPE, KV_LORA + QK_ROPE + Q_LORA,
                  KV_LORA + QK_ROPE + Q_LORA + CONV_CH, KV_LORA + QK_ROPE + Q_LORA + 2 * CONV_CH], axis=1)
    w_in0 = jnp.concatenate([a_x, a_b, a_c, q_lat, kv_lat], axis=1).astype(BF16)
    w_kr = jnp.concatenate([k_rope, _rot_cols(k_rope)], axis=1).astype(BF16)
    z0, kr = _inproj(xa, l0_norm1, mod0, w_in0, w_kr, tn=1280)

    wq = l0_w_qb.reshape(Q_LORA, MLA_HEADS, QK_DIM)
    wqn = wq[:, :, :QK_NOPE].reshape(Q_LORA, MLA_HEADS * QK_NOPE).astype(BF16)
    wqr = wq[:, :, QK_NOPE:]
    wqrr = _rot_cols(wqr).reshape(Q_LORA, MLA_HEADS * QK_ROPE).astype(BF16)
    wqr = wqr.reshape(Q_LORA, MLA_HEADS * QK_ROPE).astype(BF16)
    wkv = l0_w_kvb.reshape(KV_LORA, MLA_HEADS, QK_NOPE + V_HEAD)
    wk = wkv[:, :, :QK_NOPE].reshape(KV_LORA, MLA_HEADS * QK_NOPE).astype(BF16)
    wv = wkv[:, :, QK_NOPE:].reshape(KV_LORA, MLA_HEADS * V_HEAD).astype(BF16)
    q, k, v = _mla_prep(z0, kr, _rope_table(), l0_q_norm, wqn, wqr, wqrr, l0_kv_norm, wk, wv)
    o = _attention(q, k, v)
    xa = _ab_out(xa, z0, o, l0_conv_a, mod0, l0_w_out.astype(BF16))
    xa = _ffn(xa, l0_norm2, mod0, l0_ffn_up.astype(BF16), l0_ffn_conv_w, l0_ffn_conv_b,
              l0_ffn_down.astype(BF16), final_norm, final=False)

    wk1, wv1, wlf, wlb, wq1, wog = jnp.split(
        l1_w_in, [GLA_KEY, GLA_KEY + GLA_VAL, GLA_KEY + GLA_VAL + GATE_RANK,
                  GLA_KEY + GLA_VAL + 2 * GATE_RANK, 2 * GLA_KEY + GLA_VAL + 2 * GATE_RANK], axis=1)
    w_in1 = jnp.concatenate([wk1, wq1, wv1, wog], axis=1).astype(BF16)
    w_lr = jnp.concatenate([wlf, wlb, jnp.zeros((D_MODEL, LANES - 2 * GATE_RANK), F32)], axis=1).astype(BF16)
    z1, lr = _inproj(xa, l1_norm1, mod1, w_in1, w_lr, tn=1536)
    pad_f = jnp.zeros((LANES - GATE_RANK, GLA_KEY), F32)
    gwf = jnp.concatenate([l1_gate_fw_w, pad_f], axis=0).astype(BF16)
    gwb = jnp.concatenate([jnp.zeros((GATE_RANK, GLA_KEY), F32), l1_gate_bw_w,
                           jnp.zeros((LANES - 2 * GATE_RANK, GLA_KEY), F32)], axis=0).astype(BF16)
    o_f, o_b = _gla(z1, lr, gwf, l1_gate_fw_b, gwb, l1_gate_bw_b)
    xa = _gla_out(xa, o_f, o_b, z1, l1_o_norm, mod1, l1_w_out.astype(BF16))
    out = _ffn(xa, l1_norm2, mod1, l1_ffn_up.astype(BF16), l1_ffn_conv_w, l1_ffn_conv_b,
               l1_ffn_down.astype(BF16), final_norm, final=True)
    return out[None]
```

```python
import functools

import jax
import jax.numpy as jnp
import numpy as np
from jax import lax
from jax.experimental import pallas as pl
from jax.experimental.pallas import tpu as pltpu

F32 = jnp.float32
BF16 = jnp.bfloat16

D_MODEL = 2048
SEQ = 8192
GRID_W = 64
CTX_LEN = 256
T_ALL = SEQ + CTX_LEN
EPS = 1e-6
LOG2_E = 1.4426950408889634
N_MOD = 6
MLA_HEADS = 8
QK_NOPE = 128
QK_ROPE = 64
QK_DIM = QK_NOPE + QK_ROPE
V_HEAD = 128
Q_LORA = 512
KV_LORA = 256
ROPE_THETA = 10000.0
AXIS_ROPE = QK_ROPE // 2
CONV_CH = D_MODEL // 2
GLA_HEADS = 4
GLA_DK = D_MODEL // 2 // GLA_HEADS
GLA_DV = D_MODEL // GLA_HEADS
GLA_KEY = GLA_HEADS * GLA_DK
GLA_VAL = GLA_HEADS * GLA_DV
GATE_RANK = 16
GATE_NORMALIZER = 16.0
CHUNK = 64
D_FF = 5632

LANES = 128
SUBLANES_F32 = 8
SUBLANES_BF16 = 16
VMEM_LIMIT = 56 * 1024 * 1024

TM = 768
TMO = 384
ROW_GROUP = 16
TQ = 256
GLA_BLOCK = 256
TF = 512


def _silu(v):
    return v / (1.0 + jnp.exp(-v))


def _row_ids(i, rows):
    return i * rows + lax.broadcasted_iota(jnp.int32, (rows, 1), 0)


def _modulated(x, norm, mod, shift_row, is_ctx):
    s = 2 * shift_row
    gain_lat = norm * (1.0 + mod[s + 2:s + 3, :])
    gain_ctx = norm * (1.0 + mod[s + 3:s + 4, :])
    y = x * lax.rsqrt(jnp.mean(x * x, axis=-1, keepdims=True) + EPS)
    return (y * jnp.where(is_ctx, gain_ctx, gain_lat)
            + jnp.where(is_ctx, mod[s + 1:s + 2, :], mod[s:s + 1, :]))


def _modulate_rows(x_ref, h_ref, norm, mod, shift_row, first_row):
    for r0 in range(0, x_ref.shape[0], ROW_GROUP):
        rows = first_row + r0 + lax.broadcasted_iota(jnp.int32, (ROW_GROUP, 1), 0)
        h = _modulated(x_ref[r0:r0 + ROW_GROUP, :], norm, mod, shift_row, rows >= SEQ)
        h_ref[r0:r0 + ROW_GROUP, :] = h.astype(BF16)


def _ada_kernel(a_ref, w_ref, b_ref, o_ref):
    a = _silu(a_ref[...]).astype(BF16)
    o_ref[...] = jnp.dot(a, w_ref[...].astype(BF16), preferred_element_type=F32) + b_ref[...]


def _ada(cvec, ada_w, ada_b):
    n = ada_w.shape[1]
    tn = 1024
    return pl.pallas_call(
        _ada_kernel,
        grid=(n // tn,),
        in_specs=[
            pl.BlockSpec((SUBLANES_F32, D_MODEL), lambda j: (0, 0)),
            pl.BlockSpec((D_MODEL, tn), lambda j: (0, j)),
            pl.BlockSpec((1, tn), lambda j: (0, j)),
        ],
        out_specs=pl.BlockSpec((SUBLANES_F32, tn), lambda j: (0, j)),
        out_shape=jax.ShapeDtypeStruct((SUBLANES_F32, n), F32),
        compiler_params=pltpu.CompilerParams(
            dimension_semantics=("arbitrary",), vmem_limit_bytes=VMEM_LIMIT),
        name="ada_mod",
    )(cvec, ada_w, ada_b.reshape(1, n))


def _mod_rows(mods):
    m = mods[:2].reshape(2, N_MOD, D_MODEL)
    m = jnp.transpose(m, (1, 0, 2)).reshape(2 * N_MOD, D_MODEL)
    return jnp.concatenate([m, jnp.zeros((4, D_MODEL), F32)], axis=0)


def _inproj_kernel(x_ref, norm_ref, mod_ref, wa_ref, wb_ref, we_ref, z_ref, ze_ref, h_ref, *, na):
    i = pl.program_id(0)
    j = pl.program_id(1)

    @pl.when(j == 0)
    def _():
        _modulate_rows(x_ref, h_ref, norm_ref[...], mod_ref[...], 0, i * TM)
        ze_ref[...] = jnp.dot(h_ref[...], we_ref[...], preferred_element_type=F32)

    @pl.when(j < na)
    def _():
        z_ref[...] = jnp.dot(h_ref[...], wa_ref[...], preferred_element_type=F32).astype(z_ref.dtype)

    @pl.when(j >= na)
    def _():
        z_ref[...] = jnp.dot(h_ref[...], wb_ref[...], preferred_element_type=F32).astype(z_ref.dtype)


def _inproj(xa, norm, mod, w_a, w_b, w_extra, tn):
    na = w_a.shape[1] // tn
    nb = w_b.shape[1] // tn
    n = (na + nb) * tn
    return pl.pallas_call(
        functools.partial(_inproj_kernel, na=na),
        grid=(T_ALL // TM, na + nb),
        in_specs=[
            pl.BlockSpec((TM, D_MODEL), lambda i, j: (i, 0)),
            pl.BlockSpec((1, D_MODEL), lambda i, j: (0, 0)),
            pl.BlockSpec((16, D_MODEL), lambda i, j: (0, 0)),
            pl.BlockSpec((D_MODEL, tn), lambda i, j: (0, jnp.minimum(j, na - 1))),
            pl.BlockSpec((D_MODEL, tn), lambda i, j: (0, jnp.maximum(j - na, 0))),
            pl.BlockSpec((D_MODEL, LANES), lambda i, j: (0, 0)),
        ],
        out_specs=[
            pl.BlockSpec((TM, tn), lambda i, j: (i, j)),
            pl.BlockSpec((TM, LANES), lambda i, j: (i, 0)),
        ],
        out_shape=[
            jax.ShapeDtypeStruct((T_ALL, n), BF16),
            jax.ShapeDtypeStruct((T_ALL, LANES), F32),
        ],
        scratch_shapes=[pltpu.VMEM((TM, D_MODEL), BF16)],
        compiler_params=pltpu.CompilerParams(
            dimension_semantics=("parallel", "arbitrary"), vmem_limit_bytes=VMEM_LIMIT),
        name="in_proj",
    )(xa, norm.reshape(1, D_MODEL), mod, w_a, w_b, w_extra)


def _mla_prep_kernel(ql_ref, kvl_ref, kr_ref, cs_ref, qn_ref, wqn_ref, wqr_ref, wqrr_ref,
                     kvn_ref, wk_ref, wv_ref, q_ref, k_ref, v_ref):
    cos = cs_ref[:, :QK_ROPE]
    sin = cs_ref[:, QK_ROPE:]

    ql = ql_ref[...].astype(F32)
    qn = (ql * lax.rsqrt(jnp.mean(ql * ql, axis=-1, keepdims=True) + EPS) * qn_ref[...]).astype(BF16)
    scale = QK_DIM ** -0.5 * LOG2_E
    q_nope = jnp.dot(qn, wqn_ref[...], preferred_element_type=F32) * scale
    q_r = jnp.dot(qn, wqr_ref[...], preferred_element_type=F32)
    q_rr = jnp.dot(qn, wqrr_ref[...], preferred_element_type=F32)

    kvl = kvl_ref[...].astype(F32)
    kvn = (kvl * lax.rsqrt(jnp.mean(kvl * kvl, axis=-1, keepdims=True) + EPS) * kvn_ref[...]).astype(BF16)
    k_nope = jnp.dot(kvn, wk_ref[...], preferred_element_type=F32)
    v = jnp.dot(kvn, wv_ref[...], preferred_element_type=F32)
    kr = kr_ref[...]
    k_rope = (kr[:, :QK_ROPE] * cos + kr[:, QK_ROPE:] * sin).astype(BF16)

    for h in range(MLA_HEADS):
        q_ref[h, :, :QK_NOPE] = q_nope[:, h * QK_NOPE:(h + 1) * QK_NOPE].astype(BF16)
        rope = q_r[:, h * QK_ROPE:(h + 1) * QK_ROPE] * cos + q_rr[:, h * QK_ROPE:(h + 1) * QK_ROPE] * sin
        q_ref[h, :, QK_NOPE:] = (rope * scale).astype(BF16)
        k_ref[h, :, :QK_NOPE] = k_nope[:, h * QK_NOPE:(h + 1) * QK_NOPE].astype(BF16)
        k_ref[h, :, QK_NOPE:] = k_rope
        v_ref[h, :, :V_HEAD] = v[:, h * V_HEAD:(h + 1) * V_HEAD].astype(BF16)
        v_ref[h, :, V_HEAD:] = jnp.ones((v.shape[0], V_HEAD), BF16)


def _mla_prep(z, kr, cs, q_norm, wqn, wqr, wqrr, kv_norm, wk, wv):
    tm = TM
    const = lambda shape: pl.BlockSpec(shape, lambda i: (0,) * len(shape))
    return pl.pallas_call(
        _mla_prep_kernel,
        grid=(T_ALL // tm,),
        in_specs=[
            pl.BlockSpec((tm, Q_LORA), lambda i: (i, 3 * CONV_CH // Q_LORA)),
            pl.BlockSpec((tm, KV_LORA), lambda i: (i, (3 * CONV_CH + Q_LORA) // KV_LORA)),
            pl.BlockSpec((tm, LANES), lambda i: (i, 0)),
            pl.BlockSpec((tm, LANES), lambda i: (i, 0)),
            const((1, Q_LORA)),
            const((Q_LORA, MLA_HEADS * QK_NOPE)),
            const((Q_LORA, MLA_HEADS * QK_ROPE)),
            const((Q_LORA, MLA_HEADS * QK_ROPE)),
            const((1, KV_LORA)),
            const((KV_LORA, MLA_HEADS * QK_NOPE)),
            const((KV_LORA, MLA_HEADS * V_HEAD)),
        ],
        out_specs=[
            pl.BlockSpec((MLA_HEADS, tm, QK_DIM), lambda i: (0, i, 0)),
            pl.BlockSpec((MLA_HEADS, tm, QK_DIM), lambda i: (0, i, 0)),
            pl.BlockSpec((MLA_HEADS, tm, 2 * V_HEAD), lambda i: (0, i, 0)),
        ],
        out_shape=[
            jax.ShapeDtypeStruct((MLA_HEADS, T_ALL, QK_DIM), BF16),
            jax.ShapeDtypeStruct((MLA_HEADS, T_ALL, QK_DIM), BF16),
            jax.ShapeDtypeStruct((MLA_HEADS, T_ALL, 2 * V_HEAD), BF16),
        ],
        compiler_params=pltpu.CompilerParams(
            dimension_semantics=("parallel",), vmem_limit_bytes=VMEM_LIMIT),
        name="mla_prep",
    )(z, z, kr, cs, q_norm.reshape(1, Q_LORA), wqn, wqr, wqrr, kv_norm.reshape(1, KV_LORA), wk, wv)


def _attn_kernel(qa_ref, qn_ref, k_ref, v_ref, up0_ref, dn0_ref, up1_ref, dn1_ref, win1_ref,
                 o_ref, up0_o, dn0_o, up1_o, dn1_o, wkv_o, wqo_o, wlr_o, sa_ref, sb_ref):
    j = pl.program_id(1)
    n_pairs = SEQ // (2 * TQ)

    def convert_weight_slabs():
        for src, dst in ((up0_ref, up0_o), (dn0_ref, dn0_o), (up1_ref, up1_o), (dn1_ref, dn1_o)):
            dst[...] = src[...].astype(BF16)
        w1 = win1_ref[...]
        kv_cols = GLA_KEY + GLA_VAL
        wkv_o[...] = w1[:, :kv_cols].astype(BF16)
        wqo_o[...] = w1[:, kv_cols + 2 * GATE_RANK:].astype(BF16)
        wlr_o[...] = jnp.concatenate(
            [w1[:, kv_cols:kv_cols + 2 * GATE_RANK],
             jnp.zeros((w1.shape[0], LANES - 2 * GATE_RANK), F32)], axis=1).astype(BF16)

    def score(q, s_ref):
        s_ref[...] = lax.dot_general(q, k_ref[0], (((1,), (1,)), ((), ())), preferred_element_type=F32)

    def attend(s, v, rows):
        p = jnp.exp2(s - jnp.max(s, axis=-1, keepdims=True)).astype(BF16)
        acc = jnp.dot(p, v, preferred_element_type=F32)
        o_ref[rows, :] = (acc[:, :V_HEAD] / acc[:, V_HEAD:V_HEAD + 1]).astype(o_ref.dtype)

    @pl.when(j == 0)
    def _():
        score(qa_ref[0, :TQ, :], sa_ref)

    @pl.when(j < n_pairs)
    def _():
        score(qa_ref[0, TQ:, :], sb_ref)
        attend(sa_ref[...], v_ref[0], slice(0, TQ))
        score(qn_ref[0], sa_ref)
        attend(sb_ref[...], v_ref[0], slice(TQ, 2 * TQ))
        convert_weight_slabs()

    @pl.when(j == n_pairs)
    def _():
        convert_weight_slabs()
        attend(sa_ref[:, SEQ:], v_ref[0, SEQ:, :], slice(0, TQ))
        o_ref[TQ:, :] = jnp.zeros((TQ, V_HEAD), o_ref.dtype)


def _attention(q, k, v, ffn_up0, ffn_down0, ffn_up1, ffn_down1, w_in1):
    n_pairs = SEQ // (2 * TQ)
    n_j = n_pairs + 1

    def slab(n_rows, rows, cols_in, cols_out=None):
        assert n_rows % rows == 0 and MLA_HEADS * n_j >= n_rows // rows
        index = lambda h, j: (jnp.minimum(h * n_j + j, n_rows // rows - 1), 0)
        return (pl.BlockSpec((rows, cols_in), index),
                pl.BlockSpec((rows, cols_in if cols_out is None else cols_out), index))

    up_in, up_out = slab(D_MODEL, SUBLANES_BF16, 2 * D_FF)
    dn_in, dn_out = slab(D_FF, 4 * SUBLANES_BF16, D_MODEL)
    n_in1 = w_in1.shape[1]
    kv_cols = GLA_KEY + GLA_VAL
    w1_in, w1_kv = slab(D_MODEL, SUBLANES_BF16, n_in1, kv_cols)
    _, w1_qo = slab(D_MODEL, SUBLANES_BF16, n_in1, n_in1 - kv_cols - 2 * GATE_RANK)
    _, w1_lr = slab(D_MODEL, SUBLANES_BF16, n_in1, LANES)
    bf = lambda shape: jax.ShapeDtypeStruct(shape, BF16)
    return pl.pallas_call(
        _attn_kernel,
        grid=(MLA_HEADS, n_j),
        in_specs=[
            pl.BlockSpec((1, 2 * TQ, QK_DIM), lambda h, j: (h, jnp.minimum(j, n_pairs - 1), 0)),
            pl.BlockSpec((1, TQ, QK_DIM), lambda h, j: (h, jnp.minimum(2 * j + 2, SEQ // TQ), 0)),
            pl.BlockSpec((1, T_ALL, QK_DIM), lambda h, j: (h, 0, 0)),
            pl.BlockSpec((1, T_ALL, 2 * V_HEAD), lambda h, j: (h, 0, 0)),
            up_in, dn_in, up_in, dn_in, w1_in,
        ],
        out_specs=[
            pl.BlockSpec((2 * TQ, V_HEAD), lambda h, j: (j, h)),
            up_out, dn_out, up_out, dn_out, w1_kv, w1_qo, w1_lr,
        ],
        out_shape=[
            bf((n_j * 2 * TQ, MLA_HEADS * V_HEAD)),
            bf(ffn_up0.shape), bf(ffn_down0.shape), bf(ffn_up1.shape), bf(ffn_down1.shape),
            bf((D_MODEL, kv_cols)), bf((D_MODEL, n_in1 - kv_cols - 2 * GATE_RANK)), bf((D_MODEL, LANES)),
        ],
        scratch_shapes=[pltpu.VMEM((TQ, T_ALL), F32), pltpu.VMEM((TQ, T_ALL), F32)],
        compiler_params=pltpu.CompilerParams(
            dimension_semantics=("arbitrary", "arbitrary"), vmem_limit_bytes=VMEM_LIMIT),
        name="mla_attn",
    )(q, q, k, v, ffn_up0, ffn_down0, ffn_up1, ffn_down1, w_in1)


def _neighbour_masks(rows):
    has_prev = (rows != 0) & (rows != SEQ)
    has_next = (rows != SEQ - 1) & (rows != T_ALL - 1)
    return has_prev, has_next


def _ab_out_kernel(x_ref, ax_ref, ab_ref, ac_ref, axp_ref, acp_ref, axn_ref, acn_ref, o_ref,
                   cw_ref, mod_ref, w_ref, out_ref, h_ref, p_ref):
    i = pl.program_id(0)
    tm = x_ref.shape[0]
    halo = SUBLANES_F32
    hb = SUBLANES_BF16
    f32 = lambda ref, r0, n: ref[r0:r0 + n, :].astype(F32)

    p_ref[halo - 1:halo, :] = f32(acp_ref, hb - 1, 1) * f32(axp_ref, hb - 1, 1)
    p_ref[halo + tm:halo + tm + 1, :] = f32(acn_ref, 0, 1) * f32(axn_ref, 0, 1)
    for r0 in range(0, tm, ROW_GROUP):
        p_ref[halo + r0:halo + r0 + ROW_GROUP, :] = f32(ac_ref, r0, ROW_GROUP) * f32(ax_ref, r0, ROW_GROUP)

    cw = cw_ref[...]
    for r0 in range(0, tm, ROW_GROUP):
        rows = i * tm + r0 + lax.broadcasted_iota(jnp.int32, (ROW_GROUP, 1), 0)
        has_prev, has_next = _neighbour_masks(rows)
        at = lambda shift: p_ref[halo + r0 + shift:halo + r0 + shift + ROW_GROUP, :]
        conv = (jnp.where(has_prev, at(-1), 0.0) * cw[0:1, :] + at(0) * cw[1:2, :]
                + jnp.where(has_next, at(1), 0.0) * cw[2:3, :])
        h_ref[r0:r0 + ROW_GROUP, :CONV_CH] = (f32(ab_ref, r0, ROW_GROUP) * conv).astype(BF16)
    h_ref[:, CONV_CH:] = o_ref[...]

    y = jnp.dot(h_ref[...], w_ref[...], preferred_element_type=F32)
    mod = mod_ref[...]
    gate = jnp.where(_row_ids(i, tm) >= SEQ, mod[5:6, :], mod[4:5, :])
    out_ref[...] = x_ref[...] + gate * y


def _ab_out(xa, z, o, conv_a, mod, w_out):
    tm = TMO
    hb = SUBLANES_BF16
    nb = tm // hb
    last = T_ALL // hb - 1
    prev_map = lambda c: (lambda i: (jnp.maximum(i * nb - 1, 0), c))
    next_map = lambda c: (lambda i: (jnp.minimum((i + 1) * nb, last), c))
    return pl.pallas_call(
        _ab_out_kernel,
        grid=(T_ALL // tm,),
        in_specs=[
            pl.BlockSpec((tm, D_MODEL), lambda i: (i, 0)),
            pl.BlockSpec((tm, CONV_CH), lambda i: (i, 0)),
            pl.BlockSpec((tm, CONV_CH), lambda i: (i, 1)),
            pl.BlockSpec((tm, CONV_CH), lambda i: (i, 2)),
            pl.BlockSpec((hb, CONV_CH), prev_map(0)),
            pl.BlockSpec((hb, CONV_CH), prev_map(2)),
            pl.BlockSpec((hb, CONV_CH), next_map(0)),
            pl.BlockSpec((hb, CONV_CH), next_map(2)),
            pl.BlockSpec((tm, MLA_HEADS * V_HEAD), lambda i: (i, 0)),
            pl.BlockSpec((3, CONV_CH), lambda i: (0, 0)),
            pl.BlockSpec((16, D_MODEL), lambda i: (0, 0)),
            pl.BlockSpec((D_MODEL, D_MODEL), lambda i: (0, 0), pipeline_mode=pl.Buffered(1)),
        ],
        out_specs=pl.BlockSpec((tm, D_MODEL), lambda i: (i, 0)),
        out_shape=jax.ShapeDtypeStruct((T_ALL, D_MODEL), F32),
        scratch_shapes=[
            pltpu.VMEM((tm, D_MODEL), BF16),
            pltpu.VMEM((tm + 2 * SUBLANES_F32, CONV_CH), F32),
        ],
        compiler_params=pltpu.CompilerParams(
            dimension_semantics=("parallel",), vmem_limit_bytes=VMEM_LIMIT),
        name="ab_out",
    )(xa, z, z, z, z, z, z, z, o, conv_a, mod, w_out)


def _ffn_kernel(x_ref, xp_ref, xn_ref, norm_ref, mod_ref, wg_ref, wv_ref, cwg_ref, cwv_ref,
                cbg_ref, cbv_ref, wd_ref, fnorm_ref, out_ref, h_ref, acta_ref, actb_ref,
                *, final):
    i = pl.program_id(0)
    j = pl.program_id(1)
    nf = pl.num_programs(1) - 1
    halo = SUBLANES_F32
    rows = _row_ids(i, TM)

    @pl.when(j == 0)
    def _():
        norm = norm_ref[...]
        mod = mod_ref[...]
        _modulate_rows(x_ref, h_ref, norm, mod, 3, i * TM)
        xh = jnp.concatenate([xp_ref[...], xn_ref[...]], axis=0)
        off = lax.broadcasted_iota(jnp.int32, (2 * halo, 1), 0)
        rows_h = jnp.where(off < halo, i * TM - halo + off, (i + 1) * TM - halo + off)
        h_ref[TM:, :] = _modulated(xh, norm, mod, 3, rows_h >= SEQ).astype(BF16)
        out_ref[...] = jnp.zeros_like(out_ref)

    def conv(w_ref, cw_ref, cb_ref):
        has_prev, has_next = _neighbour_masks(rows)
        local = lax.broadcasted_iota(jnp.int32, (TM, 1), 0)
        w = w_ref[...]
        half = TM // 2
        u_lo = jnp.dot(h_ref[:half, :], w, preferred_element_type=F32)
        u_hi = jnp.dot(h_ref[half:, :], w, preferred_element_type=F32)
        u = jnp.concatenate([u_lo, u_hi[:half, :]], axis=0)
        uh = u_hi[half:, :]
        cw = cw_ref[...]
        u_prev = jnp.where(local == 0, uh[halo - 1:halo, :], pltpu.roll(u, 1, axis=0))
        u_next = jnp.where(local == TM - 1, uh[halo:halo + 1, :], pltpu.roll(u, TM - 1, axis=0))
        return (jnp.where(has_prev, u_prev, 0.0) * cw[0:1, :] + u * cw[1:2, :]
                + jnp.where(has_next, u_next, 0.0) * cw[2:3, :] + cb_ref[...])

    def up(act_ref):
        act_ref[...] = (_silu(conv(wg_ref, cwg_ref, cbg_ref))
                        * conv(wv_ref, cwv_ref, cbv_ref)).astype(BF16)

    def down(act_ref):
        out_ref[...] += jnp.dot(act_ref[...], wd_ref[...], preferred_element_type=F32)

    even = lax.rem(j, 2) == 0
    inner = (j > 0) & (j < nf)

    @pl.when(j == 0)
    def _():
        up(acta_ref)

    @pl.when(inner & even)
    def _():
        up(acta_ref)
        down(actb_ref)

    @pl.when(inner & jnp.logical_not(even))
    def _():
        up(actb_ref)
        down(acta_ref)

    @pl.when(j == nf)
    def _():
        down(acta_ref if (D_FF // TF - 1) % 2 == 0 else actb_ref)
        mod = mod_ref[...]
        gate = jnp.where(rows >= SEQ, mod[11:12, :], mod[10:11, :])
        y = x_ref[...] + gate * out_ref[...]
        if final:
            y = y * lax.rsqrt(jnp.mean(y * y, axis=-1, keepdims=True) + EPS) * fnorm_ref[...]
        out_ref[...] = y


def _ffn(xa, norm, mod, w_up, conv_w, conv_b, w_down, final_norm, final):
    nf = D_FF // TF
    hb = SUBLANES_F32
    nb = TM // hb
    last = T_ALL // hb - 1
    out_rows = SEQ if final else T_ALL
    cb = conv_b.reshape(1, 2 * D_FF)
    up_tile = lambda j: jnp.minimum(j, nf - 1)
    return pl.pallas_call(
        functools.partial(_ffn_kernel, final=final),
        grid=(T_ALL // TM, nf + 1),
        in_specs=[
            pl.BlockSpec((TM, D_MODEL), lambda i, j: (i, 0)),
            pl.BlockSpec((hb, D_MODEL), lambda i, j: (jnp.maximum(i * nb - 1, 0), 0)),
            pl.BlockSpec((hb, D_MODEL), lambda i, j: (jnp.minimum((i + 1) * nb, last), 0)),
            pl.BlockSpec((1, D_MODEL), lambda i, j: (0, 0)),
            pl.BlockSpec((16, D_MODEL), lambda i, j: (0, 0)),
            pl.BlockSpec((D_MODEL, TF), lambda i, j: (0, up_tile(j))),
            pl.BlockSpec((D_MODEL, TF), lambda i, j: (0, up_tile(j) + nf)),
            pl.BlockSpec((3, TF), lambda i, j: (0, up_tile(j))),
            pl.BlockSpec((3, TF), lambda i, j: (0, up_tile(j) + nf)),
            pl.BlockSpec((1, TF), lambda i, j: (0, up_tile(j))),
            pl.BlockSpec((1, TF), lambda i, j: (0, up_tile(j) + nf)),
            pl.BlockSpec((TF, D_MODEL), lambda i, j: (jnp.maximum(j - 1, 0), 0)),
            pl.BlockSpec((1, D_MODEL), lambda i, j: (0, 0)),
        ],
        out_specs=pl.BlockSpec((TM, D_MODEL), lambda i, j: (i, 0)),
        out_shape=jax.ShapeDtypeStruct((out_rows, D_MODEL), F32),
        scratch_shapes=[
            pltpu.VMEM((TM + 2 * hb, D_MODEL), BF16),
            pltpu.VMEM((TM, TF), BF16),
            pltpu.VMEM((TM, TF), BF16),
        ],
        compiler_params=pltpu.CompilerParams(
            dimension_semantics=("parallel", "arbitrary"), vmem_limit_bytes=VMEM_LIMIT),
        name="conv_ffn_final" if final else "conv_ffn",
    )(xa, xa, xa, norm.reshape(1, D_MODEL), mod, w_up, w_up, conv_w, conv_w, cb, cb, w_down,
      final_norm.reshape(1, D_MODEL))


def _per_chunk(rows):
    return jnp.concatenate([jnp.broadcast_to(t, (CHUNK, t.shape[-1])) for t in rows], axis=0)


def _gla_direction(q_ref, k_ref, v_refs, lr_ref, gw_ref, gb_ref, o_ref, st_ref, causal):
    n = GLA_BLOCK
    nc = n // CHUNK
    nt = (((1,), (1,)), ((), ()))
    r = lax.broadcasted_iota(jnp.int32, (n, n), 0)
    c = lax.broadcasted_iota(jnp.int32, (n, n), 1)
    shift = CHUNK.bit_length() - 1
    rc = jnp.right_shift(r, shift)
    cc = jnp.right_shift(c, shift)
    tri = (c <= r) if causal else (c >= r)
    delta = (rc - cc) if causal else (cc - rc)

    pre = jnp.dot(lr_ref[...].astype(BF16), gw_ref[...], preferred_element_type=F32) + gb_ref[...]
    g = (jnp.minimum(pre, 0.0) - jnp.log(1.0 + jnp.exp(-jnp.abs(pre)))) / GATE_NORMALIZER
    tri_b = jnp.where(tri, 1.0, 0.0).astype(BF16)
    g_hi = g.astype(BF16)
    g_lo = (g - g_hi.astype(F32)).astype(BF16)
    bb = (jnp.dot(tri_b, g_hi, preferred_element_type=F32)
          + jnp.dot(tri_b, g_lo, preferred_element_type=F32))

    zero = jnp.zeros((1, GLA_KEY), F32)
    if causal:
        edge = [zero] + [bb[ci * CHUNK - 1:ci * CHUNK, :] for ci in range(1, nc)]
        tot = [bb[ci * CHUNK + CHUNK - 1:ci * CHUNK + CHUNK, :] - edge[ci] for ci in range(nc)]
        ball = bb[n - 1:n, :]
        near = [zero] + tot[:nc - 1]
        far = [zero, zero] + tot[:nc - 2]
    else:
        edge = [bb[(ci + 1) * CHUNK:(ci + 1) * CHUNK + 1, :] for ci in range(nc - 1)] + [zero]
        tot = [bb[ci * CHUNK:ci * CHUNK + 1, :] - edge[ci] for ci in range(nc)]
        ball = bb[0:1, :]
        near = tot[1:] + [zero]
        far = tot[2:] + [zero, zero]
    b = bb - _per_chunk(edge)
    skip1 = _per_chunk([jnp.exp(t) for t in near])
    skip2 = _per_chunk([jnp.exp(t1 + t2) for t1, t2 in zip(near, far)])
    to_chunk_end = _per_chunk([jnp.exp(t) for t in tot])
    from_block_edge = _per_chunk([jnp.exp(e) for e in edge])
    to_block_end = _per_chunk([jnp.exp(ball - e - t) for e, t in zip(edge, tot)])
    dec_all = jnp.exp(ball)

    q = q_ref[...].astype(F32) * (GLA_DK ** -0.5)
    k = k_ref[...].astype(F32)
    qe_f = q * jnp.exp(b)
    ke_f = k * jnp.exp(-b)
    kd_f = ke_f * to_chunk_end
    qe = qe_f.astype(BF16)
    ke = ke_f.astype(BF16)
    kd = kd_f.astype(BF16)
    q1 = (qe_f * skip1).astype(BF16)
    q2 = (qe_f * skip2).astype(BF16)
    q_in = (qe_f * from_block_edge).astype(BF16)
    k_out = (kd_f * to_block_end).astype(BF16)

    for hd in range(GLA_HEADS):
        ks = slice(hd * GLA_DK, (hd + 1) * GLA_DK)
        v_half = v_refs[hd // (GLA_HEADS // 2)]
        v_col = (hd % (GLA_HEADS // 2)) * GLA_DV
        v = v_half[:, v_col:v_col + GLA_DV]
        q_far = jnp.concatenate([qe[:, ks], q1[:, ks], q2[:, ks]], axis=0)
        a0 = lax.dot_general(qe[:, ks], ke[:, ks], nt, preferred_element_type=F32)
        a123 = lax.dot_general(q_far, kd[:, ks], nt, preferred_element_type=F32)
        att = jnp.where((delta == 0) & tri, a0, 0.0)
        for d in range(1, nc):
            att = jnp.where(delta == d, a123[(d - 1) * n:d * n, :], att)
        st = st_ref[hd]
        o_ref[:, hd * GLA_DV:(hd + 1) * GLA_DV] = (
            jnp.dot(att.astype(BF16), v, preferred_element_type=F32)
            + lax.dot_general(q_in[:, ks], st.astype(BF16), nt, preferred_element_type=F32))
        upd = lax.dot_general(v, k_out[:, ks], (((0,), (0,)), ((), ())), preferred_element_type=F32)
        st_ref[hd] = st * dec_all[:, ks] + upd


def _gla_kernel(qf_ref, kf_ref, vf0_ref, vf1_ref, lrf_ref, qb_ref, kb_ref, vb0_ref, vb1_ref, lrb_ref,
                gwf_ref, gbf_ref, gwb_ref, gbb_ref, of_ref, ob_ref, sf_ref, sb_ref):
    @pl.when(pl.program_id(0) == 0)
    def _():
        sf_ref[...] = jnp.zeros_like(sf_ref)
        sb_ref[...] = jnp.zeros_like(sb_ref)

    _gla_direction(qf_ref, kf_ref, (vf0_ref, vf1_ref), lrf_ref, gwf_ref, gbf_ref, of_ref, sf_ref, True)
    _gla_direction(qb_ref, kb_ref, (vb0_ref, vb1_ref), lrb_ref, gwb_ref, gbb_ref, ob_ref, sb_ref, False)


def _gla(z, lr, gwf, gbf, gwb, gbb):
    nblk = T_ALL // GLA_BLOCK
    ctx_blk = SEQ // GLA_BLOCK
    fwd = lambda s: jnp.where(s == 0, ctx_blk, s - 1)
    bwd = lambda s: jnp.where(s == 0, ctx_blk, ctx_blk - s)

    def specs(order):
        return [
            pl.BlockSpec((GLA_BLOCK, GLA_KEY), lambda s: (order(s), 3)),
            pl.BlockSpec((GLA_BLOCK, GLA_KEY), lambda s: (order(s), 0)),
            pl.BlockSpec((GLA_BLOCK, GLA_KEY), lambda s: (order(s), 1)),
            pl.BlockSpec((GLA_BLOCK, GLA_KEY), lambda s: (order(s), 2)),
            pl.BlockSpec((GLA_BLOCK, LANES), lambda s: (order(s), 0)),
        ]

    gate_specs = [
        pl.BlockSpec((LANES, GLA_KEY), lambda s: (0, 0)),
        pl.BlockSpec((1, GLA_KEY), lambda s: (0, 0)),
    ]
    return pl.pallas_call(
        _gla_kernel,
        grid=(nblk,),
        in_specs=specs(fwd) + specs(bwd) + gate_specs + gate_specs,
        out_specs=[
            pl.BlockSpec((GLA_BLOCK, GLA_VAL), lambda s: (fwd(s), 0)),
            pl.BlockSpec((GLA_BLOCK, GLA_VAL), lambda s: (bwd(s), 0)),
        ],
        out_shape=[
            jax.ShapeDtypeStruct((T_ALL, GLA_VAL), F32),
            jax.ShapeDtypeStruct((T_ALL, GLA_VAL), F32),
        ],
        scratch_shapes=[
            pltpu.VMEM((GLA_HEADS, GLA_DV, GLA_DK), F32),
            pltpu.VMEM((GLA_HEADS, GLA_DV, GLA_DK), F32),
        ],
        compiler_params=pltpu.CompilerParams(
            dimension_semantics=("arbitrary",), vmem_limit_bytes=VMEM_LIMIT),
        name="gla_scan",
    )(z, z, z, z, lr, z, z, z, z, lr, gwf, gbf.reshape(1, GLA_KEY), gwb, gbb.reshape(1, GLA_KEY))


def _gla_out_kernel(x_ref, of_ref, ob_ref, og_ref, on_ref, mod_ref, w_ref, out_ref, h_ref):
    i = pl.program_id(0)
    tm = x_ref.shape[0]
    on = on_ref[...]
    for r0 in range(0, tm, ROW_GROUP):
        rs = slice(r0, r0 + ROW_GROUP)
        for hd in range(GLA_HEADS):
            sl = slice(hd * GLA_DV, (hd + 1) * GLA_DV)
            oh = of_ref[rs, sl] + ob_ref[rs, sl]
            y = oh * lax.rsqrt(jnp.mean(oh * oh, axis=-1, keepdims=True) + EPS) * on
            h_ref[rs, sl] = (y * _silu(og_ref[rs, sl].astype(F32))).astype(BF16)

    y = jnp.dot(h_ref[...], w_ref[...], preferred_element_type=F32)
    mod = mod_ref[...]
    gate = jnp.where(_row_ids(i, tm) >= SEQ, mod[5:6, :], mod[4:5, :])
    out_ref[...] = x_ref[...] + gate * y


def _gla_out(xa, o_f, o_b, z, o_norm, mod, w_out):
    tm = TMO
    og_blk = 2 * GLA_KEY // GLA_VAL + 1
    return pl.pallas_call(
        _gla_out_kernel,
        grid=(T_ALL // tm,),
        in_specs=[
            pl.BlockSpec((tm, D_MODEL), lambda i: (i, 0)),
            pl.BlockSpec((tm, GLA_VAL), lambda i: (i, 0)),
            pl.BlockSpec((tm, GLA_VAL), lambda i: (i, 0)),
            pl.BlockSpec((tm, GLA_VAL), lambda i: (i, og_blk)),
            pl.BlockSpec((1, GLA_DV), lambda i: (0, 0)),
            pl.BlockSpec((16, D_MODEL), lambda i: (0, 0)),
            pl.BlockSpec((GLA_VAL, D_MODEL), lambda i: (0, 0), pipeline_mode=pl.Buffered(1)),
        ],
        out_specs=pl.BlockSpec((tm, D_MODEL), lambda i: (i, 0)),
        out_shape=jax.ShapeDtypeStruct((T_ALL, D_MODEL), F32),
        scratch_shapes=[pltpu.VMEM((tm, GLA_VAL), BF16)],
        compiler_params=pltpu.CompilerParams(
            dimension_semantics=("parallel",), vmem_limit_bytes=VMEM_LIMIT),
        name="gla_out",
    )(xa, o_f, o_b, z, o_norm.reshape(1, GLA_DV), mod, w_out)


def _rot_cols(w):
    a1, a2, b1, b2 = jnp.split(w, 4, axis=-1)
    return jnp.concatenate([-a2, a1, -b2, b1], axis=-1)


def _rope_table():
    t = np.arange(SEQ)
    row = (t // GRID_W).astype(np.float32)
    col = (t % GRID_W).astype(np.float32)
    inv = (np.float32(ROPE_THETA) ** (-np.arange(0, AXIS_ROPE, 2, dtype=np.float32) / AXIS_ROPE)).astype(np.float32)
    ar = row[:, None] * inv
    ac = col[:, None] * inv
    ang = np.concatenate([ar, ar, ac, ac], axis=-1).astype(np.float64)
    lat = np.concatenate([np.cos(ang), np.sin(ang)], axis=-1)
    ctx = np.concatenate([np.ones((CTX_LEN, QK_ROPE)), np.zeros((CTX_LEN, QK_ROPE))], axis=-1)
    return jnp.asarray(np.concatenate([lat, ctx], axis=0), dtype=F32)


def kernel(x, c, ctx, c_ctx, l0_ada_w, l0_ada_b, l0_norm1, l0_w_in, l0_conv_a, l0_q_norm, l0_w_qb, l0_kv_norm, l0_w_kvb, l0_w_out, l0_norm2, l0_ffn_up, l0_ffn_conv_w, l0_ffn_conv_b, l0_ffn_down, l1_ada_w, l1_ada_b, l1_norm1, l1_w_in, l1_gate_fw_w, l1_gate_fw_b, l1_gate_bw_w, l1_gate_bw_b, l1_o_norm, l1_w_out, l1_norm2, l1_ffn_up, l1_ffn_conv_w, l1_ffn_conv_b, l1_ffn_down, final_norm):
    assert x.shape == (1, SEQ, D_MODEL) and ctx.shape == (1, CTX_LEN, D_MODEL)
    xa = jnp.concatenate([x[0], ctx[0]], axis=0)
    cvec = jnp.concatenate([c, c_ctx[None, :], jnp.zeros((SUBLANES_F32 - 2, D_MODEL), F32)], axis=0)
    mod0 = _mod_rows(_ada(cvec, l0_ada_w, l0_ada_b))
    mod1 = _mod_rows(_ada(cvec, l1_ada_w, l1_ada_b))

    c_rope, c_q, c_conv = KV_LORA, KV_LORA + QK_ROPE, KV_LORA + QK_ROPE + Q_LORA
    w_conv0 = l0_w_in[:, c_conv:].astype(BF16)
    w_lat0 = jnp.concatenate([l0_w_in[:, c_q:c_conv], l0_w_in[:, :c_rope]], axis=1).astype(BF16)
    k_rope = l0_w_in[:, c_rope:c_q]
    w_kr = jnp.concatenate([k_rope, _rot_cols(k_rope)], axis=1).astype(BF16)
    z0, kr = _inproj(xa, l0_norm1, mod0, w_conv0, w_lat0, w_kr, tn=Q_LORA + KV_LORA)

    wq = l0_w_qb.reshape(Q_LORA, MLA_HEADS, QK_DIM)
    wqn = wq[:, :, :QK_NOPE].reshape(Q_LORA, MLA_HEADS * QK_NOPE).astype(BF16)
    wqr = wq[:, :, QK_NOPE:]
    wqrr = _rot_cols(wqr).reshape(Q_LORA, MLA_HEADS * QK_ROPE).astype(BF16)
    wqr = wqr.reshape(Q_LORA, MLA_HEADS * QK_ROPE).astype(BF16)
    wkv = l0_w_kvb.reshape(KV_LORA, MLA_HEADS, QK_NOPE + V_HEAD)
    wk = wkv[:, :, :QK_NOPE].reshape(KV_LORA, MLA_HEADS * QK_NOPE).astype(BF16)
    wv = wkv[:, :, QK_NOPE:].reshape(KV_LORA, MLA_HEADS * V_HEAD).astype(BF16)
    q, k, v = _mla_prep(z0, kr, _rope_table(), l0_q_norm, wqn, wqr, wqrr, l0_kv_norm, wk, wv)
    o, up0, down0, up1, down1, w_kv1, w_qo1, w_lr = _attention(
        q, k, v, l0_ffn_up, l0_ffn_down, l1_ffn_up, l1_ffn_down, l1_w_in)
    xa = _ab_out(xa, z0, o, l0_conv_a, mod0, l0_w_out.astype(BF16))
    xa = _ffn(xa, l0_norm2, mod0, up0, l0_ffn_conv_w, l0_ffn_conv_b, down0, final_norm, final=False)

    z1, lr = _inproj(xa, l1_norm1, mod1, w_kv1, w_qo1, w_lr, tn=1536)
    pad_f = jnp.zeros((LANES - GATE_RANK, GLA_KEY), F32)
    gwf = jnp.concatenate([l1_gate_fw_w, pad_f], axis=0).astype(BF16)
    gwb = jnp.concatenate([jnp.zeros((GATE_RANK, GLA_KEY), F32), l1_gate_bw_w,
                           jnp.zeros((LANES - 2 * GATE_RANK, GLA_KEY), F32)], axis=0).astype(BF16)
    o_f, o_b = _gla(z1, lr, gwf, l1_gate_fw_b, gwb, l1_gate_bw_b)
    xa = _gla_out(xa, o_f, o_b, z1, l1_o_norm, mod1, l1_w_out.astype(BF16))
    out = _ffn(xa, l1_norm2, mod1, up1, l1_ffn_conv_w, l1_ffn_conv_b, down1, final_norm, final=True)
    return out[None]
```

```python
import functools

import jax
import jax.numpy as jnp
import numpy as np
from jax import lax
from jax.experimental import pallas as pl
from jax.experimental.pallas import tpu as pltpu

F32 = jnp.float32
BF16 = jnp.bfloat16

D_MODEL = 2048
SEQ = 8192
GRID_W = 64
CTX_LEN = 256
T_ALL = SEQ + CTX_LEN
EPS = 1e-6
LOG2_E = 1.4426950408889634
N_MOD = 6
MLA_HEADS = 8
QK_NOPE = 128
QK_ROPE = 64
QK_DIM = QK_NOPE + QK_ROPE
V_HEAD = 128
Q_LORA = 512
KV_LORA = 256
ROPE_THETA = 10000.0
AXIS_ROPE = QK_ROPE // 2
CONV_CH = D_MODEL // 2
GLA_HEADS = 4
GLA_DK = D_MODEL // 2 // GLA_HEADS
GLA_DV = D_MODEL // GLA_HEADS
GLA_KEY = GLA_HEADS * GLA_DK
GLA_VAL = GLA_HEADS * GLA_DV
GATE_RANK = 16
GATE_NORMALIZER = 16.0
CHUNK = 64
D_FF = 5632

LANES = 128
SUBLANES_F32 = 8
SUBLANES_BF16 = 16
VMEM_LIMIT = 56 * 1024 * 1024

TM = 768
TMO = 384
ROW_GROUP = 16
TQ = 256
GLA_BLOCK = 256
TF = 512


def _silu(v):
    return v / (1.0 + jnp.exp(-v))


def _row_ids(i, rows):
    return i * rows + lax.broadcasted_iota(jnp.int32, (rows, 1), 0)


def _modulated(x, norm, mod, shift_row, is_ctx):
    s = 2 * shift_row
    gain_lat = norm * (1.0 + mod[s + 2:s + 3, :])
    gain_ctx = norm * (1.0 + mod[s + 3:s + 4, :])
    y = x * lax.rsqrt(jnp.mean(x * x, axis=-1, keepdims=True) + EPS)
    return (y * jnp.where(is_ctx, gain_ctx, gain_lat)
            + jnp.where(is_ctx, mod[s + 1:s + 2, :], mod[s:s + 1, :]))


def _token_rows(x_ref, ctx_ref, is_ctx_tile, r0, n):
    rows = x_ref[r0:r0 + n, :]
    tile_rows = x_ref.shape[0]
    ctx_start = SEQ % tile_rows
    if ctx_ref is None or r0 + n <= ctx_start:
        return rows
    assert r0 >= ctx_start and ctx_start + CTX_LEN == tile_rows
    return jnp.where(is_ctx_tile, ctx_ref[r0 - ctx_start:r0 - ctx_start + n, :], rows)


def _modulate_rows(x_ref, h_ref, norm, mod, shift_row, first_row, ctx_ref=None):
    is_ctx_tile = first_row == (SEQ // x_ref.shape[0]) * x_ref.shape[0]
    for r0 in range(0, x_ref.shape[0], ROW_GROUP):
        rows = first_row + r0 + lax.broadcasted_iota(jnp.int32, (ROW_GROUP, 1), 0)
        x = _token_rows(x_ref, ctx_ref, is_ctx_tile, r0, ROW_GROUP)
        h_ref[r0:r0 + ROW_GROUP, :] = _modulated(x, norm, mod, shift_row, rows >= SEQ).astype(BF16)


def _ada_kernel(a_ref, w_ref, b_ref, o_ref):
    a = _silu(a_ref[...]).astype(BF16)
    o_ref[...] = jnp.dot(a, w_ref[...].astype(BF16), preferred_element_type=F32) + b_ref[...]


def _ada(cvec, ada_w, ada_b):
    n = ada_w.shape[1]
    tn = 1024
    return pl.pallas_call(
        _ada_kernel,
        grid=(n // tn,),
        in_specs=[
            pl.BlockSpec((SUBLANES_F32, D_MODEL), lambda j: (0, 0)),
            pl.BlockSpec((D_MODEL, tn), lambda j: (0, j)),
            pl.BlockSpec((1, tn), lambda j: (0, j)),
        ],
        out_specs=pl.BlockSpec((SUBLANES_F32, tn), lambda j: (0, j)),
        out_shape=jax.ShapeDtypeStruct((SUBLANES_F32, n), F32),
        compiler_params=pltpu.CompilerParams(
            dimension_semantics=("arbitrary",), vmem_limit_bytes=VMEM_LIMIT),
        name="ada_mod",
    )(cvec, ada_w, ada_b.reshape(1, n))


def _mod_rows(mods):
    m = mods[:2].reshape(2, N_MOD, D_MODEL)
    m = jnp.transpose(m, (1, 0, 2)).reshape(2 * N_MOD, D_MODEL)
    return jnp.concatenate([m, jnp.zeros((4, D_MODEL), F32)], axis=0)


def _inproj_kernel(x_ref, ctx_ref, norm_ref, mod_ref, wa_ref, wb_ref, we_ref, z_ref, ze_ref, h_ref,
                   *, na, split_tokens):
    i = pl.program_id(0)
    j = pl.program_id(1)

    @pl.when(j == 0)
    def _():
        _modulate_rows(x_ref, h_ref, norm_ref[...], mod_ref[...], 0, i * TM,
                       ctx_ref if split_tokens else None)
        ze_ref[...] = jnp.dot(h_ref[...], we_ref[...], preferred_element_type=F32)

    @pl.when(j < na)
    def _():
        z_ref[...] = jnp.dot(h_ref[...], wa_ref[...], preferred_element_type=F32).astype(z_ref.dtype)

    @pl.when(j >= na)
    def _():
        z_ref[...] = jnp.dot(h_ref[...], wb_ref[...], preferred_element_type=F32).astype(z_ref.dtype)


def _inproj(tokens, ctx, norm, mod, w_a, w_b, w_extra, tn):
    na = w_a.shape[1] // tn
    nb = w_b.shape[1] // tn
    n = (na + nb) * tn
    split_tokens = ctx is not None
    if not split_tokens:
        ctx = tokens
    wb_tile = lambda j: jnp.where(j == 0, nb - 1, jnp.maximum(j - na, 0))
    return pl.pallas_call(
        functools.partial(_inproj_kernel, na=na, split_tokens=split_tokens),
        grid=(T_ALL // TM, na + nb),
        in_specs=[
            pl.BlockSpec((TM, D_MODEL), lambda i, j: (i, 0)),
            pl.BlockSpec((CTX_LEN, D_MODEL), lambda i, j: (0, 0)),
            pl.BlockSpec((1, D_MODEL), lambda i, j: (0, 0)),
            pl.BlockSpec((16, D_MODEL), lambda i, j: (0, 0)),
            pl.BlockSpec((D_MODEL, tn), lambda i, j: (0, jnp.minimum(j, na - 1))),
            pl.BlockSpec((D_MODEL, tn), lambda i, j: (0, wb_tile(j))),
            pl.BlockSpec((D_MODEL, LANES), lambda i, j: (0, 0)),
        ],
        out_specs=[
            pl.BlockSpec((TM, tn), lambda i, j: (i, j)),
            pl.BlockSpec((TM, LANES), lambda i, j: (i, 0)),
        ],
        out_shape=[
            jax.ShapeDtypeStruct((T_ALL, n), BF16),
            jax.ShapeDtypeStruct((T_ALL, LANES), F32),
        ],
        scratch_shapes=[pltpu.VMEM((TM, D_MODEL), BF16)],
        compiler_params=pltpu.CompilerParams(
            dimension_semantics=("parallel", "arbitrary"), vmem_limit_bytes=VMEM_LIMIT),
        name="in_proj",
    )(tokens, ctx, norm.reshape(1, D_MODEL), mod, w_a, w_b, w_extra)


def _mla_prep_kernel(ql_ref, kvl_ref, kr_ref, cs_ref, qn_ref, wqn_ref, wqr_ref, wqrr_ref,
                     kvn_ref, wk_ref, wv_ref, q_ref, k_ref, v_ref):
    cos = cs_ref[:, :QK_ROPE]
    sin = cs_ref[:, QK_ROPE:]

    ql = ql_ref[...].astype(F32)
    qn = (ql * lax.rsqrt(jnp.mean(ql * ql, axis=-1, keepdims=True) + EPS) * qn_ref[...]).astype(BF16)
    scale = QK_DIM ** -0.5 * LOG2_E
    q_nope = jnp.dot(qn, wqn_ref[...], preferred_element_type=F32) * scale
    q_r = jnp.dot(qn, wqr_ref[...], preferred_element_type=F32)
    q_rr = jnp.dot(qn, wqrr_ref[...], preferred_element_type=F32)

    kvl = kvl_ref[...].astype(F32)
    kvn = (kvl * lax.rsqrt(jnp.mean(kvl * kvl, axis=-1, keepdims=True) + EPS) * kvn_ref[...]).astype(BF16)
    k_nope = jnp.dot(kvn, wk_ref[...], preferred_element_type=F32)
    v = jnp.dot(kvn, wv_ref[...], preferred_element_type=F32)
    kr = kr_ref[...]
    k_rope = (kr[:, :QK_ROPE] * cos + kr[:, QK_ROPE:] * sin).astype(BF16)

    for h in range(MLA_HEADS):
        q_ref[h, :, :QK_NOPE] = q_nope[:, h * QK_NOPE:(h + 1) * QK_NOPE].astype(BF16)
        rope = q_r[:, h * QK_ROPE:(h + 1) * QK_ROPE] * cos + q_rr[:, h * QK_ROPE:(h + 1) * QK_ROPE] * sin
        q_ref[h, :, QK_NOPE:] = (rope * scale).astype(BF16)
        k_ref[h, :, :QK_NOPE] = k_nope[:, h * QK_NOPE:(h + 1) * QK_NOPE].astype(BF16)
        k_ref[h, :, QK_NOPE:] = k_rope
        v_ref[h, :, :V_HEAD] = v[:, h * V_HEAD:(h + 1) * V_HEAD].astype(BF16)
        v_ref[h, :, V_HEAD:] = jnp.ones((v.shape[0], V_HEAD), BF16)


def _mla_prep(z, kr, cs, q_norm, wqn, wqr, wqrr, kv_norm, wk, wv):
    tm = TM
    const = lambda shape: pl.BlockSpec(shape, lambda i: (0,) * len(shape))
    return pl.pallas_call(
        _mla_prep_kernel,
        grid=(T_ALL // tm,),
        in_specs=[
            pl.BlockSpec((tm, Q_LORA), lambda i: (i, 3 * CONV_CH // Q_LORA)),
            pl.BlockSpec((tm, KV_LORA), lambda i: (i, (3 * CONV_CH + Q_LORA) // KV_LORA)),
            pl.BlockSpec((tm, LANES), lambda i: (i, 0)),
            pl.BlockSpec((tm, LANES), lambda i: (i, 0)),
            const((1, Q_LORA)),
            const((Q_LORA, MLA_HEADS * QK_NOPE)),
            const((Q_LORA, MLA_HEADS * QK_ROPE)),
            const((Q_LORA, MLA_HEADS * QK_ROPE)),
            const((1, KV_LORA)),
            const((KV_LORA, MLA_HEADS * QK_NOPE)),
            const((KV_LORA, MLA_HEADS * V_HEAD)),
        ],
        out_specs=[
            pl.BlockSpec((MLA_HEADS, tm, QK_DIM), lambda i: (0, i, 0)),
            pl.BlockSpec((MLA_HEADS, tm, QK_DIM), lambda i: (0, i, 0)),
            pl.BlockSpec((MLA_HEADS, tm, 2 * V_HEAD), lambda i: (0, i, 0)),
        ],
        out_shape=[
            jax.ShapeDtypeStruct((MLA_HEADS, T_ALL, QK_DIM), BF16),
            jax.ShapeDtypeStruct((MLA_HEADS, T_ALL, QK_DIM), BF16),
            jax.ShapeDtypeStruct((MLA_HEADS, T_ALL, 2 * V_HEAD), BF16),
        ],
        compiler_params=pltpu.CompilerParams(
            dimension_semantics=("parallel",), vmem_limit_bytes=VMEM_LIMIT),
        name="mla_prep",
    )(z, z, kr, cs, q_norm.reshape(1, Q_LORA), wqn, wqr, wqrr, kv_norm.reshape(1, KV_LORA), wk, wv)


def _attn_kernel(qa_ref, qn_ref, k_ref, v_ref, up0_ref, dn0_ref, up1_ref, dn1_ref, wo0_ref, wo1_ref,
                 win1_ref, cvec_ref, adaw_ref, adab_ref,
                 o_ref, up0_o, dn0_o, up1_o, dn1_o, wo0_o, wo1_o, wkv_o, wqo_o, wlr_o, mods_o,
                 sa_ref, sb_ref):
    j = pl.program_id(1)
    n_pairs = SEQ // (2 * TQ)

    def convert_weight_slabs():
        for src, dst in ((up0_ref, up0_o), (dn0_ref, dn0_o), (up1_ref, up1_o), (dn1_ref, dn1_o),
                         (wo0_ref, wo0_o), (wo1_ref, wo1_o)):
            dst[...] = src[...].astype(BF16)
        _ada_kernel(cvec_ref, adaw_ref, adab_ref, mods_o)
        w1 = win1_ref[...]
        kv_cols = GLA_KEY + GLA_VAL
        wkv_o[...] = w1[:, :kv_cols].astype(BF16)
        wqo_o[...] = w1[:, kv_cols + 2 * GATE_RANK:].astype(BF16)
        wlr_o[...] = jnp.concatenate(
            [w1[:, kv_cols:kv_cols + 2 * GATE_RANK],
             jnp.zeros((w1.shape[0], LANES - 2 * GATE_RANK), F32)], axis=1).astype(BF16)

    def score(q, s_ref):
        s_ref[...] = lax.dot_general(q, k_ref[0], (((1,), (1,)), ((), ())), preferred_element_type=F32)

    def attend(s, v, rows):
        p = jnp.exp2(s - jnp.max(s, axis=-1, keepdims=True)).astype(BF16)
        acc = jnp.dot(p, v, preferred_element_type=F32)
        o_ref[rows, :] = (acc[:, :V_HEAD] / acc[:, V_HEAD:V_HEAD + 1]).astype(o_ref.dtype)

    @pl.when(j == 0)
    def _():
        score(qa_ref[0, :TQ, :], sa_ref)

    @pl.when(j < n_pairs)
    def _():
        score(qa_ref[0, TQ:, :], sb_ref)
        attend(sa_ref[...], v_ref[0], slice(0, TQ))
        score(qn_ref[0], sa_ref)
        attend(sb_ref[...], v_ref[0], slice(TQ, 2 * TQ))
        convert_weight_slabs()

    @pl.when(j == n_pairs)
    def _():
        convert_weight_slabs()
        attend(sa_ref[:, SEQ:], v_ref[0, SEQ:, :], slice(0, TQ))
        o_ref[TQ:, :] = jnp.zeros((TQ, V_HEAD), o_ref.dtype)


def _attention(q, k, v, ffn_up0, ffn_down0, ffn_up1, ffn_down1, w_out0, w_out1, w_in1,
               cvec, ada_w1, ada_b1):
    n_pairs = SEQ // (2 * TQ)
    n_j = n_pairs + 1
    n_mod = ada_w1.shape[1]
    ada_col = lambda h, j: (0, jnp.minimum(h * n_j + j, n_mod // LANES - 1))
    assert MLA_HEADS * n_j >= n_mod // LANES

    def slab(n_rows, rows, cols_in, cols_out=None):
        assert n_rows % rows == 0 and MLA_HEADS * n_j >= n_rows // rows
        index = lambda h, j: (jnp.minimum(h * n_j + j, n_rows // rows - 1), 0)
        return (pl.BlockSpec((rows, cols_in), index),
                pl.BlockSpec((rows, cols_in if cols_out is None else cols_out), index))

    up_in, up_out = slab(D_MODEL, SUBLANES_BF16, 2 * D_FF)
    dn_in, dn_out = slab(D_FF, 4 * SUBLANES_BF16, D_MODEL)
    wo_in, wo_out = slab(D_MODEL, SUBLANES_BF16, D_MODEL)
    n_in1 = w_in1.shape[1]
    kv_cols = GLA_KEY + GLA_VAL
    w1_in, w1_kv = slab(D_MODEL, SUBLANES_BF16, n_in1, kv_cols)
    _, w1_qo = slab(D_MODEL, SUBLANES_BF16, n_in1, n_in1 - kv_cols - 2 * GATE_RANK)
    _, w1_lr = slab(D_MODEL, SUBLANES_BF16, n_in1, LANES)
    bf = lambda shape: jax.ShapeDtypeStruct(shape, BF16)
    return pl.pallas_call(
        _attn_kernel,
        grid=(MLA_HEADS, n_j),
        in_specs=[
            pl.BlockSpec((1, 2 * TQ, QK_DIM), lambda h, j: (h, jnp.minimum(j, n_pairs - 1), 0)),
            pl.BlockSpec((1, TQ, QK_DIM), lambda h, j: (h, jnp.minimum(2 * j + 2, SEQ // TQ), 0)),
            pl.BlockSpec((1, T_ALL, QK_DIM), lambda h, j: (h, 0, 0)),
            pl.BlockSpec((1, T_ALL, 2 * V_HEAD), lambda h, j: (h, 0, 0)),
            up_in, dn_in, up_in, dn_in, wo_in, wo_in, w1_in,
            pl.BlockSpec((SUBLANES_F32, D_MODEL), lambda h, j: (0, 0)),
            pl.BlockSpec((D_MODEL, LANES), ada_col),
            pl.BlockSpec((1, LANES), ada_col),
        ],
        out_specs=[
            pl.BlockSpec((2 * TQ, V_HEAD), lambda h, j: (j, h)),
            up_out, dn_out, up_out, dn_out, wo_out, wo_out, w1_kv, w1_qo, w1_lr,
            pl.BlockSpec((SUBLANES_F32, LANES), ada_col),
        ],
        out_shape=[
            bf((n_j * 2 * TQ, MLA_HEADS * V_HEAD)),
            bf(ffn_up0.shape), bf(ffn_down0.shape), bf(ffn_up1.shape), bf(ffn_down1.shape),
            bf(w_out0.shape), bf(w_out1.shape),
            bf((D_MODEL, kv_cols)), bf((D_MODEL, n_in1 - kv_cols - 2 * GATE_RANK)), bf((D_MODEL, LANES)),
            jax.ShapeDtypeStruct((SUBLANES_F32, n_mod), F32),
        ],
        scratch_shapes=[pltpu.VMEM((TQ, T_ALL), F32), pltpu.VMEM((TQ, T_ALL), F32)],
        compiler_params=pltpu.CompilerParams(
            dimension_semantics=("arbitrary", "arbitrary"), vmem_limit_bytes=VMEM_LIMIT),
        name="mla_attn",
    )(q, q, k, v, ffn_up0, ffn_down0, ffn_up1, ffn_down1, w_out0, w_out1, w_in1,
      cvec, ada_w1, ada_b1.reshape(1, n_mod))


def _neighbour_masks(rows):
    has_prev = (rows != 0) & (rows != SEQ)
    has_next = (rows != SEQ - 1) & (rows != T_ALL - 1)
    return has_prev, has_next


def _ab_out_kernel(x_ref, ctx_ref, ax_ref, ab_ref, ac_ref, axp_ref, acp_ref, axn_ref, acn_ref, o_ref,
                   cw_ref, mod_ref, w_ref, out_ref, h_ref, p_ref):
    i = pl.program_id(0)
    tm = x_ref.shape[0]
    halo = SUBLANES_F32
    hb = SUBLANES_BF16
    f32 = lambda ref, r0, n: ref[r0:r0 + n, :].astype(F32)

    p_ref[halo - 1:halo, :] = f32(acp_ref, hb - 1, 1) * f32(axp_ref, hb - 1, 1)
    p_ref[halo + tm:halo + tm + 1, :] = f32(acn_ref, 0, 1) * f32(axn_ref, 0, 1)
    for r0 in range(0, tm, ROW_GROUP):
        p_ref[halo + r0:halo + r0 + ROW_GROUP, :] = f32(ac_ref, r0, ROW_GROUP) * f32(ax_ref, r0, ROW_GROUP)

    cw = cw_ref[...]
    for r0 in range(0, tm, ROW_GROUP):
        rows = i * tm + r0 + lax.broadcasted_iota(jnp.int32, (ROW_GROUP, 1), 0)
        has_prev, has_next = _neighbour_masks(rows)
        at = lambda shift: p_ref[halo + r0 + shift:halo + r0 + shift + ROW_GROUP, :]
        conv = (jnp.where(has_prev, at(-1), 0.0) * cw[0:1, :] + at(0) * cw[1:2, :]
                + jnp.where(has_next, at(1), 0.0) * cw[2:3, :])
        h_ref[r0:r0 + ROW_GROUP, :CONV_CH] = (f32(ab_ref, r0, ROW_GROUP) * conv).astype(BF16)
    h_ref[:, CONV_CH:] = o_ref[...]

    y = jnp.dot(h_ref[...], w_ref[...], preferred_element_type=F32)
    mod = mod_ref[...]
    gate = jnp.where(_row_ids(i, tm) >= SEQ, mod[5:6, :], mod[4:5, :])
    ctx_start = SEQ % tm
    is_ctx_tile = i == SEQ // tm
    out_ref[:ctx_start, :] = x_ref[:ctx_start, :] + gate[:ctx_start] * y[:ctx_start]
    out_ref[ctx_start:, :] = (_token_rows(x_ref, ctx_ref, is_ctx_tile, ctx_start, tm - ctx_start)
                              + gate[ctx_start:] * y[ctx_start:])


def _ab_out(x_lat, ctx, z, o, conv_a, mod, w_out):
    tm = TMO
    hb = SUBLANES_BF16
    nb = tm // hb
    last = T_ALL // hb - 1
    prev_map = lambda c: (lambda i: (jnp.maximum(i * nb - 1, 0), c))
    next_map = lambda c: (lambda i: (jnp.minimum((i + 1) * nb, last), c))
    return pl.pallas_call(
        _ab_out_kernel,
        grid=(T_ALL // tm,),
        in_specs=[
            pl.BlockSpec((tm, D_MODEL), lambda i: (i, 0)),
            pl.BlockSpec((CTX_LEN, D_MODEL), lambda i: (0, 0)),
            pl.BlockSpec((tm, CONV_CH), lambda i: (i, 0)),
            pl.BlockSpec((tm, CONV_CH), lambda i: (i, 1)),
            pl.BlockSpec((tm, CONV_CH), lambda i: (i, 2)),
            pl.BlockSpec((hb, CONV_CH), prev_map(0)),
            pl.BlockSpec((hb, CONV_CH), prev_map(2)),
            pl.BlockSpec((hb, CONV_CH), next_map(0)),
            pl.BlockSpec((hb, CONV_CH), next_map(2)),
            pl.BlockSpec((tm, MLA_HEADS * V_HEAD), lambda i: (i, 0)),
            pl.BlockSpec((3, CONV_CH), lambda i: (0, 0)),
            pl.BlockSpec((16, D_MODEL), lambda i: (0, 0)),
            pl.BlockSpec((D_MODEL, D_MODEL), lambda i: (0, 0), pipeline_mode=pl.Buffered(1)),
        ],
        out_specs=pl.BlockSpec((tm, D_MODEL), lambda i: (i, 0)),
        out_shape=jax.ShapeDtypeStruct((T_ALL, D_MODEL), F32),
        scratch_shapes=[
            pltpu.VMEM((tm, D_MODEL), BF16),
            pltpu.VMEM((tm + 2 * SUBLANES_F32, CONV_CH), F32),
        ],
        compiler_params=pltpu.CompilerParams(
            dimension_semantics=("parallel",), vmem_limit_bytes=VMEM_LIMIT),
        name="ab_out",
    )(x_lat, ctx, z, z, z, z, z, z, z, o, conv_a, mod, w_out)


def _ffn_kernel(x_ref, xp_ref, xn_ref, norm_ref, mod_ref, wg_ref, wv_ref, cwg_ref, cwv_ref,
                cbg_ref, cbv_ref, wd_ref, fnorm_ref, out_ref, h_ref, acta_ref, actb_ref,
                *, final):
    i = pl.program_id(0)
    j = pl.program_id(1)
    nf = pl.num_programs(1) - 1
    halo = SUBLANES_F32
    rows = _row_ids(i, TM)

    @pl.when(j == 0)
    def _():
        norm = norm_ref[...]
        mod = mod_ref[...]
        _modulate_rows(x_ref, h_ref, norm, mod, 3, i * TM)
        xh = jnp.concatenate([xp_ref[...], xn_ref[...]], axis=0)
        off = lax.broadcasted_iota(jnp.int32, (2 * halo, 1), 0)
        rows_h = jnp.where(off < halo, i * TM - halo + off, (i + 1) * TM - halo + off)
        h_ref[TM:, :] = _modulated(xh, norm, mod, 3, rows_h >= SEQ).astype(BF16)
        out_ref[...] = jnp.zeros_like(out_ref)

    def conv(w_ref, cw_ref, cb_ref):
        has_prev, has_next = _neighbour_masks(rows)
        local = lax.broadcasted_iota(jnp.int32, (TM, 1), 0)
        w = w_ref[...]
        half = TM // 2
        u_lo = jnp.dot(h_ref[:half, :], w, preferred_element_type=F32)
        u_hi = jnp.dot(h_ref[half:, :], w, preferred_element_type=F32)
        u = jnp.concatenate([u_lo, u_hi[:half, :]], axis=0)
        uh = u_hi[half:, :]
        cw = cw_ref[...]
        u_prev = jnp.where(local == 0, uh[halo - 1:halo, :], pltpu.roll(u, 1, axis=0))
        u_next = jnp.where(local == TM - 1, uh[halo:halo + 1, :], pltpu.roll(u, TM - 1, axis=0))
        return (jnp.where(has_prev, u_prev, 0.0) * cw[0:1, :] + u * cw[1:2, :]
                + jnp.where(has_next, u_next, 0.0) * cw[2:3, :] + cb_ref[...])

    def up(act_ref):
        act_ref[...] = (_silu(conv(wg_ref, cwg_ref, cbg_ref))
                        * conv(wv_ref, cwv_ref, cbv_ref)).astype(BF16)

    def down(act_ref):
        out_ref[...] += jnp.dot(act_ref[...], wd_ref[...], preferred_element_type=F32)

    even = lax.rem(j, 2) == 0
    inner = (j > 0) & (j < nf)

    @pl.when(j == 0)
    def _():
        up(acta_ref)

    @pl.when(inner & even)
    def _():
        up(acta_ref)
        down(actb_ref)

    @pl.when(inner & jnp.logical_not(even))
    def _():
        up(actb_ref)
        down(acta_ref)

    @pl.when(j == nf)
    def _():
        down(acta_ref if (D_FF // TF - 1) % 2 == 0 else actb_ref)
        mod = mod_ref[...]
        gate = jnp.where(rows >= SEQ, mod[11:12, :], mod[10:11, :])
        y = x_ref[...] + gate * out_ref[...]
        if final:
            y = y * lax.rsqrt(jnp.mean(y * y, axis=-1, keepdims=True) + EPS) * fnorm_ref[...]
        out_ref[...] = y


def _ffn(xa, norm, mod, w_up, conv_w, conv_b, w_down, final_norm, final):
    nf = D_FF // TF
    hb = SUBLANES_F32
    nb = TM // hb
    last = T_ALL // hb - 1
    out_rows = SEQ if final else T_ALL
    cb = conv_b.reshape(1, 2 * D_FF)
    up_tile = lambda j: jnp.minimum(j, nf - 1)
    return pl.pallas_call(
        functools.partial(_ffn_kernel, final=final),
        grid=(T_ALL // TM, nf + 1),
        in_specs=[
            pl.BlockSpec((TM, D_MODEL), lambda i, j: (i, 0)),
            pl.BlockSpec((hb, D_MODEL), lambda i, j: (jnp.maximum(i * nb - 1, 0), 0)),
            pl.BlockSpec((hb, D_MODEL), lambda i, j: (jnp.minimum((i + 1) * nb, last), 0)),
            pl.BlockSpec((1, D_MODEL), lambda i, j: (0, 0)),
            pl.BlockSpec((16, D_MODEL), lambda i, j: (0, 0)),
            pl.BlockSpec((D_MODEL, TF), lambda i, j: (0, up_tile(j))),
            pl.BlockSpec((D_MODEL, TF), lambda i, j: (0, up_tile(j) + nf)),
            pl.BlockSpec((3, TF), lambda i, j: (0, up_tile(j))),
            pl.BlockSpec((3, TF), lambda i, j: (0, up_tile(j) + nf)),
            pl.BlockSpec((1, TF), lambda i, j: (0, up_tile(j))),
            pl.BlockSpec((1, TF), lambda i, j: (0, up_tile(j) + nf)),
            pl.BlockSpec((TF, D_MODEL), lambda i, j: (jnp.maximum(j - 1, 0), 0)),
            pl.BlockSpec((1, D_MODEL), lambda i, j: (0, 0)),
        ],
        out_specs=pl.BlockSpec((TM, D_MODEL), lambda i, j: (i, 0)),
        out_shape=jax.ShapeDtypeStruct((out_rows, D_MODEL), F32),
        scratch_shapes=[
            pltpu.VMEM((TM + 2 * hb, D_MODEL), BF16),
            pltpu.VMEM((TM, TF), BF16),
            pltpu.VMEM((TM, TF), BF16),
        ],
        compiler_params=pltpu.CompilerParams(
            dimension_semantics=("parallel", "arbitrary"), vmem_limit_bytes=VMEM_LIMIT),
        name="conv_ffn_final" if final else "conv_ffn",
    )(xa, xa, xa, norm.reshape(1, D_MODEL), mod, w_up, w_up, conv_w, conv_w, cb, cb, w_down,
      final_norm.reshape(1, D_MODEL))


def _per_chunk(rows):
    return jnp.concatenate([jnp.broadcast_to(t, (CHUNK, t.shape[-1])) for t in rows], axis=0)


def _gla_direction(q_ref, k_ref, v_refs, lr_ref, gw_ref, gb_ref, o_ref, st_ref, causal):
    n = GLA_BLOCK
    nc = n // CHUNK
    nt = (((1,), (1,)), ((), ()))
    r = lax.broadcasted_iota(jnp.int32, (n, n), 0)
    c = lax.broadcasted_iota(jnp.int32, (n, n), 1)
    shift = CHUNK.bit_length() - 1
    rc = jnp.right_shift(r, shift)
    cc = jnp.right_shift(c, shift)
    tri = (c <= r) if causal else (c >= r)
    delta = (rc - cc) if causal else (cc - rc)

    pre = jnp.dot(lr_ref[...].astype(BF16), gw_ref[...], preferred_element_type=F32) + gb_ref[...]
    g = (jnp.minimum(pre, 0.0) - jnp.log(1.0 + jnp.exp(-jnp.abs(pre)))) / GATE_NORMALIZER
    tri_b = jnp.where(tri, 1.0, 0.0).astype(BF16)
    g_hi = g.astype(BF16)
    g_lo = (g - g_hi.astype(F32)).astype(BF16)
    bb = (jnp.dot(tri_b, g_hi, preferred_element_type=F32)
          + jnp.dot(tri_b, g_lo, preferred_element_type=F32))

    zero = jnp.zeros((1, GLA_KEY), F32)
    if causal:
        edge = [zero] + [bb[ci * CHUNK - 1:ci * CHUNK, :] for ci in range(1, nc)]
        tot = [bb[ci * CHUNK + CHUNK - 1:ci * CHUNK + CHUNK, :] - edge[ci] for ci in range(nc)]
        ball = bb[n - 1:n, :]
        near = [zero] + tot[:nc - 1]
        far = [zero, zero] + tot[:nc - 2]
    else:
        edge = [bb[(ci + 1) * CHUNK:(ci + 1) * CHUNK + 1, :] for ci in range(nc - 1)] + [zero]
        tot = [bb[ci * CHUNK:ci * CHUNK + 1, :] - edge[ci] for ci in range(nc)]
        ball = bb[0:1, :]
        near = tot[1:] + [zero]
        far = tot[2:] + [zero, zero]
    b = bb - _per_chunk(edge)
    skip1 = _per_chunk([jnp.exp(t) for t in near])
    skip2 = _per_chunk([jnp.exp(t1 + t2) for t1, t2 in zip(near, far)])
    to_chunk_end = _per_chunk([jnp.exp(t) for t in tot])
    from_block_edge = _per_chunk([jnp.exp(e) for e in edge])
    to_block_end = _per_chunk([jnp.exp(ball - e - t) for e, t in zip(edge, tot)])
    dec_all = jnp.exp(ball)

    q = q_ref[...].astype(F32) * (GLA_DK ** -0.5)
    k = k_ref[...].astype(F32)
    qe_f = q * jnp.exp(b)
    ke_f = k * jnp.exp(-b)
    kd_f = ke_f * to_chunk_end
    qe = qe_f.astype(BF16)
    ke = ke_f.astype(BF16)
    kd = kd_f.astype(BF16)
    q1 = (qe_f * skip1).astype(BF16)
    q2 = (qe_f * skip2).astype(BF16)
    q_in = (qe_f * from_block_edge).astype(BF16)
    k_out = (kd_f * to_block_end).astype(BF16)

    for hd in range(GLA_HEADS):
        ks = slice(hd * GLA_DK, (hd + 1) * GLA_DK)
        v_half = v_refs[hd // (GLA_HEADS // 2)]
        v_col = (hd % (GLA_HEADS // 2)) * GLA_DV
        v = v_half[:, v_col:v_col + GLA_DV]
        q_far = jnp.concatenate([qe[:, ks], q1[:, ks], q2[:, ks]], axis=0)
        a0 = lax.dot_general(qe[:, ks], ke[:, ks], nt, preferred_element_type=F32)
        a123 = lax.dot_general(q_far, kd[:, ks], nt, preferred_element_type=F32)
        att = jnp.where((delta == 0) & tri, a0, 0.0)
        for d in range(1, nc):
            att = jnp.where(delta == d, a123[(d - 1) * n:d * n, :], att)
        st = st_ref[hd]
        o_ref[:, hd * GLA_DV:(hd + 1) * GLA_DV] = (
            jnp.dot(att.astype(BF16), v, preferred_element_type=F32)
            + lax.dot_general(q_in[:, ks], st.astype(BF16), nt, preferred_element_type=F32))
        upd = lax.dot_general(v, k_out[:, ks], (((0,), (0,)), ((), ())), preferred_element_type=F32)
        st_ref[hd] = st * dec_all[:, ks] + upd


def _gla_kernel(qf_ref, kf_ref, vf0_ref, vf1_ref, lrf_ref, qb_ref, kb_ref, vb0_ref, vb1_ref, lrb_ref,
                gwf_ref, gbf_ref, gwb_ref, gbb_ref, of_ref, ob_ref, sf_ref, sb_ref):
    @pl.when(pl.program_id(0) == 0)
    def _():
        sf_ref[...] = jnp.zeros_like(sf_ref)
        sb_ref[...] = jnp.zeros_like(sb_ref)

    _gla_direction(qf_ref, kf_ref, (vf0_ref, vf1_ref), lrf_ref, gwf_ref, gbf_ref, of_ref, sf_ref, True)
    _gla_direction(qb_ref, kb_ref, (vb0_ref, vb1_ref), lrb_ref, gwb_ref, gbb_ref, ob_ref, sb_ref, False)


def _gla(z, lr, gwf, gbf, gwb, gbb):
    nblk = T_ALL // GLA_BLOCK
    ctx_blk = SEQ // GLA_BLOCK
    fwd = lambda s: jnp.where(s == 0, ctx_blk, s - 1)
    bwd = lambda s: jnp.where(s == 0, ctx_blk, ctx_blk - s)

    def specs(order):
        return [
            pl.BlockSpec((GLA_BLOCK, GLA_KEY), lambda s: (order(s), 3)),
            pl.BlockSpec((GLA_BLOCK, GLA_KEY), lambda s: (order(s), 0)),
            pl.BlockSpec((GLA_BLOCK, GLA_KEY), lambda s: (order(s), 1)),
            pl.BlockSpec((GLA_BLOCK, GLA_KEY), lambda s: (order(s), 2)),
            pl.BlockSpec((GLA_BLOCK, LANES), lambda s: (order(s), 0)),
        ]

    gate_specs = [
        pl.BlockSpec((LANES, GLA_KEY), lambda s: (0, 0)),
        pl.BlockSpec((1, GLA_KEY), lambda s: (0, 0)),
    ]
    return pl.pallas_call(
        _gla_kernel,
        grid=(nblk,),
        in_specs=specs(fwd) + specs(bwd) + gate_specs + gate_specs,
        out_specs=[
            pl.BlockSpec((GLA_BLOCK, GLA_VAL), lambda s: (fwd(s), 0)),
            pl.BlockSpec((GLA_BLOCK, GLA_VAL), lambda s: (bwd(s), 0)),
        ],
        out_shape=[
            jax.ShapeDtypeStruct((T_ALL, GLA_VAL), F32),
            jax.ShapeDtypeStruct((T_ALL, GLA_VAL), F32),
        ],
        scratch_shapes=[
            pltpu.VMEM((GLA_HEADS, GLA_DV, GLA_DK), F32),
            pltpu.VMEM((GLA_HEADS, GLA_DV, GLA_DK), F32),
        ],
        compiler_params=pltpu.CompilerParams(
            dimension_semantics=("arbitrary",), vmem_limit_bytes=VMEM_LIMIT),
        name="gla_scan",
    )(z, z, z, z, lr, z, z, z, z, lr, gwf, gbf.reshape(1, GLA_KEY), gwb, gbb.reshape(1, GLA_KEY))


def _gla_out_kernel(x_ref, of_ref, ob_ref, og_ref, on_ref, mod_ref, w_ref, out_ref, h_ref):
    i = pl.program_id(0)
    tm = x_ref.shape[0]
    on = on_ref[...]
    for r0 in range(0, tm, ROW_GROUP):
        rs = slice(r0, r0 + ROW_GROUP)
        for hd in range(GLA_HEADS):
            sl = slice(hd * GLA_DV, (hd + 1) * GLA_DV)
            oh = of_ref[rs, sl] + ob_ref[rs, sl]
            y = oh * lax.rsqrt(jnp.mean(oh * oh, axis=-1, keepdims=True) + EPS) * on
            h_ref[rs, sl] = (y * _silu(og_ref[rs, sl].astype(F32))).astype(BF16)

    y = jnp.dot(h_ref[...], w_ref[...], preferred_element_type=F32)
    mod = mod_ref[...]
    gate = jnp.where(_row_ids(i, tm) >= SEQ, mod[5:6, :], mod[4:5, :])
    out_ref[...] = x_ref[...] + gate * y


def _gla_out(xa, o_f, o_b, z, o_norm, mod, w_out):
    tm = TMO
    og_blk = 2 * GLA_KEY // GLA_VAL + 1
    return pl.pallas_call(
        _gla_out_kernel,
        grid=(T_ALL // tm,),
        in_specs=[
            pl.BlockSpec((tm, D_MODEL), lambda i: (i, 0)),
            pl.BlockSpec((tm, GLA_VAL), lambda i: (i, 0)),
            pl.BlockSpec((tm, GLA_VAL), lambda i: (i, 0)),
            pl.BlockSpec((tm, GLA_VAL), lambda i: (i, og_blk)),
            pl.BlockSpec((1, GLA_DV), lambda i: (0, 0)),
            pl.BlockSpec((16, D_MODEL), lambda i: (0, 0)),
            pl.BlockSpec((GLA_VAL, D_MODEL), lambda i: (0, 0), pipeline_mode=pl.Buffered(1)),
        ],
        out_specs=pl.BlockSpec((tm, D_MODEL), lambda i: (i, 0)),
        out_shape=jax.ShapeDtypeStruct((T_ALL, D_MODEL), F32),
        scratch_shapes=[pltpu.VMEM((tm, GLA_VAL), BF16)],
        compiler_params=pltpu.CompilerParams(
            dimension_semantics=("parallel",), vmem_limit_bytes=VMEM_LIMIT),
        name="gla_out",
    )(xa, o_f, o_b, z, o_norm.reshape(1, GLA_DV), mod, w_out)


def _rot_cols(w):
    a1, a2, b1, b2 = jnp.split(w, 4, axis=-1)
    return jnp.concatenate([-a2, a1, -b2, b1], axis=-1)


def _rope_table():
    t = np.arange(SEQ)
    row = (t // GRID_W).astype(np.float32)
    col = (t % GRID_W).astype(np.float32)
    inv = (np.float32(ROPE_THETA) ** (-np.arange(0, AXIS_ROPE, 2, dtype=np.float32) / AXIS_ROPE)).astype(np.float32)
    ar = row[:, None] * inv
    ac = col[:, None] * inv
    ang = np.concatenate([ar, ar, ac, ac], axis=-1).astype(np.float64)
    lat = np.concatenate([np.cos(ang), np.sin(ang)], axis=-1)
    ctx = np.concatenate([np.ones((CTX_LEN, QK_ROPE)), np.zeros((CTX_LEN, QK_ROPE))], axis=-1)
    return jnp.asarray(np.concatenate([lat, ctx], axis=0), dtype=F32)


def kernel(x, c, ctx, c_ctx, l0_ada_w, l0_ada_b, l0_norm1, l0_w_in, l0_conv_a, l0_q_norm, l0_w_qb, l0_kv_norm, l0_w_kvb, l0_w_out, l0_norm2, l0_ffn_up, l0_ffn_conv_w, l0_ffn_conv_b, l0_ffn_down, l1_ada_w, l1_ada_b, l1_norm1, l1_w_in, l1_gate_fw_w, l1_gate_fw_b, l1_gate_bw_w, l1_gate_bw_b, l1_o_norm, l1_w_out, l1_norm2, l1_ffn_up, l1_ffn_conv_w, l1_ffn_conv_b, l1_ffn_down, final_norm):
    assert x.shape == (1, SEQ, D_MODEL) and ctx.shape == (1, CTX_LEN, D_MODEL)
    x_lat, x_ctx = x[0], ctx[0]
    cvec = jnp.concatenate([c, c_ctx[None, :], jnp.zeros((SUBLANES_F32 - 2, D_MODEL), F32)], axis=0)
    mod0 = _mod_rows(_ada(cvec, l0_ada_w, l0_ada_b))

    c_rope, c_q, c_conv = KV_LORA, KV_LORA + QK_ROPE, KV_LORA + QK_ROPE + Q_LORA
    w_conv0 = l0_w_in[:, c_conv:].astype(BF16)
    w_lat0 = jnp.concatenate([l0_w_in[:, c_q:c_conv], l0_w_in[:, :c_rope]], axis=1).astype(BF16)
    k_rope = l0_w_in[:, c_rope:c_q]
    w_kr = jnp.concatenate([k_rope, _rot_cols(k_rope)], axis=1).astype(BF16)
    z0, kr = _inproj(x_lat, x_ctx, l0_norm1, mod0, w_conv0, w_lat0, w_kr, tn=Q_LORA + KV_LORA)

    wq = l0_w_qb.reshape(Q_LORA, MLA_HEADS, QK_DIM)
    wqn = wq[:, :, :QK_NOPE].reshape(Q_LORA, MLA_HEADS * QK_NOPE).astype(BF16)
    wqr = wq[:, :, QK_NOPE:]
    wqrr = _rot_cols(wqr).reshape(Q_LORA, MLA_HEADS * QK_ROPE).astype(BF16)
    wqr = wqr.reshape(Q_LORA, MLA_HEADS * QK_ROPE).astype(BF16)
    wkv = l0_w_kvb.reshape(KV_LORA, MLA_HEADS, QK_NOPE + V_HEAD)
    wk = wkv[:, :, :QK_NOPE].reshape(KV_LORA, MLA_HEADS * QK_NOPE).astype(BF16)
    wv = wkv[:, :, QK_NOPE:].reshape(KV_LORA, MLA_HEADS * V_HEAD).astype(BF16)
    q, k, v = _mla_prep(z0, kr, _rope_table(), l0_q_norm, wqn, wqr, wqrr, l0_kv_norm, wk, wv)
    o, up0, down0, up1, down1, w_out0, w_out1, w_kv1, w_qo1, w_lr, mods1 = _attention(
        q, k, v, l0_ffn_up, l0_ffn_down, l1_ffn_up, l1_ffn_down, l0_w_out, l1_w_out, l1_w_in,
        cvec, l1_ada_w, l1_ada_b)
    mod1 = _mod_rows(mods1)
    xa = _ab_out(x_lat, x_ctx, z0, o, l0_conv_a, mod0, w_out0)
    xa = _ffn(xa, l0_norm2, mod0, up0, l0_ffn_conv_w, l0_ffn_conv_b, down0, final_norm, final=False)

    z1, lr = _inproj(xa, None, l1_norm1, mod1, w_kv1, w_qo1, w_lr, tn=1536)
    pad_f = jnp.zeros((LANES - GATE_RANK, GLA_KEY), F32)
    gwf = jnp.concatenate([l1_gate_fw_w, pad_f], axis=0).astype(BF16)
    gwb = jnp.concatenate([jnp.zeros((GATE_RANK, GLA_KEY), F32), l1_gate_bw_w,
                           jnp.zeros((LANES - 2 * GATE_RANK, GLA_KEY), F32)], axis=0).astype(BF16)
    o_f, o_b = _gla(z1, lr, gwf, l1_gate_fw_b, gwb, l1_gate_bw_b)
    xa = _gla_out(xa, o_f, o_b, z1, l1_o_norm, mod1, w_out1)
    out = _ffn(xa, l1_norm2, mod1, up1, l1_ffn_conv_w, l1_ffn_conv_b, down1, final_norm, final=True)
    return out[None]
```

```python
import functools

import jax
import jax.numpy as jnp
import numpy as np
from jax import lax
from jax.experimental import pallas as pl
from jax.experimental.pallas import tpu as pltpu

F32 = jnp.float32
BF16 = jnp.bfloat16

D_MODEL = 2048
SEQ = 8192
GRID_W = 64
CTX_LEN = 256
T_ALL = SEQ + CTX_LEN
EPS = 1e-6
LOG2_E = 1.4426950408889634
N_MOD = 6
MLA_HEADS = 8
QK_NOPE = 128
QK_ROPE = 64
QK_DIM = QK_NOPE + QK_ROPE
V_HEAD = 128
Q_LORA = 512
KV_LORA = 256
ROPE_THETA = 10000.0
AXIS_ROPE = QK_ROPE // 2
CONV_CH = D_MODEL // 2
GLA_HEADS = 4
GLA_DK = D_MODEL // 2 // GLA_HEADS
GLA_DV = D_MODEL // GLA_HEADS
GLA_KEY = GLA_HEADS * GLA_DK
GLA_VAL = GLA_HEADS * GLA_DV
GATE_RANK = 16
GATE_NORMALIZER = 16.0
CHUNK = 64
D_FF = 5632

LANES = 128
SUBLANES_F32 = 8
SUBLANES_BF16 = 16
VMEM_LIMIT = 56 * 1024 * 1024

TM = 768
TMO = 384
ROW_GROUP = 16
TQ = 256
GLA_BLOCK = 256
TF = 512


def _silu(v):
    return v / (1.0 + jnp.exp(-v))


def _row_ids(i, rows):
    return i * rows + lax.broadcasted_iota(jnp.int32, (rows, 1), 0)


def _modulated(x, norm, mod, shift_row, is_ctx):
    s = 2 * shift_row
    gain_lat = norm * (1.0 + mod[s + 2:s + 3, :])
    gain_ctx = norm * (1.0 + mod[s + 3:s + 4, :])
    y = x * lax.rsqrt(jnp.mean(x * x, axis=-1, keepdims=True) + EPS)
    return (y * jnp.where(is_ctx, gain_ctx, gain_lat)
            + jnp.where(is_ctx, mod[s + 1:s + 2, :], mod[s:s + 1, :]))


def _token_rows(x_ref, ctx_ref, is_ctx_tile, r0, n, cols=slice(None)):
    rows = x_ref[r0:r0 + n, cols]
    tile_rows = x_ref.shape[0]
    ctx_start = SEQ % tile_rows
    if ctx_ref is None or r0 + n <= ctx_start:
        return rows
    assert r0 >= ctx_start and ctx_start + CTX_LEN == tile_rows
    return jnp.where(is_ctx_tile, ctx_ref[r0 - ctx_start:r0 - ctx_start + n, cols], rows)


def _modulate_rows(x_ref, h_ref, norm, mod, shift_row, first_row, ctx_ref=None):
    is_ctx_tile = first_row == (SEQ // x_ref.shape[0]) * x_ref.shape[0]
    for r0 in range(0, x_ref.shape[0], ROW_GROUP):
        rows = first_row + r0 + lax.broadcasted_iota(jnp.int32, (ROW_GROUP, 1), 0)
        x = _token_rows(x_ref, ctx_ref, is_ctx_tile, r0, ROW_GROUP)
        h_ref[r0:r0 + ROW_GROUP, :] = _modulated(x, norm, mod, shift_row, rows >= SEQ).astype(BF16)


def _ada_kernel(a_ref, w_ref, b_ref, o_ref):
    a = _silu(a_ref[...]).astype(BF16)
    o_ref[...] = jnp.dot(a, w_ref[...].astype(BF16), preferred_element_type=F32) + b_ref[...]


def _ada(cvec, ada_w, ada_b):
    n = ada_w.shape[1]
    tn = 1024
    return pl.pallas_call(
        _ada_kernel,
        grid=(n // tn,),
        in_specs=[
            pl.BlockSpec((SUBLANES_F32, D_MODEL), lambda j: (0, 0)),
            pl.BlockSpec((D_MODEL, tn), lambda j: (0, j)),
            pl.BlockSpec((1, tn), lambda j: (0, j)),
        ],
        out_specs=pl.BlockSpec((SUBLANES_F32, tn), lambda j: (0, j)),
        out_shape=jax.ShapeDtypeStruct((SUBLANES_F32, n), F32),
        compiler_params=pltpu.CompilerParams(
            dimension_semantics=("arbitrary",), vmem_limit_bytes=VMEM_LIMIT),
        name="ada_mod",
    )(cvec, ada_w, ada_b.reshape(1, n))


def _mod_rows(mods):
    m = mods[:2].reshape(2, N_MOD, D_MODEL)
    m = jnp.transpose(m, (1, 0, 2)).reshape(2 * N_MOD, D_MODEL)
    return jnp.concatenate([m, jnp.zeros((4, D_MODEL), F32)], axis=0)


def _inproj_kernel(x_ref, ctx_ref, norm_ref, mod_ref, wa_ref, wb_ref, we_ref, z_ref, ze_ref, h_ref,
                   *, na, split_tokens):
    i = pl.program_id(0)
    j = pl.program_id(1)

    @pl.when(j == 0)
    def _():
        _modulate_rows(x_ref, h_ref, norm_ref[...], mod_ref[...], 0, i * TM,
                       ctx_ref if split_tokens else None)
        ze_ref[...] = jnp.dot(h_ref[...], we_ref[...], preferred_element_type=F32)
        part = TM // 3
        for r0 in range(0, TM, part):
            z_ref[r0:r0 + part, :] = jnp.dot(h_ref[r0:r0 + part, :], wa_ref[...],
                                             preferred_element_type=F32).astype(z_ref.dtype)

    @pl.when((j > 0) & (j < na))
    def _():
        z_ref[...] = jnp.dot(h_ref[...], wa_ref[...], preferred_element_type=F32).astype(z_ref.dtype)

    @pl.when(j >= na)
    def _():
        z_ref[...] = jnp.dot(h_ref[...], wb_ref[...], preferred_element_type=F32).astype(z_ref.dtype)


def _inproj(tokens, ctx, norm, mod, w_a, w_b, w_extra, tn):
    na = w_a.shape[1] // tn
    nb = w_b.shape[1] // tn
    n = (na + nb) * tn
    split_tokens = ctx is not None
    if not split_tokens:
        ctx = tokens
    wb_tile = lambda j: jnp.where(j == 0, nb - 1, jnp.maximum(j - na, 0))
    return pl.pallas_call(
        functools.partial(_inproj_kernel, na=na, split_tokens=split_tokens),
        grid=(T_ALL // TM, na + nb),
        in_specs=[
            pl.BlockSpec((TM, D_MODEL), lambda i, j: (i, 0)),
            pl.BlockSpec((CTX_LEN, D_MODEL), lambda i, j: (0, 0)),
            pl.BlockSpec((1, D_MODEL), lambda i, j: (0, 0)),
            pl.BlockSpec((16, D_MODEL), lambda i, j: (0, 0)),
            pl.BlockSpec((D_MODEL, tn), lambda i, j: (0, jnp.minimum(j, na - 1))),
            pl.BlockSpec((D_MODEL, tn), lambda i, j: (0, wb_tile(j))),
            pl.BlockSpec((D_MODEL, LANES), lambda i, j: (0, 0)),
        ],
        out_specs=[
            pl.BlockSpec((TM, tn), lambda i, j: (i, j)),
            pl.BlockSpec((TM, LANES), lambda i, j: (i, 0)),
        ],
        out_shape=[
            jax.ShapeDtypeStruct((T_ALL, n), BF16),
            jax.ShapeDtypeStruct((T_ALL, LANES), F32),
        ],
        scratch_shapes=[pltpu.VMEM((TM, D_MODEL), BF16)],
        compiler_params=pltpu.CompilerParams(
            dimension_semantics=("parallel", "arbitrary"), vmem_limit_bytes=VMEM_LIMIT),
        name="in_proj",
    )(tokens, ctx, norm.reshape(1, D_MODEL), mod, w_a, w_b, w_extra)


def _mla_prep_kernel(ql_ref, kvl_ref, kr_ref, cs_ref, qn_ref, wqn_ref, wqr_ref, wqrr_ref,
                     kvn_ref, wk_ref, wv_ref, q_ref, k_ref, v_ref):
    cos = cs_ref[:, :QK_ROPE]
    sin = cs_ref[:, QK_ROPE:]

    ql = ql_ref[...].astype(F32)
    qn = (ql * lax.rsqrt(jnp.mean(ql * ql, axis=-1, keepdims=True) + EPS) * qn_ref[...]).astype(BF16)
    scale = QK_DIM ** -0.5 * LOG2_E
    q_nope = jnp.dot(qn, wqn_ref[...], preferred_element_type=F32) * scale
    q_r = jnp.dot(qn, wqr_ref[...], preferred_element_type=F32)
    q_rr = jnp.dot(qn, wqrr_ref[...], preferred_element_type=F32)

    kvl = kvl_ref[...].astype(F32)
    kvn = (kvl * lax.rsqrt(jnp.mean(kvl * kvl, axis=-1, keepdims=True) + EPS) * kvn_ref[...]).astype(BF16)
    k_nope = jnp.dot(kvn, wk_ref[...], preferred_element_type=F32)
    v = jnp.dot(kvn, wv_ref[...], preferred_element_type=F32)
    kr = kr_ref[...]
    k_rope = (kr[:, :QK_ROPE] * cos + kr[:, QK_ROPE:] * sin).astype(BF16)

    for h in range(MLA_HEADS):
        q_ref[h, :, :QK_NOPE] = q_nope[:, h * QK_NOPE:(h + 1) * QK_NOPE].astype(BF16)
        rope = q_r[:, h * QK_ROPE:(h + 1) * QK_ROPE] * cos + q_rr[:, h * QK_ROPE:(h + 1) * QK_ROPE] * sin
        q_ref[h, :, QK_NOPE:] = (rope * scale).astype(BF16)
        k_ref[h, :, :QK_NOPE] = k_nope[:, h * QK_NOPE:(h + 1) * QK_NOPE].astype(BF16)
        k_ref[h, :, QK_NOPE:] = k_rope
        v_ref[h, :, :V_HEAD] = v[:, h * V_HEAD:(h + 1) * V_HEAD].astype(BF16)
        v_ref[h, :, V_HEAD:] = jnp.ones((v.shape[0], V_HEAD), BF16)


def _mla_prep(z, kr, cs, q_norm, wqn, wqr, wqrr, kv_norm, wk, wv):
    tm = TM
    const = lambda shape: pl.BlockSpec(shape, lambda i: (0,) * len(shape))
    return pl.pallas_call(
        _mla_prep_kernel,
        grid=(T_ALL // tm,),
        in_specs=[
            pl.BlockSpec((tm, Q_LORA), lambda i: (i, 3 * CONV_CH // Q_LORA)),
            pl.BlockSpec((tm, KV_LORA), lambda i: (i, (3 * CONV_CH + Q_LORA) // KV_LORA)),
            pl.BlockSpec((tm, LANES), lambda i: (i, 0)),
            pl.BlockSpec((tm, LANES), lambda i: (i, 0)),
            const((1, Q_LORA)),
            const((Q_LORA, MLA_HEADS * QK_NOPE)),
            const((Q_LORA, MLA_HEADS * QK_ROPE)),
            const((Q_LORA, MLA_HEADS * QK_ROPE)),
            const((1, KV_LORA)),
            const((KV_LORA, MLA_HEADS * QK_NOPE)),
            const((KV_LORA, MLA_HEADS * V_HEAD)),
        ],
        out_specs=[
            pl.BlockSpec((MLA_HEADS, tm, QK_DIM), lambda i: (0, i, 0)),
            pl.BlockSpec((MLA_HEADS, tm, QK_DIM), lambda i: (0, i, 0)),
            pl.BlockSpec((MLA_HEADS, tm, 2 * V_HEAD), lambda i: (0, i, 0)),
        ],
        out_shape=[
            jax.ShapeDtypeStruct((MLA_HEADS, T_ALL, QK_DIM), BF16),
            jax.ShapeDtypeStruct((MLA_HEADS, T_ALL, QK_DIM), BF16),
            jax.ShapeDtypeStruct((MLA_HEADS, T_ALL, 2 * V_HEAD), BF16),
        ],
        compiler_params=pltpu.CompilerParams(
            dimension_semantics=("parallel",), vmem_limit_bytes=VMEM_LIMIT),
        name="mla_prep",
    )(z, z, kr, cs, q_norm.reshape(1, Q_LORA), wqn, wqr, wqrr, kv_norm.reshape(1, KV_LORA), wk, wv)


def _attn_kernel(qa_ref, qn_ref, k_ref, v_ref, up0_ref, dn0_ref, up1_ref, dn1_ref, wo0_ref, wo1_ref,
                 win1_ref, cvec_ref, adaw_ref, adab_ref,
                 o_ref, up0_o, dn0_o, up1_o, dn1_o, wo0_o, wo1_o, wkv_o, wqo_o, wlr_o, mods_o,
                 sa_ref, sb_ref):
    j = pl.program_id(1)
    n_pairs = SEQ // (2 * TQ)

    def convert_weight_slabs():
        for src, dst in ((up0_ref, up0_o), (dn0_ref, dn0_o), (up1_ref, up1_o), (dn1_ref, dn1_o),
                         (wo0_ref, wo0_o), (wo1_ref, wo1_o)):
            dst[...] = src[...].astype(BF16)
        _ada_kernel(cvec_ref, adaw_ref, adab_ref, mods_o)
        w1 = win1_ref[...]
        kv_cols = GLA_KEY + GLA_VAL
        wkv_o[...] = w1[:, :kv_cols].astype(BF16)
        wqo_o[...] = w1[:, kv_cols + 2 * GATE_RANK:].astype(BF16)
        wlr_o[...] = jnp.concatenate(
            [w1[:, kv_cols:kv_cols + 2 * GATE_RANK],
             jnp.zeros((w1.shape[0], LANES - 2 * GATE_RANK), F32)], axis=1).astype(BF16)

    def score(q, s_ref):
        s_ref[...] = lax.dot_general(q, k_ref[0], (((1,), (1,)), ((), ())), preferred_element_type=F32)

    def attend(s, v, rows):
        p = jnp.exp2(s - jnp.max(s, axis=-1, keepdims=True)).astype(BF16)
        acc = jnp.dot(p, v, preferred_element_type=F32)
        o_ref[rows, :] = (acc[:, :V_HEAD] / acc[:, V_HEAD:V_HEAD + 1]).astype(o_ref.dtype)

    @pl.when(j == 0)
    def _():
        score(qa_ref[0, :TQ, :], sa_ref)

    @pl.when(j < n_pairs)
    def _():
        score(qa_ref[0, TQ:, :], sb_ref)
        attend(sa_ref[...], v_ref[0], slice(0, TQ))
        score(qn_ref[0], sa_ref)
        attend(sb_ref[...], v_ref[0], slice(TQ, 2 * TQ))
        convert_weight_slabs()

    @pl.when(j == n_pairs)
    def _():
        convert_weight_slabs()
        attend(sa_ref[:, SEQ:], v_ref[0, SEQ:, :], slice(0, TQ))
        o_ref[TQ:, :] = jnp.zeros((TQ, V_HEAD), o_ref.dtype)


def _attention(q, k, v, ffn_up0, ffn_down0, ffn_up1, ffn_down1, w_out0, w_out1, w_in1,
               cvec, ada_w1, ada_b1):
    n_pairs = SEQ // (2 * TQ)
    n_j = n_pairs + 1
    n_mod = ada_w1.shape[1]
    ada_col = lambda h, j: (0, jnp.minimum(h * n_j + j, n_mod // LANES - 1))
    assert MLA_HEADS * n_j >= n_mod // LANES

    def slab(n_rows, rows, cols_in, cols_out=None):
        assert n_rows % rows == 0 and MLA_HEADS * n_j >= n_rows // rows
        index = lambda h, j: (jnp.minimum(h * n_j + j, n_rows // rows - 1), 0)
        return (pl.BlockSpec((rows, cols_in), index),
                pl.BlockSpec((rows, cols_in if cols_out is None else cols_out), index))

    up_in, up_out = slab(D_MODEL, SUBLANES_BF16, 2 * D_FF)
    dn_in, dn_out = slab(D_FF, 4 * SUBLANES_BF16, D_MODEL)
    wo_in, wo_out = slab(D_MODEL, SUBLANES_BF16, D_MODEL)
    n_in1 = w_in1.shape[1]
    kv_cols = GLA_KEY + GLA_VAL
    w1_in, w1_kv = slab(D_MODEL, SUBLANES_BF16, n_in1, kv_cols)
    _, w1_qo = slab(D_MODEL, SUBLANES_BF16, n_in1, n_in1 - kv_cols - 2 * GATE_RANK)
    _, w1_lr = slab(D_MODEL, SUBLANES_BF16, n_in1, LANES)
    bf = lambda shape: jax.ShapeDtypeStruct(shape, BF16)
    return pl.pallas_call(
        _attn_kernel,
        grid=(MLA_HEADS, n_j),
        in_specs=[
            pl.BlockSpec((1, 2 * TQ, QK_DIM), lambda h, j: (h, jnp.minimum(j, n_pairs - 1), 0)),
            pl.BlockSpec((1, TQ, QK_DIM), lambda h, j: (h, jnp.minimum(2 * j + 2, SEQ // TQ), 0)),
            pl.BlockSpec((1, T_ALL, QK_DIM), lambda h, j: (h, 0, 0)),
            pl.BlockSpec((1, T_ALL, 2 * V_HEAD), lambda h, j: (h, 0, 0)),
            up_in, dn_in, up_in, dn_in, wo_in, wo_in, w1_in,
            pl.BlockSpec((SUBLANES_F32, D_MODEL), lambda h, j: (0, 0)),
            pl.BlockSpec((D_MODEL, LANES), ada_col),
            pl.BlockSpec((1, LANES), ada_col),
        ],
        out_specs=[
            pl.BlockSpec((2 * TQ, V_HEAD), lambda h, j: (j, h)),
            up_out, dn_out, up_out, dn_out, wo_out, wo_out, w1_kv, w1_qo, w1_lr,
            pl.BlockSpec((SUBLANES_F32, LANES), ada_col),
        ],
        out_shape=[
            bf((n_j * 2 * TQ, MLA_HEADS * V_HEAD)),
            bf(ffn_up0.shape), bf(ffn_down0.shape), bf(ffn_up1.shape), bf(ffn_down1.shape),
            bf(w_out0.shape), bf(w_out1.shape),
            bf((D_MODEL, kv_cols)), bf((D_MODEL, n_in1 - kv_cols - 2 * GATE_RANK)), bf((D_MODEL, LANES)),
            jax.ShapeDtypeStruct((SUBLANES_F32, n_mod), F32),
        ],
        scratch_shapes=[pltpu.VMEM((TQ, T_ALL), F32), pltpu.VMEM((TQ, T_ALL), F32)],
        compiler_params=pltpu.CompilerParams(
            dimension_semantics=("arbitrary", "arbitrary"), vmem_limit_bytes=VMEM_LIMIT),
        name="mla_attn",
    )(q, q, k, v, ffn_up0, ffn_down0, ffn_up1, ffn_down1, w_out0, w_out1, w_in1,
      cvec, ada_w1, ada_b1.reshape(1, n_mod))


def _neighbour_masks(rows):
    has_prev = (rows != 0) & (rows != SEQ)
    has_next = (rows != SEQ - 1) & (rows != T_ALL - 1)
    return has_prev, has_next


def _ab_out_kernel(x_ref, ctx_ref, ax_ref, ab_ref, ac_ref, axp_ref, acp_ref, axn_ref, acn_ref, o_ref,
                   cw_ref, mod_ref, w_ref, out_ref, h_ref, p_ref):
    i = pl.program_id(0)
    tm = x_ref.shape[0]
    halo = SUBLANES_F32
    hb = SUBLANES_BF16
    f32 = lambda ref, r0, n: ref[r0:r0 + n, :].astype(F32)

    p_ref[halo - 1:halo, :] = f32(acp_ref, hb - 1, 1) * f32(axp_ref, hb - 1, 1)
    p_ref[halo + tm:halo + tm + 1, :] = f32(acn_ref, 0, 1) * f32(axn_ref, 0, 1)
    for r0 in range(0, tm, ROW_GROUP):
        p_ref[halo + r0:halo + r0 + ROW_GROUP, :] = f32(ac_ref, r0, ROW_GROUP) * f32(ax_ref, r0, ROW_GROUP)

    cw = cw_ref[...]
    mod = mod_ref[...]
    gate = jnp.where(_row_ids(i, tm) >= SEQ, mod[5:6, :], mod[4:5, :])
    ctx_start = SEQ % tm
    is_ctx_tile = i == SEQ // tm

    def residual(cols):
        return jnp.concatenate(
            [_token_rows(x_ref, ctx_ref, is_ctx_tile, 0, ctx_start, cols),
             _token_rows(x_ref, ctx_ref, is_ctx_tile, ctx_start, tm - ctx_start, cols)], axis=0)

    for r0 in range(0, tm, ROW_GROUP):
        rows = i * tm + r0 + lax.broadcasted_iota(jnp.int32, (ROW_GROUP, 1), 0)
        has_prev, has_next = _neighbour_masks(rows)
        at = lambda shift: p_ref[halo + r0 + shift:halo + r0 + shift + ROW_GROUP, :]
        conv = (jnp.where(has_prev, at(-1), 0.0) * cw[0:1, :] + at(0) * cw[1:2, :]
                + jnp.where(has_next, at(1), 0.0) * cw[2:3, :])
        h_ref[r0:r0 + ROW_GROUP, :CONV_CH] = (f32(ab_ref, r0, ROW_GROUP) * conv).astype(BF16)
    h_ref[:, CONV_CH:] = o_ref[...]
    y = jnp.dot(h_ref[...], w_ref[...], preferred_element_type=F32)
    out_ref[...] = residual(slice(None)) + gate * y


def _ab_out(x_lat, ctx, z, o, conv_a, mod, w_out):
    tm = TMO
    hb = SUBLANES_BF16
    nb = tm // hb
    last = T_ALL // hb - 1
    prev_map = lambda c: (lambda i: (jnp.maximum(i * nb - 1, 0), c))
    next_map = lambda c: (lambda i: (jnp.minimum((i + 1) * nb, last), c))
    return pl.pallas_call(
        _ab_out_kernel,
        grid=(T_ALL // tm,),
        in_specs=[
            pl.BlockSpec((tm, D_MODEL), lambda i: (i, 0)),
            pl.BlockSpec((CTX_LEN, D_MODEL), lambda i: (0, 0)),
            pl.BlockSpec((tm, CONV_CH), lambda i: (i, 0)),
            pl.BlockSpec((tm, CONV_CH), lambda i: (i, 1)),
            pl.BlockSpec((tm, CONV_CH), lambda i: (i, 2)),
            pl.BlockSpec((hb, CONV_CH), prev_map(0)),
            pl.BlockSpec((hb, CONV_CH), prev_map(2)),
            pl.BlockSpec((hb, CONV_CH), next_map(0)),
            pl.BlockSpec((hb, CONV_CH), next_map(2)),
            pl.BlockSpec((tm, MLA_HEADS * V_HEAD), lambda i: (i, 0)),
            pl.BlockSpec((3, CONV_CH), lambda i: (0, 0)),
            pl.BlockSpec((16, D_MODEL), lambda i: (0, 0)),
            pl.BlockSpec((D_MODEL, D_MODEL), lambda i: (0, 0), pipeline_mode=pl.Buffered(1)),
        ],
        out_specs=pl.BlockSpec((tm, D_MODEL), lambda i: (i, 0)),
        out_shape=jax.ShapeDtypeStruct((T_ALL, D_MODEL), F32),
        scratch_shapes=[
            pltpu.VMEM((tm, D_MODEL), BF16),
            pltpu.VMEM((tm + 2 * SUBLANES_F32, CONV_CH), F32),
        ],
        compiler_params=pltpu.CompilerParams(
            dimension_semantics=("parallel",), vmem_limit_bytes=VMEM_LIMIT),
        name="ab_out",
    )(x_lat, ctx, z, z, z, z, z, z, z, o, conv_a, mod, w_out)


def _ffn_kernel(x_ref, xp_ref, xn_ref, norm_ref, mod_ref, wg_ref, wv_ref, cwg_ref, cwv_ref,
                cbg_ref, cbv_ref, wd_ref, fnorm_ref, out_ref, h_ref, acta_ref, actb_ref,
                *, final):
    i = pl.program_id(0)
    j = pl.program_id(1)
    nf = pl.num_programs(1) - 1
    halo = SUBLANES_F32
    rows = _row_ids(i, TM)

    def prologue():
        norm = norm_ref[...]
        mod = mod_ref[...]
        _modulate_rows(x_ref, h_ref, norm, mod, 3, i * TM)
        xh = jnp.concatenate([xp_ref[...], xn_ref[...]], axis=0)
        off = lax.broadcasted_iota(jnp.int32, (2 * halo, 1), 0)
        rows_h = jnp.where(off < halo, i * TM - halo + off, (i + 1) * TM - halo + off)
        h_ref[TM:, :] = _modulated(xh, norm, mod, 3, rows_h >= SEQ).astype(BF16)
        out_ref[...] = jnp.zeros_like(out_ref)

    def conv(w_ref, cw_ref, cb_ref):
        has_prev, has_next = _neighbour_masks(rows)
        local = lax.broadcasted_iota(jnp.int32, (TM, 1), 0)
        w = w_ref[...]
        part = TM // 3
        u_a = jnp.dot(h_ref[:part, :], w, preferred_element_type=F32)
        u_b = jnp.dot(h_ref[part:2 * part, :], w, preferred_element_type=F32)
        u_c = jnp.dot(h_ref[2 * part:, :], w, preferred_element_type=F32)
        u = jnp.concatenate([u_a, u_b, u_c[:part, :]], axis=0)
        uh = u_c[part:, :]
        cw = cw_ref[...]
        u_prev = jnp.where(local == 0, uh[halo - 1:halo, :], pltpu.roll(u, 1, axis=0))
        u_next = jnp.where(local == TM - 1, uh[halo:halo + 1, :], pltpu.roll(u, TM - 1, axis=0))
        return (jnp.where(has_prev, u_prev, 0.0) * cw[0:1, :] + u * cw[1:2, :]
                + jnp.where(has_next, u_next, 0.0) * cw[2:3, :] + cb_ref[...])

    def up(act_ref):
        act_ref[...] = (_silu(conv(wg_ref, cwg_ref, cbg_ref))
                        * conv(wv_ref, cwv_ref, cbv_ref)).astype(BF16)

    def down(act_ref):
        out_ref[...] += jnp.dot(act_ref[...], wd_ref[...], preferred_element_type=F32)

    even = lax.rem(j, 2) == 0
    inner = (j > 0) & (j < nf)

    @pl.when(j == 0)
    def _():
        prologue()
        up(acta_ref)

    @pl.when(inner & even)
    def _():
        up(acta_ref)
        down(actb_ref)

    @pl.when(inner & jnp.logical_not(even))
    def _():
        up(actb_ref)
        down(acta_ref)

    @pl.when(j == nf)
    def _():
        down(acta_ref if (D_FF // TF - 1) % 2 == 0 else actb_ref)
        mod = mod_ref[...]
        gate = jnp.where(rows >= SEQ, mod[11:12, :], mod[10:11, :])
        y = x_ref[...] + gate * out_ref[...]
        if final:
            y = y * lax.rsqrt(jnp.mean(y * y, axis=-1, keepdims=True) + EPS) * fnorm_ref[...]
        out_ref[...] = y


def _ffn(xa, norm, mod, w_up, conv_w, conv_b, w_down, final_norm, final):
    nf = D_FF // TF
    hb = SUBLANES_F32
    nb = TM // hb
    last = T_ALL // hb - 1
    out_rows = SEQ if final else T_ALL
    cb = conv_b.reshape(1, 2 * D_FF)
    up_tile = lambda j: jnp.minimum(j, nf - 1)
    return pl.pallas_call(
        functools.partial(_ffn_kernel, final=final),
        grid=(T_ALL // TM, nf + 1),
        in_specs=[
            pl.BlockSpec((TM, D_MODEL), lambda i, j: (i, 0)),
            pl.BlockSpec((hb, D_MODEL), lambda i, j: (jnp.maximum(i * nb - 1, 0), 0)),
            pl.BlockSpec((hb, D_MODEL), lambda i, j: (jnp.minimum((i + 1) * nb, last), 0)),
            pl.BlockSpec((1, D_MODEL), lambda i, j: (0, 0)),
            pl.BlockSpec((16, D_MODEL), lambda i, j: (0, 0)),
            pl.BlockSpec((D_MODEL, TF), lambda i, j: (0, up_tile(j))),
            pl.BlockSpec((D_MODEL, TF), lambda i, j: (0, up_tile(j) + nf)),
            pl.BlockSpec((3, TF), lambda i, j: (0, up_tile(j))),
            pl.BlockSpec((3, TF), lambda i, j: (0, up_tile(j) + nf)),
            pl.BlockSpec((1, TF), lambda i, j: (0, up_tile(j))),
            pl.BlockSpec((1, TF), lambda i, j: (0, up_tile(j) + nf)),
            pl.BlockSpec((TF, D_MODEL), lambda i, j: (jnp.maximum(j - 1, 0), 0)),
            pl.BlockSpec((1, D_MODEL), lambda i, j: (0, 0)),
        ],
        out_specs=pl.BlockSpec((TM, D_MODEL), lambda i, j: (i, 0)),
        out_shape=jax.ShapeDtypeStruct((out_rows, D_MODEL), F32),
        scratch_shapes=[
            pltpu.VMEM((TM + 2 * hb, D_MODEL), BF16),
            pltpu.VMEM((TM, TF), BF16),
            pltpu.VMEM((TM, TF), BF16),
        ],
        compiler_params=pltpu.CompilerParams(
            dimension_semantics=("parallel", "arbitrary"), vmem_limit_bytes=VMEM_LIMIT),
        name="conv_ffn_final" if final else "conv_ffn",
    )(xa, xa, xa, norm.reshape(1, D_MODEL), mod, w_up, w_up, conv_w, conv_w, cb, cb, w_down,
      final_norm.reshape(1, D_MODEL))


def _per_chunk(rows):
    return jnp.concatenate([jnp.broadcast_to(t, (CHUNK, t.shape[-1])) for t in rows], axis=0)


def _gla_direction(q_ref, k_ref, v_refs, lr_ref, gw_ref, gb_ref, o_ref, st_ref, causal):
    n = GLA_BLOCK
    nc = n // CHUNK
    nt = (((1,), (1,)), ((), ()))
    r = lax.broadcasted_iota(jnp.int32, (n, n), 0)
    c = lax.broadcasted_iota(jnp.int32, (n, n), 1)
    shift = CHUNK.bit_length() - 1
    rc = jnp.right_shift(r, shift)
    cc = jnp.right_shift(c, shift)
    tri = (c <= r) if causal else (c >= r)
    delta = (rc - cc) if causal else (cc - rc)

    pre = jnp.dot(lr_ref[...].astype(BF16), gw_ref[...], preferred_element_type=F32) + gb_ref[...]
    g = (jnp.minimum(pre, 0.0) - jnp.log(1.0 + jnp.exp(-jnp.abs(pre)))) / GATE_NORMALIZER
    tri_b = jnp.where(tri, 1.0, 0.0).astype(BF16)
    g_hi = g.astype(BF16)
    g_lo = (g - g_hi.astype(F32)).astype(BF16)
    bb = (jnp.dot(tri_b, g_hi, preferred_element_type=F32)
          + jnp.dot(tri_b, g_lo, preferred_element_type=F32))

    zero = jnp.zeros((1, GLA_KEY), F32)
    if causal:
        edge = [zero] + [bb[ci * CHUNK - 1:ci * CHUNK, :] for ci in range(1, nc)]
        tot = [bb[ci * CHUNK + CHUNK - 1:ci * CHUNK + CHUNK, :] - edge[ci] for ci in range(nc)]
        ball = bb[n - 1:n, :]
        near = [zero] + tot[:nc - 1]
        far = [zero, zero] + tot[:nc - 2]
    else:
        edge = [bb[(ci + 1) * CHUNK:(ci + 1) * CHUNK + 1, :] for ci in range(nc - 1)] + [zero]
        tot = [bb[ci * CHUNK:ci * CHUNK + 1, :] - edge[ci] for ci in range(nc)]
        ball = bb[0:1, :]
        near = tot[1:] + [zero]
        far = tot[2:] + [zero, zero]
    b = bb - _per_chunk(edge)
    skip1 = _per_chunk([jnp.exp(t) for t in near])
    skip2 = _per_chunk([jnp.exp(t1 + t2) for t1, t2 in zip(near, far)])
    to_chunk_end = _per_chunk([jnp.exp(t) for t in tot])
    from_block_edge = _per_chunk([jnp.exp(e) for e in edge])
    to_block_end = _per_chunk([jnp.exp(ball - e - t) for e, t in zip(edge, tot)])
    dec_all = jnp.exp(ball)

    q = q_ref[...].astype(F32) * (GLA_DK ** -0.5)
    k = k_ref[...].astype(F32)
    qe_f = q * jnp.exp(b)
    ke_f = k * jnp.exp(-b)
    kd_f = ke_f * to_chunk_end
    qe = qe_f.astype(BF16)
    ke = ke_f.astype(BF16)
    kd = kd_f.astype(BF16)
    q1 = (qe_f * skip1).astype(BF16)
    q2 = (qe_f * skip2).astype(BF16)
    q_in = (qe_f * from_block_edge).astype(BF16)
    k_out = (kd_f * to_block_end).astype(BF16)

    for hd in range(GLA_HEADS):
        ks = slice(hd * GLA_DK, (hd + 1) * GLA_DK)
        v_half = v_refs[hd // (GLA_HEADS // 2)]
        v_col = (hd % (GLA_HEADS // 2)) * GLA_DV
        v = v_half[:, v_col:v_col + GLA_DV]
        q_far = jnp.concatenate([qe[:, ks], q1[:, ks], q2[:, ks]], axis=0)
        a0 = lax.dot_general(qe[:, ks], ke[:, ks], nt, preferred_element_type=F32)
        a123 = lax.dot_general(q_far, kd[:, ks], nt, preferred_element_type=F32)
        att = jnp.where((delta == 0) & tri, a0, 0.0)
        for d in range(1, nc):
            att = jnp.where(delta == d, a123[(d - 1) * n:d * n, :], att)
        st = st_ref[hd]
        o_ref[:, hd * GLA_DV:(hd + 1) * GLA_DV] = (
            jnp.dot(att.astype(BF16), v, preferred_element_type=F32)
            + lax.dot_general(q_in[:, ks], st.astype(BF16), nt, preferred_element_type=F32)
        ).astype(o_ref.dtype)
        upd = lax.dot_general(v, k_out[:, ks], (((0,), (0,)), ((), ())), preferred_element_type=F32)
        st_ref[hd] = st * dec_all[:, ks] + upd


def _gla_kernel(qf_ref, kf_ref, vf0_ref, vf1_ref, lrf_ref, qb_ref, kb_ref, vb0_ref, vb1_ref, lrb_ref,
                gwf_ref, gbf_ref, gwb_ref, gbb_ref, of_ref, ob_ref, sf_ref, sb_ref):
    @pl.when(pl.program_id(0) == 0)
    def _():
        sf_ref[...] = jnp.zeros_like(sf_ref)
        sb_ref[...] = jnp.zeros_like(sb_ref)

    _gla_direction(qf_ref, kf_ref, (vf0_ref, vf1_ref), lrf_ref, gwf_ref, gbf_ref, of_ref, sf_ref, True)
    _gla_direction(qb_ref, kb_ref, (vb0_ref, vb1_ref), lrb_ref, gwb_ref, gbb_ref, ob_ref, sb_ref, False)


def _gla(z, lr, gwf, gbf, gwb, gbb):
    nblk = T_ALL // GLA_BLOCK
    ctx_blk = SEQ // GLA_BLOCK
    fwd = lambda s: jnp.where(s == 0, ctx_blk, s - 1)
    bwd = lambda s: jnp.where(s == 0, ctx_blk, ctx_blk - s)

    def specs(order):
        return [
            pl.BlockSpec((GLA_BLOCK, GLA_KEY), lambda s: (order(s), 3)),
            pl.BlockSpec((GLA_BLOCK, GLA_KEY), lambda s: (order(s), 0)),
            pl.BlockSpec((GLA_BLOCK, GLA_KEY), lambda s: (order(s), 1)),
            pl.BlockSpec((GLA_BLOCK, GLA_KEY), lambda s: (order(s), 2)),
            pl.BlockSpec((GLA_BLOCK, LANES), lambda s: (order(s), 0)),
        ]

    gate_specs = [
        pl.BlockSpec((LANES, GLA_KEY), lambda s: (0, 0)),
        pl.BlockSpec((1, GLA_KEY), lambda s: (0, 0)),
    ]
    return pl.pallas_call(
        _gla_kernel,
        grid=(nblk,),
        in_specs=specs(fwd) + specs(bwd) + gate_specs + gate_specs,
        out_specs=[
            pl.BlockSpec((GLA_BLOCK, GLA_VAL), lambda s: (fwd(s), 0)),
            pl.BlockSpec((GLA_BLOCK, GLA_VAL), lambda s: (bwd(s), 0)),
        ],
        out_shape=[
            jax.ShapeDtypeStruct((T_ALL, GLA_VAL), BF16),
            jax.ShapeDtypeStruct((T_ALL, GLA_VAL), BF16),
        ],
        scratch_shapes=[
            pltpu.VMEM((GLA_HEADS, GLA_DV, GLA_DK), F32),
            pltpu.VMEM((GLA_HEADS, GLA_DV, GLA_DK), F32),
        ],
        compiler_params=pltpu.CompilerParams(
            dimension_semantics=("arbitrary",), vmem_limit_bytes=VMEM_LIMIT),
        name="gla_scan",
    )(z, z, z, z, lr, z, z, z, z, lr, gwf, gbf.reshape(1, GLA_KEY), gwb, gbb.reshape(1, GLA_KEY))


def _gla_out_kernel(x_ref, of_ref, ob_ref, og_ref, on_ref, mod_ref, w_ref, out_ref, h_ref):
    i = pl.program_id(0)
    tm = x_ref.shape[0]
    on = on_ref[...]
    mod = mod_ref[...]
    gate = jnp.where(_row_ids(i, tm) >= SEQ, mod[5:6, :], mod[4:5, :])
    for r0 in range(0, tm, ROW_GROUP):
        rs = slice(r0, r0 + ROW_GROUP)
        for hd in range(GLA_HEADS):
            sl = slice(hd * GLA_DV, (hd + 1) * GLA_DV)
            oh = of_ref[rs, sl].astype(F32) + ob_ref[rs, sl].astype(F32)
            y = oh * lax.rsqrt(jnp.mean(oh * oh, axis=-1, keepdims=True) + EPS) * on
            h_ref[rs, sl] = (y * _silu(og_ref[rs, sl].astype(F32))).astype(BF16)
    y = jnp.dot(h_ref[...], w_ref[...], preferred_element_type=F32)
    out_ref[...] = x_ref[...] + gate * y


def _gla_out(xa, o_f, o_b, z, o_norm, mod, w_out):
    tm = TMO
    og_blk = 2 * GLA_KEY // GLA_VAL + 1
    return pl.pallas_call(
        _gla_out_kernel,
        grid=(T_ALL // tm,),
        in_specs=[
            pl.BlockSpec((tm, D_MODEL), lambda i: (i, 0)),
            pl.BlockSpec((tm, GLA_VAL), lambda i: (i, 0)),
            pl.BlockSpec((tm, GLA_VAL), lambda i: (i, 0)),
            pl.BlockSpec((tm, GLA_VAL), lambda i: (i, og_blk)),
            pl.BlockSpec((1, GLA_DV), lambda i: (0, 0)),
            pl.BlockSpec((16, D_MODEL), lambda i: (0, 0)),
            pl.BlockSpec((GLA_VAL, D_MODEL), lambda i: (0, 0), pipeline_mode=pl.Buffered(1)),
        ],
        out_specs=pl.BlockSpec((tm, D_MODEL), lambda i: (i, 0)),
        out_shape=jax.ShapeDtypeStruct((T_ALL, D_MODEL), F32),
        scratch_shapes=[pltpu.VMEM((tm, GLA_VAL), BF16)],
        compiler_params=pltpu.CompilerParams(
            dimension_semantics=("parallel",), vmem_limit_bytes=VMEM_LIMIT),
        name="gla_out",
    )(xa, o_f, o_b, z, o_norm.reshape(1, GLA_DV), mod, w_out)


def _rot_cols(w):
    a1, a2, b1, b2 = jnp.split(w, 4, axis=-1)
    return jnp.concatenate([-a2, a1, -b2, b1], axis=-1)


def _rope_table():
    t = np.arange(SEQ)
    row = (t // GRID_W).astype(np.float32)
    col = (t % GRID_W).astype(np.float32)
    inv = (np.float32(ROPE_THETA) ** (-np.arange(0, AXIS_ROPE, 2, dtype=np.float32) / AXIS_ROPE)).astype(np.float32)
    ar = row[:, None] * inv
    ac = col[:, None] * inv
    ang = np.concatenate([ar, ar, ac, ac], axis=-1).astype(np.float64)
    lat = np.concatenate([np.cos(ang), np.sin(ang)], axis=-1)
    ctx = np.concatenate([np.ones((CTX_LEN, QK_ROPE)), np.zeros((CTX_LEN, QK_ROPE))], axis=-1)
    return jnp.asarray(np.concatenate([lat, ctx], axis=0), dtype=F32)


def kernel(x, c, ctx, c_ctx, l0_ada_w, l0_ada_b, l0_norm1, l0_w_in, l0_conv_a, l0_q_norm, l0_w_qb, l0_kv_norm, l0_w_kvb, l0_w_out, l0_norm2, l0_ffn_up, l0_ffn_conv_w, l0_ffn_conv_b, l0_ffn_down, l1_ada_w, l1_ada_b, l1_norm1, l1_w_in, l1_gate_fw_w, l1_gate_fw_b, l1_gate_bw_w, l1_gate_bw_b, l1_o_norm, l1_w_out, l1_norm2, l1_ffn_up, l1_ffn_conv_w, l1_ffn_conv_b, l1_ffn_down, final_norm):
    assert x.shape == (1, SEQ, D_MODEL) and ctx.shape == (1, CTX_LEN, D_MODEL)
    x_lat, x_ctx = x[0], ctx[0]
    cvec = jnp.concatenate([c, c_ctx[None, :], jnp.zeros((SUBLANES_F32 - 2, D_MODEL), F32)], axis=0)
    mod0 = _mod_rows(_ada(cvec, l0_ada_w, l0_ada_b))

    c_rope, c_q, c_conv = KV_LORA, KV_LORA + QK_ROPE, KV_LORA + QK_ROPE + Q_LORA
    w_conv0 = l0_w_in[:, c_conv:].astype(BF16)
    w_lat0 = jnp.concatenate([l0_w_in[:, c_q:c_conv], l0_w_in[:, :c_rope]], axis=1).astype(BF16)
    k_rope = l0_w_in[:, c_rope:c_q]
    w_kr = jnp.concatenate([k_rope, _rot_cols(k_rope)], axis=1).astype(BF16)
    z0, kr = _inproj(x_lat, x_ctx, l0_norm1, mod0, w_conv0, w_lat0, w_kr, tn=Q_LORA + KV_LORA)

    wq = l0_w_qb.reshape(Q_LORA, MLA_HEADS, QK_DIM)
    wqn = wq[:, :, :QK_NOPE].reshape(Q_LORA, MLA_HEADS * QK_NOPE).astype(BF16)
    wqr = wq[:, :, QK_NOPE:]
    wqrr = _rot_cols(wqr).reshape(Q_LORA, MLA_HEADS * QK_ROPE).astype(BF16)
    wqr = wqr.reshape(Q_LORA, MLA_HEADS * QK_ROPE).astype(BF16)
    wkv = l0_w_kvb.reshape(KV_LORA, MLA_HEADS, QK_NOPE + V_HEAD)
    wk = wkv[:, :, :QK_NOPE].reshape(KV_LORA, MLA_HEADS * QK_NOPE).astype(BF16)
    wv = wkv[:, :, QK_NOPE:].reshape(KV_LORA, MLA_HEADS * V_HEAD).astype(BF16)
    q, k, v = _mla_prep(z0, kr, _rope_table(), l0_q_norm, wqn, wqr, wqrr, l0_kv_norm, wk, wv)
    o, up0, down0, up1, down1, w_out0, w_out1, w_kv1, w_qo1, w_lr, mods1 = _attention(
        q, k, v, l0_ffn_up, l0_ffn_down, l1_ffn_up, l1_ffn_down, l0_w_out, l1_w_out, l1_w_in,
        cvec, l1_ada_w, l1_ada_b)
    mod1 = _mod_rows(mods1)
    xa = _ab_out(x_lat, x_ctx, z0, o, l0_conv_a, mod0, w_out0)
    xa = _ffn(xa, l0_norm2, mod0, up0, l0_ffn_conv_w, l0_ffn_conv_b, down0, final_norm, final=False)

    z1, lr = _inproj(xa, None, l1_norm1, mod1, w_kv1, w_qo1, w_lr, tn=1536)
    pad_f = jnp.zeros((LANES - GATE_RANK, GLA_KEY), F32)
    gwf = jnp.concatenate([l1_gate_fw_w, pad_f], axis=0).astype(BF16)
    gwb = jnp.concatenate([jnp.zeros((GATE_RANK, GLA_KEY), F32), l1_gate_bw_w,
                           jnp.zeros((LANES - 2 * GATE_RANK, GLA_KEY), F32)], axis=0).astype(BF16)
    o_f, o_b = _gla(z1, lr, gwf, l1_gate_fw_b, gwb, l1_gate_bw_b)
    xa = _gla_out(xa, o_f, o_b, z1, l1_o_norm, mod1, w_out1)
    out = _ffn(xa, l1_norm2, mod1, up1, l1_ffn_conv_w, l1_ffn_conv_b, down1, final_norm, final=True)
    return out[None]
```

```python
import functools

import jax
import jax.numpy as jnp
import numpy as np
from jax import lax
from jax.experimental import pallas as pl
from jax.experimental.pallas import tpu as pltpu

F32 = jnp.float32
BF16 = jnp.bfloat16

D_MODEL = 2048
SEQ = 8192
GRID_W = 64
CTX_LEN = 256
T_ALL = SEQ + CTX_LEN
EPS = 1e-6
LOG2_E = 1.4426950408889634
N_MOD = 6
MLA_HEADS = 8
QK_NOPE = 128
QK_ROPE = 64
QK_DIM = QK_NOPE + QK_ROPE
V_HEAD = 128
Q_LORA = 512
KV_LORA = 256
ROPE_THETA = 10000.0
AXIS_ROPE = QK_ROPE // 2
CONV_CH = D_MODEL // 2
GLA_HEADS = 4
GLA_DK = D_MODEL // 2 // GLA_HEADS
GLA_DV = D_MODEL // GLA_HEADS
GLA_KEY = GLA_HEADS * GLA_DK
GLA_VAL = GLA_HEADS * GLA_DV
GATE_RANK = 16
GATE_NORMALIZER = 16.0
CHUNK = 64
D_FF = 5632

LANES = 128
SUBLANES_F32 = 8
SUBLANES_BF16 = 16
VMEM_LIMIT = 56 * 1024 * 1024
MOD_ROWS = 16

TM = 768
TMO = 384
ROW_GROUP = 16
TQ = 256
GLA_BLOCK = 256
TF = 512


def _silu(v):
    return v / (1.0 + jnp.exp(-v))


def _row_ids(i, rows):
    return i * rows + lax.broadcasted_iota(jnp.int32, (rows, 1), 0)


def _modulated(x, norm, mod, shift_row, is_ctx):
    s = 2 * shift_row
    gain_lat = norm * (1.0 + mod[s + 2:s + 3, :])
    gain_ctx = norm * (1.0 + mod[s + 3:s + 4, :])
    y = x * lax.rsqrt(jnp.mean(x * x, axis=-1, keepdims=True) + EPS)
    return (y * jnp.where(is_ctx, gain_ctx, gain_lat)
            + jnp.where(is_ctx, mod[s + 1:s + 2, :], mod[s:s + 1, :]))


def _token_rows(x_ref, ctx_ref, is_ctx_tile, r0, n, cols=slice(None)):
    rows = x_ref[r0:r0 + n, cols]
    tile_rows = x_ref.shape[0]
    ctx_start = SEQ % tile_rows
    if ctx_ref is None or r0 + n <= ctx_start:
        return rows
    assert r0 >= ctx_start and ctx_start + CTX_LEN == tile_rows
    return jnp.where(is_ctx_tile, ctx_ref[r0 - ctx_start:r0 - ctx_start + n, cols], rows)


def _modulate_rows(x_ref, h_ref, norm, mod, shift_row, first_row, ctx_ref=None):
    tile_rows = x_ref.shape[0]
    ctx_start = SEQ % tile_rows
    assert ctx_start + CTX_LEN == tile_rows and ctx_start % ROW_GROUP == 0
    is_ctx_tile = first_row == (SEQ // tile_rows) * tile_rows
    s = 2 * shift_row
    gain_lat = norm * (1.0 + mod[s + 2:s + 3, :])
    gain_tail = jnp.where(is_ctx_tile, norm * (1.0 + mod[s + 3:s + 4, :]), gain_lat)
    shift_lat = mod[s:s + 1, :]
    shift_tail = jnp.where(is_ctx_tile, mod[s + 1:s + 2, :], shift_lat)
    for r0 in range(0, tile_rows, ROW_GROUP):
        x = _token_rows(x_ref, ctx_ref, is_ctx_tile, r0, ROW_GROUP)
        y = x * lax.rsqrt(jnp.mean(x * x, axis=-1, keepdims=True) + EPS)
        gain, shift = (gain_tail, shift_tail) if r0 >= ctx_start else (gain_lat, shift_lat)
        h_ref[r0:r0 + ROW_GROUP, :] = (y * gain + shift).astype(BF16)


def _ada_kernel(a_ref, w_ref, b_ref, o_ref):
    a = _silu(a_ref[...]).astype(BF16)
    o_ref[...] = jnp.dot(a, w_ref[...].astype(BF16), preferred_element_type=F32) + b_ref[...]


def _ada(cvec, ada_w, ada_b):
    n = ada_w.shape[1]
    tn = 1024
    return pl.pallas_call(
        _ada_kernel,
        grid=(n // tn,),
        in_specs=[
            pl.BlockSpec((SUBLANES_F32, D_MODEL), lambda j: (0, 0)),
            pl.BlockSpec((D_MODEL, tn), lambda j: (0, j)),
            pl.BlockSpec((1, tn), lambda j: (0, j)),
        ],
        out_specs=pl.BlockSpec((SUBLANES_F32, tn), lambda j: (0, j)),
        out_shape=jax.ShapeDtypeStruct((SUBLANES_F32, n), F32),
        compiler_params=pltpu.CompilerParams(
            dimension_semantics=("arbitrary",), vmem_limit_bytes=VMEM_LIMIT),
        name="ada_mod",
    )(cvec, ada_w, ada_b.reshape(1, n))


def _mod_rows(mods):
    m = mods[:2].reshape(2, N_MOD, D_MODEL)
    m = jnp.transpose(m, (1, 0, 2)).reshape(2 * N_MOD, D_MODEL)
    return jnp.concatenate([m, jnp.zeros((MOD_ROWS - 2 * N_MOD, D_MODEL), F32)], axis=0)


def _inproj_kernel(x_ref, ctx_ref, norm_ref, mod_ref, wa_ref, wb_ref, we_ref, z_ref, ze_ref, h_ref,
                   *, na, split_tokens):
    i = pl.program_id(0)
    j = pl.program_id(1)

    @pl.when(j == 0)
    def _():
        _modulate_rows(x_ref, h_ref, norm_ref[...], mod_ref[...], 0, i * TM,
                       ctx_ref if split_tokens else None)
        ze_ref[...] = jnp.dot(h_ref[...], we_ref[...], preferred_element_type=F32)
        part = TM // 3
        for r0 in range(0, TM, part):
            z_ref[r0:r0 + part, :] = jnp.dot(h_ref[r0:r0 + part, :], wa_ref[...],
                                             preferred_element_type=F32).astype(z_ref.dtype)

    @pl.when((j > 0) & (j < na))
    def _():
        z_ref[...] = jnp.dot(h_ref[...], wa_ref[...], preferred_element_type=F32).astype(z_ref.dtype)

    @pl.when(j >= na)
    def _():
        z_ref[...] = jnp.dot(h_ref[...], wb_ref[...], preferred_element_type=F32).astype(z_ref.dtype)


def _inproj(tokens, ctx, norm, mod, w_a, w_b, w_extra, tn):
    na = w_a.shape[1] // tn
    nb = w_b.shape[1] // tn
    n = (na + nb) * tn
    split_tokens = ctx is not None
    if not split_tokens:
        ctx = tokens
    wb_tile = lambda j: jnp.where(j == 0, nb - 1, jnp.maximum(j - na, 0))
    return pl.pallas_call(
        functools.partial(_inproj_kernel, na=na, split_tokens=split_tokens),
        grid=(T_ALL // TM, na + nb),
        in_specs=[
            pl.BlockSpec((TM, D_MODEL), lambda i, j: (i, 0)),
            pl.BlockSpec((CTX_LEN, D_MODEL), lambda i, j: (0, 0)),
            pl.BlockSpec((1, D_MODEL), lambda i, j: (0, 0)),
            pl.BlockSpec((MOD_ROWS, D_MODEL), lambda i, j: (0, 0)),
            pl.BlockSpec((D_MODEL, tn), lambda i, j: (0, jnp.minimum(j, na - 1))),
            pl.BlockSpec((D_MODEL, tn), lambda i, j: (0, wb_tile(j))),
            pl.BlockSpec((D_MODEL, LANES), lambda i, j: (0, 0)),
        ],
        out_specs=[
            pl.BlockSpec((TM, tn), lambda i, j: (i, j)),
            pl.BlockSpec((TM, LANES), lambda i, j: (i, 0)),
        ],
        out_shape=[
            jax.ShapeDtypeStruct((T_ALL, n), BF16),
            jax.ShapeDtypeStruct((T_ALL, LANES), F32),
        ],
        scratch_shapes=[pltpu.VMEM((TM, D_MODEL), BF16)],
        compiler_params=pltpu.CompilerParams(
            dimension_semantics=("parallel", "arbitrary"), vmem_limit_bytes=VMEM_LIMIT),
        name="in_proj",
    )(tokens, ctx, norm.reshape(1, D_MODEL), mod, w_a, w_b, w_extra)


def _mla_prep_kernel(ql_ref, kvl_ref, kr_ref, cs_ref, qn_ref, wqn_ref, wqr_ref, wqrr_ref,
                     kvn_ref, wk_ref, wv_ref, q_ref, k_ref, v_ref):
    cos = cs_ref[:, :QK_ROPE]
    sin = cs_ref[:, QK_ROPE:]

    ql = ql_ref[...].astype(F32)
    qn = (ql * lax.rsqrt(jnp.mean(ql * ql, axis=-1, keepdims=True) + EPS) * qn_ref[...]).astype(BF16)
    scale = QK_DIM ** -0.5 * LOG2_E
    q_nope = jnp.dot(qn, wqn_ref[...], preferred_element_type=F32) * scale
    q_r = jnp.dot(qn, wqr_ref[...], preferred_element_type=F32)
    q_rr = jnp.dot(qn, wqrr_ref[...], preferred_element_type=F32)

    kvl = kvl_ref[...].astype(F32)
    kvn = (kvl * lax.rsqrt(jnp.mean(kvl * kvl, axis=-1, keepdims=True) + EPS) * kvn_ref[...]).astype(BF16)
    k_nope = jnp.dot(kvn, wk_ref[...], preferred_element_type=F32)
    v = jnp.dot(kvn, wv_ref[...], preferred_element_type=F32)
    kr = kr_ref[...]
    k_rope = (kr[:, :QK_ROPE] * cos + kr[:, QK_ROPE:] * sin).astype(BF16)

    for h in range(MLA_HEADS):
        q_ref[h, :, :QK_NOPE] = q_nope[:, h * QK_NOPE:(h + 1) * QK_NOPE].astype(BF16)
        rope = q_r[:, h * QK_ROPE:(h + 1) * QK_ROPE] * cos + q_rr[:, h * QK_ROPE:(h + 1) * QK_ROPE] * sin
        q_ref[h, :, QK_NOPE:] = (rope * scale).astype(BF16)
        k_ref[h, :, :QK_NOPE] = k_nope[:, h * QK_NOPE:(h + 1) * QK_NOPE].astype(BF16)
        k_ref[h, :, QK_NOPE:] = k_rope
        v_ref[h, :, :V_HEAD] = v[:, h * V_HEAD:(h + 1) * V_HEAD].astype(BF16)
        v_ref[h, :, V_HEAD:] = jnp.ones((v.shape[0], V_HEAD), BF16)


def _mla_prep(z, kr, cs, q_norm, wqn, wqr, wqrr, kv_norm, wk, wv):
    tm = TM
    const = lambda shape: pl.BlockSpec(shape, lambda i: (0,) * len(shape))
    return pl.pallas_call(
        _mla_prep_kernel,
        grid=(T_ALL // tm,),
        in_specs=[
            pl.BlockSpec((tm, Q_LORA), lambda i: (i, 3 * CONV_CH // Q_LORA)),
            pl.BlockSpec((tm, KV_LORA), lambda i: (i, (3 * CONV_CH + Q_LORA) // KV_LORA)),
            pl.BlockSpec((tm, LANES), lambda i: (i, 0)),
            pl.BlockSpec((tm, LANES), lambda i: (i, 0)),
            const((1, Q_LORA)),
            const((Q_LORA, MLA_HEADS * QK_NOPE)),
            const((Q_LORA, MLA_HEADS * QK_ROPE)),
            const((Q_LORA, MLA_HEADS * QK_ROPE)),
            const((1, KV_LORA)),
            const((KV_LORA, MLA_HEADS * QK_NOPE)),
            const((KV_LORA, MLA_HEADS * V_HEAD)),
        ],
        out_specs=[
            pl.BlockSpec((MLA_HEADS, tm, QK_DIM), lambda i: (0, i, 0)),
            pl.BlockSpec((MLA_HEADS, tm, QK_DIM), lambda i: (0, i, 0)),
            pl.BlockSpec((MLA_HEADS, tm, 2 * V_HEAD), lambda i: (0, i, 0)),
        ],
        out_shape=[
            jax.ShapeDtypeStruct((MLA_HEADS, T_ALL, QK_DIM), BF16),
            jax.ShapeDtypeStruct((MLA_HEADS, T_ALL, QK_DIM), BF16),
            jax.ShapeDtypeStruct((MLA_HEADS, T_ALL, 2 * V_HEAD), BF16),
        ],
        compiler_params=pltpu.CompilerParams(
            dimension_semantics=("parallel",), vmem_limit_bytes=VMEM_LIMIT),
        name="mla_prep",
    )(z, z, kr, cs, q_norm.reshape(1, Q_LORA), wqn, wqr, wqrr, kv_norm.reshape(1, KV_LORA), wk, wv)


def _attn_kernel(qa_ref, qn_ref, k_ref, v_ref, up0_ref, dn0_ref, up1_ref, dn1_ref, wo0_ref, wo1_ref,
                 win1_ref, cvec_ref, adaw_ref, adab_ref,
                 o_ref, up0_o, dn0_o, up1_o, dn1_o, wo0_o, wo1_o, wkv_o, wqo_o, wlr_o, mods_o,
                 sa_ref, sb_ref):
    j = pl.program_id(1)
    n_pairs = SEQ // (2 * TQ)

    def convert_weight_slabs():
        for src, dst in ((up0_ref, up0_o), (dn0_ref, dn0_o), (up1_ref, up1_o), (dn1_ref, dn1_o),
                         (wo0_ref, wo0_o), (wo1_ref, wo1_o)):
            dst[...] = src[...].astype(BF16)
        _ada_kernel(cvec_ref, adaw_ref, adab_ref, mods_o)
        w1 = win1_ref[...]
        kv_cols = GLA_KEY + GLA_VAL
        wkv_o[...] = w1[:, :kv_cols].astype(BF16)
        wqo_o[...] = w1[:, kv_cols + 2 * GATE_RANK:].astype(BF16)
        wlr_o[...] = jnp.concatenate(
            [w1[:, kv_cols:kv_cols + 2 * GATE_RANK],
             jnp.zeros((w1.shape[0], LANES - 2 * GATE_RANK), F32)], axis=1).astype(BF16)

    def score(q, s_ref):
        s_ref[...] = lax.dot_general(q, k_ref[0], (((1,), (1,)), ((), ())), preferred_element_type=F32)

    def attend(s, v, rows):
        p = jnp.exp2(s - jnp.max(s, axis=-1, keepdims=True)).astype(BF16)
        acc = jnp.dot(p, v, preferred_element_type=F32)
        o_ref[rows, :] = (acc[:, :V_HEAD] / acc[:, V_HEAD:V_HEAD + 1]).astype(o_ref.dtype)

    @pl.when(j == 0)
    def _():
        score(qa_ref[0, :TQ, :], sa_ref)

    @pl.when(j < n_pairs)
    def _():
        score(qa_ref[0, TQ:, :], sb_ref)
        attend(sa_ref[...], v_ref[0], slice(0, TQ))
        score(qn_ref[0], sa_ref)
        attend(sb_ref[...], v_ref[0], slice(TQ, 2 * TQ))
        convert_weight_slabs()

    @pl.when(j == n_pairs)
    def _():
        convert_weight_slabs()
        attend(sa_ref[:, SEQ:], v_ref[0, SEQ:, :], slice(0, TQ))
        o_ref[TQ:, :] = jnp.zeros((TQ, V_HEAD), o_ref.dtype)


def _attention(q, k, v, ffn_up0, ffn_down0, ffn_up1, ffn_down1, w_out0, w_out1, w_in1,
               cvec, ada_w1, ada_b1):
    n_pairs = SEQ // (2 * TQ)
    n_j = n_pairs + 1
    n_mod = ada_w1.shape[1]
    ada_col = lambda h, j: (0, jnp.minimum(h * n_j + j, n_mod // LANES - 1))
    assert MLA_HEADS * n_j >= n_mod // LANES

    def slab(n_rows, rows, cols_in, cols_out=None):
        assert n_rows % rows == 0 and MLA_HEADS * n_j >= n_rows // rows
        index = lambda h, j: (jnp.minimum(h * n_j + j, n_rows // rows - 1), 0)
        return (pl.BlockSpec((rows, cols_in), index),
                pl.BlockSpec((rows, cols_in if cols_out is None else cols_out), index))

    up_in, up_out = slab(D_MODEL, SUBLANES_BF16, 2 * D_FF)
    dn_in, dn_out = slab(D_FF, 4 * SUBLANES_BF16, D_MODEL)
    wo_in, wo_out = slab(D_MODEL, SUBLANES_BF16, D_MODEL)
    n_in1 = w_in1.shape[1]
    kv_cols = GLA_KEY + GLA_VAL
    w1_in, w1_kv = slab(D_MODEL, SUBLANES_BF16, n_in1, kv_cols)
    _, w1_qo = slab(D_MODEL, SUBLANES_BF16, n_in1, n_in1 - kv_cols - 2 * GATE_RANK)
    _, w1_lr = slab(D_MODEL, SUBLANES_BF16, n_in1, LANES)
    bf = lambda shape: jax.ShapeDtypeStruct(shape, BF16)
    return pl.pallas_call(
        _attn_kernel,
        grid=(MLA_HEADS, n_j),
        in_specs=[
            pl.BlockSpec((1, 2 * TQ, QK_DIM), lambda h, j: (h, jnp.minimum(j, n_pairs - 1), 0)),
            pl.BlockSpec((1, TQ, QK_DIM), lambda h, j: (h, jnp.minimum(2 * j + 2, SEQ // TQ), 0)),
            pl.BlockSpec((1, T_ALL, QK_DIM), lambda h, j: (h, 0, 0)),
            pl.BlockSpec((1, T_ALL, 2 * V_HEAD), lambda h, j: (h, 0, 0)),
            up_in, dn_in, up_in, dn_in, wo_in, wo_in, w1_in,
            pl.BlockSpec((SUBLANES_F32, D_MODEL), lambda h, j: (0, 0)),
            pl.BlockSpec((D_MODEL, LANES), ada_col),
            pl.BlockSpec((1, LANES), ada_col),
        ],
        out_specs=[
            pl.BlockSpec((2 * TQ, V_HEAD), lambda h, j: (j, h)),
            up_out, dn_out, up_out, dn_out, wo_out, wo_out, w1_kv, w1_qo, w1_lr,
            pl.BlockSpec((SUBLANES_F32, LANES), ada_col),
        ],
        out_shape=[
            bf((n_j * 2 * TQ, MLA_HEADS * V_HEAD)),
            bf(ffn_up0.shape), bf(ffn_down0.shape), bf(ffn_up1.shape), bf(ffn_down1.shape),
            bf(w_out0.shape), bf(w_out1.shape),
            bf((D_MODEL, kv_cols)), bf((D_MODEL, n_in1 - kv_cols - 2 * GATE_RANK)), bf((D_MODEL, LANES)),
            jax.ShapeDtypeStruct((SUBLANES_F32, n_mod), F32),
        ],
        scratch_shapes=[pltpu.VMEM((TQ, T_ALL), F32), pltpu.VMEM((TQ, T_ALL), F32)],
        compiler_params=pltpu.CompilerParams(
            dimension_semantics=("arbitrary", "arbitrary"), vmem_limit_bytes=VMEM_LIMIT),
        name="mla_attn",
    )(q, q, k, v, ffn_up0, ffn_down0, ffn_up1, ffn_down1, w_out0, w_out1, w_in1,
      cvec, ada_w1, ada_b1.reshape(1, n_mod))


def _neighbour_masks(rows):
    has_prev = (rows != 0) & (rows != SEQ)
    has_next = (rows != SEQ - 1) & (rows != T_ALL - 1)
    return has_prev, has_next


def _ab_out_kernel(x_ref, ctx_ref, ax_ref, ab_ref, ac_ref, axp_ref, acp_ref, axn_ref, acn_ref, o_ref,
                   cw_ref, mod_ref, w_ref, out_ref, h_ref, p_ref):
    i = pl.program_id(0)
    tm = x_ref.shape[0]
    halo = SUBLANES_F32
    hb = SUBLANES_BF16
    f32 = lambda ref, r0, n: ref[r0:r0 + n, :].astype(F32)

    p_ref[halo - 1:halo, :] = f32(acp_ref, hb - 1, 1) * f32(axp_ref, hb - 1, 1)
    p_ref[halo + tm:halo + tm + 1, :] = f32(acn_ref, 0, 1) * f32(axn_ref, 0, 1)
    for r0 in range(0, tm, ROW_GROUP):
        p_ref[halo + r0:halo + r0 + ROW_GROUP, :] = f32(ac_ref, r0, ROW_GROUP) * f32(ax_ref, r0, ROW_GROUP)

    cw = cw_ref[...]
    mod = mod_ref[...]
    gate = jnp.where(_row_ids(i, tm) >= SEQ, mod[5:6, :], mod[4:5, :])
    ctx_start = SEQ % tm
    is_ctx_tile = i == SEQ // tm

    def residual(cols):
        return jnp.concatenate(
            [_token_rows(x_ref, ctx_ref, is_ctx_tile, 0, ctx_start, cols),
             _token_rows(x_ref, ctx_ref, is_ctx_tile, ctx_start, tm - ctx_start, cols)], axis=0)

    for r0 in range(0, tm, ROW_GROUP):
        rows = i * tm + r0 + lax.broadcasted_iota(jnp.int32, (ROW_GROUP, 1), 0)
        has_prev, has_next = _neighbour_masks(rows)
        at = lambda shift: p_ref[halo + r0 + shift:halo + r0 + shift + ROW_GROUP, :]
        conv = (jnp.where(has_prev, at(-1), 0.0) * cw[0:1, :] + at(0) * cw[1:2, :]
                + jnp.where(has_next, at(1), 0.0) * cw[2:3, :])
        h_ref[r0:r0 + ROW_GROUP, :CONV_CH] = (f32(ab_ref, r0, ROW_GROUP) * conv).astype(BF16)
    h_ref[:, CONV_CH:] = o_ref[...]
    y = jnp.dot(h_ref[...], w_ref[...], preferred_element_type=F32)
    out_ref[...] = residual(slice(None)) + gate * y


def _ab_out(x_lat, ctx, z, o, conv_a, mod, w_out):
    tm = TMO
    hb = SUBLANES_BF16
    nb = tm // hb
    last = T_ALL // hb - 1
    prev_map = lambda c: (lambda i: (jnp.maximum(i * nb - 1, 0), c))
    next_map = lambda c: (lambda i: (jnp.minimum((i + 1) * nb, last), c))
    return pl.pallas_call(
        _ab_out_kernel,
        grid=(T_ALL // tm,),
        in_specs=[
            pl.BlockSpec((tm, D_MODEL), lambda i: (i, 0)),
            pl.BlockSpec((CTX_LEN, D_MODEL), lambda i: (0, 0)),
            pl.BlockSpec((tm, CONV_CH), lambda i: (i, 0)),
            pl.BlockSpec((tm, CONV_CH), lambda i: (i, 1)),
            pl.BlockSpec((tm, CONV_CH), lambda i: (i, 2)),
            pl.BlockSpec((hb, CONV_CH), prev_map(0)),
            pl.BlockSpec((hb, CONV_CH), prev_map(2)),
            pl.BlockSpec((hb, CONV_CH), next_map(0)),
            pl.BlockSpec((hb, CONV_CH), next_map(2)),
            pl.BlockSpec((tm, MLA_HEADS * V_HEAD), lambda i: (i, 0)),
            pl.BlockSpec((3, CONV_CH), lambda i: (0, 0)),
            pl.BlockSpec((MOD_ROWS, D_MODEL), lambda i: (0, 0)),
            pl.BlockSpec((D_MODEL, D_MODEL), lambda i: (0, 0), pipeline_mode=pl.Buffered(1)),
        ],
        out_specs=pl.BlockSpec((tm, D_MODEL), lambda i: (i, 0)),
        out_shape=jax.ShapeDtypeStruct((T_ALL, D_MODEL), F32),
        scratch_shapes=[
            pltpu.VMEM((tm, D_MODEL), BF16),
            pltpu.VMEM((tm + 2 * SUBLANES_F32, CONV_CH), F32),
        ],
        compiler_params=pltpu.CompilerParams(
            dimension_semantics=("parallel",), vmem_limit_bytes=VMEM_LIMIT),
        name="ab_out",
    )(x_lat, ctx, z, z, z, z, z, z, z, o, conv_a, mod, w_out)


def _ffn_kernel(x_ref, xp_ref, xn_ref, norm_ref, mod_ref, wg_ref, wv_ref, cwg_ref, cwv_ref,
                cbg_ref, cbv_ref, wd_ref, fnorm_ref, out_ref, h_ref, acta_ref, actb_ref,
                *, final):
    i = pl.program_id(0)
    j = pl.program_id(1)
    nf = pl.num_programs(1) - 1
    halo = SUBLANES_F32
    rows = _row_ids(i, TM)

    def prologue():
        norm = norm_ref[...]
        mod = mod_ref[...]
        _modulate_rows(x_ref, h_ref, norm, mod, 3, i * TM)
        xh = jnp.concatenate([xp_ref[...], xn_ref[...]], axis=0)
        off = lax.broadcasted_iota(jnp.int32, (2 * halo, 1), 0)
        rows_h = jnp.where(off < halo, i * TM - halo + off, (i + 1) * TM - halo + off)
        h_ref[TM:, :] = _modulated(xh, norm, mod, 3, rows_h >= SEQ).astype(BF16)
        out_ref[...] = jnp.zeros_like(out_ref)

    def conv(w_ref, cw_ref, cb_ref):
        has_prev, has_next = _neighbour_masks(rows)
        local = lax.broadcasted_iota(jnp.int32, (TM, 1), 0)
        w = w_ref[...]
        part = TM // 3
        u_a = jnp.dot(h_ref[:part, :], w, preferred_element_type=F32)
        u_b = jnp.dot(h_ref[part:2 * part, :], w, preferred_element_type=F32)
        u_c = jnp.dot(h_ref[2 * part:, :], w, preferred_element_type=F32)
        u = jnp.concatenate([u_a, u_b, u_c[:part, :]], axis=0)
        uh = u_c[part:, :]
        cw = cw_ref[...]
        u_prev = jnp.where(local == 0, uh[halo - 1:halo, :], pltpu.roll(u, 1, axis=0))
        u_next = jnp.where(local == TM - 1, uh[halo:halo + 1, :], pltpu.roll(u, TM - 1, axis=0))
        return (jnp.where(has_prev, u_prev, 0.0) * cw[0:1, :] + u * cw[1:2, :]
                + jnp.where(has_next, u_next, 0.0) * cw[2:3, :] + cb_ref[...])

    def up(act_ref):
        act_ref[...] = (_silu(conv(wg_ref, cwg_ref, cbg_ref))
                        * conv(wv_ref, cwv_ref, cbv_ref)).astype(BF16)

    def down(act_ref):
        out_ref[...] += jnp.dot(act_ref[...], wd_ref[...], preferred_element_type=F32)

    even = lax.rem(j, 2) == 0
    inner = (j > 0) & (j < nf)

    @pl.when(j == 0)
    def _():
        prologue()
        up(acta_ref)

    @pl.when(inner & even)
    def _():
        up(acta_ref)
        down(actb_ref)

    @pl.when(inner & jnp.logical_not(even))
    def _():
        up(actb_ref)
        down(acta_ref)

    @pl.when(j == nf)
    def _():
        down(acta_ref if (D_FF // TF - 1) % 2 == 0 else actb_ref)
        mod = mod_ref[...]
        gate = jnp.where(rows >= SEQ, mod[11:12, :], mod[10:11, :])
        y = x_ref[...] + gate * out_ref[...]
        if final:
            y = y * lax.rsqrt(jnp.mean(y * y, axis=-1, keepdims=True) + EPS) * fnorm_ref[...]
        out_ref[...] = y


def _ffn(xa, norm, mod, w_up, conv_w, conv_b, w_down, final_norm, final):
    nf = D_FF // TF
    hb = SUBLANES_F32
    nb = TM // hb
    last = T_ALL // hb - 1
    out_rows = SEQ if final else T_ALL
    cb = conv_b.reshape(1, 2 * D_FF)
    up_tile = lambda j: jnp.minimum(j, nf - 1)
    return pl.pallas_call(
        functools.partial(_ffn_kernel, final=final),
        grid=(T_ALL // TM, nf + 1),
        in_specs=[
            pl.BlockSpec((TM, D_MODEL), lambda i, j: (i, 0)),
            pl.BlockSpec((hb, D_MODEL), lambda i, j: (jnp.maximum(i * nb - 1, 0), 0)),
            pl.BlockSpec((hb, D_MODEL), lambda i, j: (jnp.minimum((i + 1) * nb, last), 0)),
            pl.BlockSpec((1, D_MODEL), lambda i, j: (0, 0)),
            pl.BlockSpec((MOD_ROWS, D_MODEL), lambda i, j: (0, 0)),
            pl.BlockSpec((D_MODEL, TF), lambda i, j: (0, up_tile(j))),
            pl.BlockSpec((D_MODEL, TF), lambda i, j: (0, up_tile(j) + nf)),
            pl.BlockSpec((3, TF), lambda i, j: (0, up_tile(j))),
            pl.BlockSpec((3, TF), lambda i, j: (0, up_tile(j) + nf)),
            pl.BlockSpec((1, TF), lambda i, j: (0, up_tile(j))),
            pl.BlockSpec((1, TF), lambda i, j: (0, up_tile(j) + nf)),
            pl.BlockSpec((TF, D_MODEL), lambda i, j: (jnp.maximum(j - 1, 0), 0)),
            pl.BlockSpec((1, D_MODEL), lambda i, j: (0, 0)),
        ],
        out_specs=pl.BlockSpec((TM, D_MODEL), lambda i, j: (i, 0)),
        out_shape=jax.ShapeDtypeStruct((out_rows, D_MODEL), F32),
        scratch_shapes=[
            pltpu.VMEM((TM + 2 * hb, D_MODEL), BF16),
            pltpu.VMEM((TM, TF), BF16),
            pltpu.VMEM((TM, TF), BF16),
        ],
        compiler_params=pltpu.CompilerParams(
            dimension_semantics=("parallel", "arbitrary"), vmem_limit_bytes=VMEM_LIMIT),
        name="conv_ffn_final" if final else "conv_ffn",
    )(xa, xa, xa, norm.reshape(1, D_MODEL), mod, w_up, w_up, conv_w, conv_w, cb, cb, w_down,
      final_norm.reshape(1, D_MODEL))


def _per_chunk(rows):
    return jnp.concatenate([jnp.broadcast_to(t, (CHUNK, t.shape[-1])) for t in rows], axis=0)


def _gla_direction(q_ref, k_ref, v_refs, lr_ref, gw_ref, gb_ref, o_ref, st_ref, causal):
    n = GLA_BLOCK
    nc = n // CHUNK
    nt = (((1,), (1,)), ((), ()))
    r = lax.broadcasted_iota(jnp.int32, (n, n), 0)
    c = lax.broadcasted_iota(jnp.int32, (n, n), 1)
    shift = CHUNK.bit_length() - 1
    rc = jnp.right_shift(r, shift)
    cc = jnp.right_shift(c, shift)
    tri = (c <= r) if causal else (c >= r)
    delta = (rc - cc) if causal else (cc - rc)

    pre = jnp.dot(lr_ref[...].astype(BF16), gw_ref[...], preferred_element_type=F32) + gb_ref[...]
    g = (jnp.minimum(pre, 0.0) - jnp.log(1.0 + jnp.exp(-jnp.abs(pre)))) / GATE_NORMALIZER
    tri_b = jnp.where(tri, 1.0, 0.0).astype(BF16)
    g_hi = g.astype(BF16)
    g_lo = (g - g_hi.astype(F32)).astype(BF16)
    bb = (jnp.dot(tri_b, g_hi, preferred_element_type=F32)
          + jnp.dot(tri_b, g_lo, preferred_element_type=F32))

    zero = jnp.zeros((1, GLA_KEY), F32)
    if causal:
        edge = [zero] + [bb[ci * CHUNK - 1:ci * CHUNK, :] for ci in range(1, nc)]
        tot = [bb[ci * CHUNK + CHUNK - 1:ci * CHUNK + CHUNK, :] - edge[ci] for ci in range(nc)]
        ball = bb[n - 1:n, :]
        near = [zero] + tot[:nc - 1]
        far = [zero, zero] + tot[:nc - 2]
    else:
        edge = [bb[(ci + 1) * CHUNK:(ci + 1) * CHUNK + 1, :] for ci in range(nc - 1)] + [zero]
        tot = [bb[ci * CHUNK:ci * CHUNK + 1, :] - edge[ci] for ci in range(nc)]
        ball = bb[0:1, :]
        near = tot[1:] + [zero]
        far = tot[2:] + [zero, zero]
    b = bb - _per_chunk(edge)
    skip1 = _per_chunk([jnp.exp(t) for t in near])
    skip2 = _per_chunk([jnp.exp(t1 + t2) for t1, t2 in zip(near, far)])
    to_chunk_end = _per_chunk([jnp.exp(t) for t in tot])
    from_block_edge = _per_chunk([jnp.exp(e) for e in edge])
    to_block_end = _per_chunk([jnp.exp(ball - e - t) for e, t in zip(edge, tot)])
    dec_all = jnp.exp(ball)

    q = q_ref[...].astype(F32) * (GLA_DK ** -0.5)
    k = k_ref[...].astype(F32)
    qe_f = q * jnp.exp(b)
    ke_f = k * jnp.exp(-b)
    kd_f = ke_f * to_chunk_end
    qe = qe_f.astype(BF16)
    ke = ke_f.astype(BF16)
    kd = kd_f.astype(BF16)
    q1 = (qe_f * skip1).astype(BF16)
    q2 = (qe_f * skip2).astype(BF16)
    q_in = (qe_f * from_block_edge).astype(BF16)
    k_out = (kd_f * to_block_end).astype(BF16)

    for hd in range(GLA_HEADS):
        ks = slice(hd * GLA_DK, (hd + 1) * GLA_DK)
        v_half = v_refs[hd // (GLA_HEADS // 2)]
        v_col = (hd % (GLA_HEADS // 2)) * GLA_DV
        v = v_half[:, v_col:v_col + GLA_DV]
        q_far = jnp.concatenate([qe[:, ks], q1[:, ks], q2[:, ks]], axis=0)
        a0 = lax.dot_general(qe[:, ks], ke[:, ks], nt, preferred_element_type=F32)
        a123 = lax.dot_general(q_far, kd[:, ks], nt, preferred_element_type=F32)
        att = jnp.where((delta == 0) & tri, a0, 0.0)
        for d in range(1, nc):
            att = jnp.where(delta == d, a123[(d - 1) * n:d * n, :], att)
        st = st_ref[hd]
        o_ref[:, hd * GLA_DV:(hd + 1) * GLA_DV] = (
            jnp.dot(att.astype(BF16), v, preferred_element_type=F32)
            + lax.dot_general(q_in[:, ks], st.astype(BF16), nt, preferred_element_type=F32)
        ).astype(o_ref.dtype)
        upd = lax.dot_general(v, k_out[:, ks], (((0,), (0,)), ((), ())), preferred_element_type=F32)
        st_ref[hd] = st * dec_all[:, ks] + upd


def _gla_kernel(qf_ref, kf_ref, vf0_ref, vf1_ref, lrf_ref, qb_ref, kb_ref, vb0_ref, vb1_ref, lrb_ref,
                gwf_ref, gbf_ref, gwb_ref, gbb_ref, of_ref, ob_ref, sf_ref, sb_ref):
    @pl.when(pl.program_id(0) == 0)
    def _():
        sf_ref[...] = jnp.zeros_like(sf_ref)
        sb_ref[...] = jnp.zeros_like(sb_ref)

    _gla_direction(qf_ref, kf_ref, (vf0_ref, vf1_ref), lrf_ref, gwf_ref, gbf_ref, of_ref, sf_ref, True)
    _gla_direction(qb_ref, kb_ref, (vb0_ref, vb1_ref), lrb_ref, gwb_ref, gbb_ref, ob_ref, sb_ref, False)


def _gla(z, lr, gwf, gbf, gwb, gbb):
    nblk = T_ALL // GLA_BLOCK
    ctx_blk = SEQ // GLA_BLOCK
    fwd = lambda s: jnp.where(s == 0, ctx_blk, s - 1)
    bwd = lambda s: jnp.where(s == 0, ctx_blk, ctx_blk - s)

    def specs(order):
        return [
            pl.BlockSpec((GLA_BLOCK, GLA_KEY), lambda s: (order(s), 3)),
            pl.BlockSpec((GLA_BLOCK, GLA_KEY), lambda s: (order(s), 0)),
            pl.BlockSpec((GLA_BLOCK, GLA_KEY), lambda s: (order(s), 1)),
            pl.BlockSpec((GLA_BLOCK, GLA_KEY), lambda s: (order(s), 2)),
            pl.BlockSpec((GLA_BLOCK, LANES), lambda s: (order(s), 0)),
        ]

    gate_specs = [
        pl.BlockSpec((LANES, GLA_KEY), lambda s: (0, 0)),
        pl.BlockSpec((1, GLA_KEY), lambda s: (0, 0)),
    ]
    return pl.pallas_call(
        _gla_kernel,
        grid=(nblk,),
        in_specs=specs(fwd) + specs(bwd) + gate_specs + gate_specs,
        out_specs=[
            pl.BlockSpec((GLA_BLOCK, GLA_VAL), lambda s: (fwd(s), 0)),
            pl.BlockSpec((GLA_BLOCK, GLA_VAL), lambda s: (bwd(s), 0)),
        ],
        out_shape=[
            jax.ShapeDtypeStruct((T_ALL, GLA_VAL), BF16),
            jax.ShapeDtypeStruct((T_ALL, GLA_VAL), BF16),
        ],
        scratch_shapes=[
            pltpu.VMEM((GLA_HEADS, GLA_DV, GLA_DK), F32),
            pltpu.VMEM((GLA_HEADS, GLA_DV, GLA_DK), F32),
        ],
        compiler_params=pltpu.CompilerParams(
            dimension_semantics=("arbitrary",), vmem_limit_bytes=VMEM_LIMIT),
        name="gla_scan",
    )(z, z, z, z, lr, z, z, z, z, lr, gwf, gbf.reshape(1, GLA_KEY), gwb, gbb.reshape(1, GLA_KEY))


def _gla_out_kernel(x_ref, of_ref, ob_ref, og_ref, on_ref, mod_ref, w_ref, out_ref, h_ref):
    i = pl.program_id(0)
    tm = x_ref.shape[0]
    on = on_ref[...]
    mod = mod_ref[...]
    gate = jnp.where(_row_ids(i, tm) >= SEQ, mod[5:6, :], mod[4:5, :])
    for r0 in range(0, tm, ROW_GROUP):
        rs = slice(r0, r0 + ROW_GROUP)
        for hd in range(GLA_HEADS):
            sl = slice(hd * GLA_DV, (hd + 1) * GLA_DV)
            oh = of_ref[rs, sl].astype(F32) + ob_ref[rs, sl].astype(F32)
            y = oh * lax.rsqrt(jnp.mean(oh * oh, axis=-1, keepdims=True) + EPS) * on
            h_ref[rs, sl] = (y * _silu(og_ref[rs, sl].astype(F32))).astype(BF16)
    y = jnp.dot(h_ref[...], w_ref[...], preferred_element_type=F32)
    out_ref[...] = x_ref[...] + gate * y


def _gla_out(xa, o_f, o_b, z, o_norm, mod, w_out):
    tm = TMO
    og_blk = 2 * GLA_KEY // GLA_VAL + 1
    return pl.pallas_call(
        _gla_out_kernel,
        grid=(T_ALL // tm,),
        in_specs=[
            pl.BlockSpec((tm, D_MODEL), lambda i: (i, 0)),
            pl.BlockSpec((tm, GLA_VAL), lambda i: (i, 0)),
            pl.BlockSpec((tm, GLA_VAL), lambda i: (i, 0)),
            pl.BlockSpec((tm, GLA_VAL), lambda i: (i, og_blk)),
            pl.BlockSpec((1, GLA_DV), lambda i: (0, 0)),
            pl.BlockSpec((MOD_ROWS, D_MODEL), lambda i: (0, 0)),
            pl.BlockSpec((GLA_VAL, D_MODEL), lambda i: (0, 0), pipeline_mode=pl.Buffered(1)),
        ],
        out_specs=pl.BlockSpec((tm, D_MODEL), lambda i: (i, 0)),
        out_shape=jax.ShapeDtypeStruct((T_ALL, D_MODEL), F32),
        scratch_shapes=[pltpu.VMEM((tm, GLA_VAL), BF16)],
        compiler_params=pltpu.CompilerParams(
            dimension_semantics=("parallel",), vmem_limit_bytes=VMEM_LIMIT),
        name="gla_out",
    )(xa, o_f, o_b, z, o_norm.reshape(1, GLA_DV), mod, w_out)


def _rot_cols(w):
    a1, a2, b1, b2 = jnp.split(w, 4, axis=-1)
    return jnp.concatenate([-a2, a1, -b2, b1], axis=-1)


def _rope_table():
    t = np.arange(SEQ)
    row = (t // GRID_W).astype(np.float32)
    col = (t % GRID_W).astype(np.float32)
    inv = (np.float32(ROPE_THETA) ** (-np.arange(0, AXIS_ROPE, 2, dtype=np.float32) / AXIS_ROPE)).astype(np.float32)
    ar = row[:, None] * inv
    ac = col[:, None] * inv
    ang = np.concatenate([ar, ar, ac, ac], axis=-1).astype(np.float64)
    lat = np.concatenate([np.cos(ang), np.sin(ang)], axis=-1)
    ctx = np.concatenate([np.ones((CTX_LEN, QK_ROPE)), np.zeros((CTX_LEN, QK_ROPE))], axis=-1)
    return jnp.asarray(np.concatenate([lat, ctx], axis=0), dtype=F32)


def kernel(x, c, ctx, c_ctx, l0_ada_w, l0_ada_b, l0_norm1, l0_w_in, l0_conv_a, l0_q_norm, l0_w_qb, l0_kv_norm, l0_w_kvb, l0_w_out, l0_norm2, l0_ffn_up, l0_ffn_conv_w, l0_ffn_conv_b, l0_ffn_down, l1_ada_w, l1_ada_b, l1_norm1, l1_w_in, l1_gate_fw_w, l1_gate_fw_b, l1_gate_bw_w, l1_gate_bw_b, l1_o_norm, l1_w_out, l1_norm2, l1_ffn_up, l1_ffn_conv_w, l1_ffn_conv_b, l1_ffn_down, final_norm):
    assert x.shape == (1, SEQ, D_MODEL) and ctx.shape == (1, CTX_LEN, D_MODEL)
    x_lat, x_ctx = x[0], ctx[0]
    cvec = jnp.concatenate([c, c_ctx[None, :], jnp.zeros((SUBLANES_F32 - 2, D_MODEL), F32)], axis=0)
    mod0 = _mod_rows(_ada(cvec, l0_ada_w, l0_ada_b))

    c_rope, c_q, c_conv = KV_LORA, KV_LORA + QK_ROPE, KV_LORA + QK_ROPE + Q_LORA
    w_conv0 = l0_w_in[:, c_conv:].astype(BF16)
    w_lat0 = jnp.concatenate([l0_w_in[:, c_q:c_conv], l0_w_in[:, :c_rope]], axis=1).astype(BF16)
    k_rope = l0_w_in[:, c_rope:c_q]
    w_kr = jnp.concatenate([k_rope, _rot_cols(k_rope)], axis=1).astype(BF16)
    z0, kr = _inproj(x_lat, x_ctx, l0_norm1, mod0, w_conv0, w_lat0, w_kr, tn=Q_LORA + KV_LORA)

    wq = l0_w_qb.reshape(Q_LORA, MLA_HEADS, QK_DIM)
    wqn = wq[:, :, :QK_NOPE].reshape(Q_LORA, MLA_HEADS * QK_NOPE).astype(BF16)
    wqr = wq[:, :, QK_NOPE:]
    wqrr = _rot_cols(wqr).reshape(Q_LORA, MLA_HEADS * QK_ROPE).astype(BF16)
    wqr = wqr.reshape(Q_LORA, MLA_HEADS * QK_ROPE).astype(BF16)
    wkv = l0_w_kvb.reshape(KV_LORA, MLA_HEADS, QK_NOPE + V_HEAD)
    wk = wkv[:, :, :QK_NOPE].reshape(KV_LORA, MLA_HEADS * QK_NOPE).astype(BF16)
    wv = wkv[:, :, QK_NOPE:].reshape(KV_LORA, MLA_HEADS * V_HEAD).astype(BF16)
    q, k, v = _mla_prep(z0, kr, _rope_table(), l0_q_norm, wqn, wqr, wqrr, l0_kv_norm, wk, wv)
    o, up0, down0, up1, down1, w_out0, w_out1, w_kv1, w_qo1, w_lr, mods1 = _attention(
        q, k, v, l0_ffn_up, l0_ffn_down, l1_ffn_up, l1_ffn_down, l0_w_out, l1_w_out, l1_w_in,
        cvec, l1_ada_w, l1_ada_b)
    mod1 = _mod_rows(mods1)
    xa = _ab_out(x_lat, x_ctx, z0, o, l0_conv_a, mod0, w_out0)
    xa = _ffn(xa, l0_norm2, mod0, up0, l0_ffn_conv_w, l0_ffn_conv_b, down0, final_norm, final=False)

    z1, lr = _inproj(xa, None, l1_norm1, mod1, w_kv1, w_qo1, w_lr, tn=1536)
    pad_f = jnp.zeros((LANES - GATE_RANK, GLA_KEY), F32)
    gwf = jnp.concatenate([l1_gate_fw_w, pad_f], axis=0).astype(BF16)
    gwb = jnp.concatenate([jnp.zeros((GATE_RANK, GLA_KEY), F32), l1_gate_bw_w,
                           jnp.zeros((LANES - 2 * GATE_RANK, GLA_KEY), F32)], axis=0).astype(BF16)
    o_f, o_b = _gla(z1, lr, gwf, l1_gate_fw_b, gwb, l1_gate_bw_b)
    xa = _gla_out(xa, o_f, o_b, z1, l1_o_norm, mod1, w_out1)
    out = _ffn(xa, l1_norm2, mod1, up1, l1_ffn_conv_w, l1_ffn_conv_b, down1, final_norm, final=True)
    return out[None]
```

```python
import functools

import jax
import jax.numpy as jnp
import numpy as np
from jax import lax
from jax.experimental import pallas as pl
from jax.experimental.pallas import tpu as pltpu

F32 = jnp.float32
BF16 = jnp.bfloat16

D_MODEL = 2048
SEQ = 8192
GRID_W = 64
CTX_LEN = 256
T_ALL = SEQ + CTX_LEN
EPS = 1e-6
LOG2_E = 1.4426950408889634
N_MOD = 6
MLA_HEADS = 8
QK_NOPE = 128
QK_ROPE = 64
QK_DIM = QK_NOPE + QK_ROPE
V_HEAD = 128
Q_LORA = 512
KV_LORA = 256
ROPE_THETA = 10000.0
AXIS_ROPE = QK_ROPE // 2
CONV_CH = D_MODEL // 2
GLA_HEADS = 4
GLA_DK = D_MODEL // 2 // GLA_HEADS
GLA_DV = D_MODEL // GLA_HEADS
GLA_KEY = GLA_HEADS * GLA_DK
GLA_VAL = GLA_HEADS * GLA_DV
GATE_RANK = 16
GATE_NORMALIZER = 16.0
CHUNK = 64
D_FF = 5632

LANES = 128
SUBLANES_F32 = 8
SUBLANES_BF16 = 16
VMEM_LIMIT = 56 * 1024 * 1024
MOD_ROWS = 16
NORM_ROW = 12

TM = 768
TMO = 384
ROW_GROUP = 16
TQ = 256
GLA_BLOCK = 256
TF = 512


def _silu(v):
    return v / (1.0 + jnp.exp(-v))


def _row_ids(i, rows):
    return i * rows + lax.broadcasted_iota(jnp.int32, (rows, 1), 0)


def _modulated(x, norm, mod, shift_row, is_ctx):
    s = 2 * shift_row
    gain_lat = norm * (1.0 + mod[s + 2:s + 3, :])
    gain_ctx = norm * (1.0 + mod[s + 3:s + 4, :])
    y = x * lax.rsqrt(jnp.mean(x * x, axis=-1, keepdims=True) + EPS)
    return (y * jnp.where(is_ctx, gain_ctx, gain_lat)
            + jnp.where(is_ctx, mod[s + 1:s + 2, :], mod[s:s + 1, :]))


def _token_rows(x_ref, ctx_ref, is_ctx_tile, r0, n, cols=slice(None)):
    rows = x_ref[r0:r0 + n, cols]
    tile_rows = x_ref.shape[0]
    ctx_start = SEQ % tile_rows
    if ctx_ref is None or r0 + n <= ctx_start:
        return rows
    assert r0 >= ctx_start and ctx_start + CTX_LEN == tile_rows
    return jnp.where(is_ctx_tile, ctx_ref[r0 - ctx_start:r0 - ctx_start + n, cols], rows)


def _modulate_rows(x_ref, h_ref, norm, mod, shift_row, first_row, ctx_ref=None):
    tile_rows = x_ref.shape[0]
    ctx_start = SEQ % tile_rows
    assert ctx_start + CTX_LEN == tile_rows and ctx_start % ROW_GROUP == 0
    is_ctx_tile = first_row == (SEQ // tile_rows) * tile_rows
    s = 2 * shift_row
    gain_lat = norm * (1.0 + mod[s + 2:s + 3, :])
    gain_tail = jnp.where(is_ctx_tile, norm * (1.0 + mod[s + 3:s + 4, :]), gain_lat)
    shift_lat = mod[s:s + 1, :]
    shift_tail = jnp.where(is_ctx_tile, mod[s + 1:s + 2, :], shift_lat)
    for r0 in range(0, tile_rows, ROW_GROUP):
        x = _token_rows(x_ref, ctx_ref, is_ctx_tile, r0, ROW_GROUP)
        y = x * lax.rsqrt(jnp.mean(x * x, axis=-1, keepdims=True) + EPS)
        gain, shift = (gain_tail, shift_tail) if r0 >= ctx_start else (gain_lat, shift_lat)
        h_ref[r0:r0 + ROW_GROUP, :] = (y * gain + shift).astype(BF16)


def _ada_kernel(a_ref, w_ref, b_ref, o_ref):
    a = _silu(a_ref[...]).astype(BF16)
    o_ref[...] = jnp.dot(a, w_ref[...].astype(BF16), preferred_element_type=F32) + b_ref[...]


def _ada(cvec, ada_w, ada_b):
    n = ada_w.shape[1]
    tn = 1024
    return pl.pallas_call(
        _ada_kernel,
        grid=(n // tn,),
        in_specs=[
            pl.BlockSpec((SUBLANES_F32, D_MODEL), lambda j: (0, 0)),
            pl.BlockSpec((D_MODEL, tn), lambda j: (0, j)),
            pl.BlockSpec((1, tn), lambda j: (0, j)),
        ],
        out_specs=pl.BlockSpec((SUBLANES_F32, tn), lambda j: (0, j)),
        out_shape=jax.ShapeDtypeStruct((SUBLANES_F32, n), F32),
        compiler_params=pltpu.CompilerParams(
            dimension_semantics=("arbitrary",), vmem_limit_bytes=VMEM_LIMIT),
        name="ada_mod",
    )(cvec, ada_w, ada_b.reshape(1, n))


def _mod_rows(mods, norm1, norm2, final_norm):
    m = mods[:2].reshape(2, N_MOD, D_MODEL)
    m = jnp.transpose(m, (1, 0, 2)).reshape(2 * N_MOD, D_MODEL)
    gains = jnp.stack([norm1, norm2, final_norm], axis=0)
    pad = jnp.zeros((MOD_ROWS - NORM_ROW - gains.shape[0], D_MODEL), F32)
    return jnp.concatenate([m, gains, pad], axis=0)


def _inproj_kernel(x_ref, ctx_ref, mod_ref, wa_ref, wb_ref, we_ref, z_ref, *rest, na, split_tokens):
    zb_ref = rest[0] if len(rest) == 3 else z_ref
    ze_ref, h_ref = rest[-2:]
    i = pl.program_id(0)
    j = pl.program_id(1)

    @pl.when(j == 0)
    def _():
        mod = mod_ref[...]
        _modulate_rows(x_ref, h_ref, mod[NORM_ROW:NORM_ROW + 1, :], mod, 0, i * TM,
                       ctx_ref if split_tokens else None)
        ze_ref[...] = jnp.dot(h_ref[...], we_ref[...], preferred_element_type=F32)
        part = TM // 3
        for r0 in range(0, TM, part):
            z_ref[r0:r0 + part, :] = jnp.dot(h_ref[r0:r0 + part, :], wa_ref[...],
                                             preferred_element_type=F32).astype(z_ref.dtype)

    @pl.when((j > 0) & (j < na))
    def _():
        z_ref[...] = jnp.dot(h_ref[...], wa_ref[...], preferred_element_type=F32).astype(z_ref.dtype)

    @pl.when(j >= na)
    def _():
        zb_ref[...] = jnp.dot(h_ref[...], wb_ref[...], preferred_element_type=F32).astype(zb_ref.dtype)


def _inproj(tokens, ctx, mod, w_a, w_b, w_extra, tn, tn_b=None):
    split_out = tn_b is not None
    tn_b = tn_b if split_out else tn
    na = w_a.shape[1] // tn
    nb = w_b.shape[1] // tn_b
    n = (na + nb) * tn
    split_tokens = ctx is not None
    if not split_tokens:
        ctx = tokens
    wb_tile = lambda j: jnp.where(j == 0, nb - 1, jnp.maximum(j - na, 0))
    return pl.pallas_call(
        functools.partial(_inproj_kernel, na=na, split_tokens=split_tokens),
        grid=(T_ALL // TM, na + nb),
        in_specs=[
            pl.BlockSpec((TM, D_MODEL), lambda i, j: (i, 0)),
            pl.BlockSpec((CTX_LEN, D_MODEL), lambda i, j: (0, 0)),
            pl.BlockSpec((MOD_ROWS, D_MODEL), lambda i, j: (0, 0)),
            pl.BlockSpec((D_MODEL, tn), lambda i, j: (0, jnp.minimum(j, na - 1))),
            pl.BlockSpec((D_MODEL, tn_b), lambda i, j: (0, wb_tile(j))),
            pl.BlockSpec((D_MODEL, LANES), lambda i, j: (0, 0)),
        ],
        out_specs=(
            [pl.BlockSpec((TM, tn), lambda i, j: (i, jnp.minimum(j, na - 1))),
             pl.BlockSpec((TM, tn_b), lambda i, j: (i, jnp.maximum(j - na, 0)))]
            if split_out else [pl.BlockSpec((TM, tn), lambda i, j: (i, j))]
        ) + [pl.BlockSpec((TM, LANES), lambda i, j: (i, 0))],
        out_shape=(
            [jax.ShapeDtypeStruct((T_ALL, na * tn), BF16), jax.ShapeDtypeStruct((T_ALL, nb * tn_b), BF16)]
            if split_out else [jax.ShapeDtypeStruct((T_ALL, n), BF16)]
        ) + [jax.ShapeDtypeStruct((T_ALL, LANES), F32)],
        scratch_shapes=[pltpu.VMEM((TM, D_MODEL), BF16)],
        compiler_params=pltpu.CompilerParams(
            dimension_semantics=("parallel", "arbitrary"), vmem_limit_bytes=VMEM_LIMIT),
        name="in_proj",
    )(tokens, ctx, mod, w_a, w_b, w_extra)


def _mla_prep_kernel(ql_ref, kvl_ref, kr_ref, cs_ref, qn_ref, wqn_ref, wqr_ref, wqrr_ref,
                     kvn_ref, wk_ref, wv_ref, q_ref, k_ref, v_ref):
    cos = cs_ref[:, :QK_ROPE]
    sin = cs_ref[:, QK_ROPE:]

    ql = ql_ref[...].astype(F32)
    qn = (ql * lax.rsqrt(jnp.mean(ql * ql, axis=-1, keepdims=True) + EPS) * qn_ref[...]).astype(BF16)
    scale = QK_DIM ** -0.5 * LOG2_E
    q_nope = jnp.dot(qn, wqn_ref[...], preferred_element_type=F32) * scale
    q_r = jnp.dot(qn, wqr_ref[...], preferred_element_type=F32)
    q_rr = jnp.dot(qn, wqrr_ref[...], preferred_element_type=F32)

    kvl = kvl_ref[...].astype(F32)
    kvn = (kvl * lax.rsqrt(jnp.mean(kvl * kvl, axis=-1, keepdims=True) + EPS) * kvn_ref[...]).astype(BF16)
    k_nope = jnp.dot(kvn, wk_ref[...], preferred_element_type=F32)
    v = jnp.dot(kvn, wv_ref[...], preferred_element_type=F32)
    kr = kr_ref[...]
    k_rope = (kr[:, :QK_ROPE] * cos + kr[:, QK_ROPE:] * sin).astype(BF16)

    for h in range(MLA_HEADS):
        q_ref[h, :, :QK_NOPE] = q_nope[:, h * QK_NOPE:(h + 1) * QK_NOPE].astype(BF16)
        rope = q_r[:, h * QK_ROPE:(h + 1) * QK_ROPE] * cos + q_rr[:, h * QK_ROPE:(h + 1) * QK_ROPE] * sin
        q_ref[h, :, QK_NOPE:] = (rope * scale).astype(BF16)
        k_ref[h, :, :QK_NOPE] = k_nope[:, h * QK_NOPE:(h + 1) * QK_NOPE].astype(BF16)
        k_ref[h, :, QK_NOPE:] = k_rope
        v_ref[h, :, :V_HEAD] = v[:, h * V_HEAD:(h + 1) * V_HEAD].astype(BF16)
        v_ref[h, :, V_HEAD:] = jnp.ones((v.shape[0], V_HEAD), BF16)


def _mla_prep(z, kr, cs, q_norm, wqn, wqr, wqrr, kv_norm, wk, wv):
    tm = TM
    const = lambda shape: pl.BlockSpec(shape, lambda i: (0,) * len(shape))
    return pl.pallas_call(
        _mla_prep_kernel,
        grid=(T_ALL // tm,),
        in_specs=[
            pl.BlockSpec((tm, Q_LORA), lambda i: (i, 0)),
            pl.BlockSpec((tm, KV_LORA), lambda i: (i, Q_LORA // KV_LORA)),
            pl.BlockSpec((tm, LANES), lambda i: (i, 0)),
            pl.BlockSpec((tm, LANES), lambda i: (i, 0)),
            const((1, Q_LORA)),
            const((Q_LORA, MLA_HEADS * QK_NOPE)),
            const((Q_LORA, MLA_HEADS * QK_ROPE)),
            const((Q_LORA, MLA_HEADS * QK_ROPE)),
            const((1, KV_LORA)),
            const((KV_LORA, MLA_HEADS * QK_NOPE)),
            const((KV_LORA, MLA_HEADS * V_HEAD)),
        ],
        out_specs=[
            pl.BlockSpec((MLA_HEADS, tm, QK_DIM), lambda i: (0, i, 0)),
            pl.BlockSpec((MLA_HEADS, tm, QK_DIM), lambda i: (0, i, 0)),
            pl.BlockSpec((MLA_HEADS, tm, 2 * V_HEAD), lambda i: (0, i, 0)),
        ],
        out_shape=[
            jax.ShapeDtypeStruct((MLA_HEADS, T_ALL, QK_DIM), BF16),
            jax.ShapeDtypeStruct((MLA_HEADS, T_ALL, QK_DIM), BF16),
            jax.ShapeDtypeStruct((MLA_HEADS, T_ALL, 2 * V_HEAD), BF16),
        ],
        compiler_params=pltpu.CompilerParams(
            dimension_semantics=("parallel",), vmem_limit_bytes=VMEM_LIMIT),
        name="mla_prep",
    )(z, z, kr, cs, q_norm.reshape(1, Q_LORA), wqn, wqr, wqrr, kv_norm.reshape(1, KV_LORA), wk, wv)


def _attn_kernel(qa_ref, qn_ref, k_ref, v_ref, up0_ref, dn0_ref, up1_ref, dn1_ref, wo0_ref, wo1_ref,
                 win1_ref, cvec_ref, adaw_ref, adab_ref,
                 o_ref, up0_o, dn0_o, up1_o, dn1_o, wo0_o, wo1_o, wkv_o, wqo_o, wlr_o, mods_o,
                 sa_ref, sb_ref):
    j = pl.program_id(1)
    n_pairs = SEQ // (2 * TQ)

    def convert_weight_slabs():
        for src, dst in ((up0_ref, up0_o), (dn0_ref, dn0_o), (up1_ref, up1_o), (dn1_ref, dn1_o),
                         (wo0_ref, wo0_o), (wo1_ref, wo1_o)):
            dst[...] = src[...].astype(BF16)
        _ada_kernel(cvec_ref, adaw_ref, adab_ref, mods_o)
        w1 = win1_ref[...]
        kv_cols = GLA_KEY + GLA_VAL
        wkv_o[...] = w1[:, :kv_cols].astype(BF16)
        wqo_o[...] = w1[:, kv_cols + 2 * GATE_RANK:].astype(BF16)
        wlr_o[...] = jnp.concatenate(
            [w1[:, kv_cols:kv_cols + 2 * GATE_RANK],
             jnp.zeros((w1.shape[0], LANES - 2 * GATE_RANK), F32)], axis=1).astype(BF16)

    def score(q, s_ref):
        s_ref[...] = lax.dot_general(q, k_ref[0], (((1,), (1,)), ((), ())), preferred_element_type=F32)

    def attend(s, v, rows):
        p = jnp.exp2(s - jnp.max(s, axis=-1, keepdims=True)).astype(BF16)
        acc = jnp.dot(p, v, preferred_element_type=F32)
        o_ref[rows, :] = (acc[:, :V_HEAD] / acc[:, V_HEAD:V_HEAD + 1]).astype(o_ref.dtype)

    @pl.when(j == 0)
    def _():
        score(qa_ref[0, :TQ, :], sa_ref)

    @pl.when(j < n_pairs)
    def _():
        score(qa_ref[0, TQ:, :], sb_ref)
        attend(sa_ref[...], v_ref[0], slice(0, TQ))
        score(qn_ref[0], sa_ref)
        attend(sb_ref[...], v_ref[0], slice(TQ, 2 * TQ))
        convert_weight_slabs()

    @pl.when(j == n_pairs)
    def _():
        convert_weight_slabs()
        attend(sa_ref[:, SEQ:], v_ref[0, SEQ:, :], slice(0, TQ))
        o_ref[TQ:, :] = jnp.zeros((TQ, V_HEAD), o_ref.dtype)


def _attention(q, k, v, ffn_up0, ffn_down0, ffn_up1, ffn_down1, w_out0, w_out1, w_in1,
               cvec, ada_w1, ada_b1):
    n_pairs = SEQ // (2 * TQ)
    n_j = n_pairs + 1
    n_mod = ada_w1.shape[1]
    ada_col = lambda h, j: (0, jnp.minimum(h * n_j + j, n_mod // LANES - 1))
    assert MLA_HEADS * n_j >= n_mod // LANES

    def slab(n_rows, rows, cols_in, cols_out=None):
        assert n_rows % rows == 0 and MLA_HEADS * n_j >= n_rows // rows
        index = lambda h, j: (jnp.minimum(h * n_j + j, n_rows // rows - 1), 0)
        return (pl.BlockSpec((rows, cols_in), index),
                pl.BlockSpec((rows, cols_in if cols_out is None else cols_out), index))

    up_in, up_out = slab(D_MODEL, SUBLANES_BF16, 2 * D_FF)
    dn_in, dn_out = slab(D_FF, 4 * SUBLANES_BF16, D_MODEL)
    wo_in, wo_out = slab(D_MODEL, SUBLANES_BF16, D_MODEL)
    n_in1 = w_in1.shape[1]
    kv_cols = GLA_KEY + GLA_VAL
    w1_in, w1_kv = slab(D_MODEL, SUBLANES_BF16, n_in1, kv_cols)
    _, w1_qo = slab(D_MODEL, SUBLANES_BF16, n_in1, n_in1 - kv_cols - 2 * GATE_RANK)
    _, w1_lr = slab(D_MODEL, SUBLANES_BF16, n_in1, LANES)
    bf = lambda shape: jax.ShapeDtypeStruct(shape, BF16)
    return pl.pallas_call(
        _attn_kernel,
        grid=(MLA_HEADS, n_j),
        in_specs=[
            pl.BlockSpec((1, 2 * TQ, QK_DIM), lambda h, j: (h, jnp.minimum(j, n_pairs - 1), 0)),
            pl.BlockSpec((1, TQ, QK_DIM), lambda h, j: (h, jnp.minimum(2 * j + 2, SEQ // TQ), 0)),
            pl.BlockSpec((1, T_ALL, QK_DIM), lambda h, j: (h, 0, 0)),
            pl.BlockSpec((1, T_ALL, 2 * V_HEAD), lambda h, j: (h, 0, 0)),
            up_in, dn_in, up_in, dn_in, wo_in, wo_in, w1_in,
            pl.BlockSpec((SUBLANES_F32, D_MODEL), lambda h, j: (0, 0)),
            pl.BlockSpec((D_MODEL, LANES), ada_col),
            pl.BlockSpec((1, LANES), ada_col),
        ],
        out_specs=[
            pl.BlockSpec((2 * TQ, V_HEAD), lambda h, j: (j, h)),
            up_out, dn_out, up_out, dn_out, wo_out, wo_out, w1_kv, w1_qo, w1_lr,
            pl.BlockSpec((SUBLANES_F32, LANES), ada_col),
        ],
        out_shape=[
            bf((n_j * 2 * TQ, MLA_HEADS * V_HEAD)),
            bf(ffn_up0.shape), bf(ffn_down0.shape), bf(ffn_up1.shape), bf(ffn_down1.shape),
            bf(w_out0.shape), bf(w_out1.shape),
            bf((D_MODEL, kv_cols)), bf((D_MODEL, n_in1 - kv_cols - 2 * GATE_RANK)), bf((D_MODEL, LANES)),
            jax.ShapeDtypeStruct((SUBLANES_F32, n_mod), F32),
        ],
        scratch_shapes=[pltpu.VMEM((TQ, T_ALL), F32), pltpu.VMEM((TQ, T_ALL), F32)],
        compiler_params=pltpu.CompilerParams(
            dimension_semantics=("arbitrary", "arbitrary"), vmem_limit_bytes=VMEM_LIMIT),
        name="mla_attn",
    )(q, q, k, v, ffn_up0, ffn_down0, ffn_up1, ffn_down1, w_out0, w_out1, w_in1,
      cvec, ada_w1, ada_b1.reshape(1, n_mod))


def _neighbour_masks(rows):
    has_prev = (rows != 0) & (rows != SEQ)
    has_next = (rows != SEQ - 1) & (rows != T_ALL - 1)
    return has_prev, has_next


def _ab_out_kernel(x_ref, ctx_ref, ax_ref, ab_ref, ac_ref, axp_ref, acp_ref, axn_ref, acn_ref, o_ref,
                   cw_ref, mod_ref, w_ref, out_ref, h_ref, p_ref):
    i = pl.program_id(0)
    tm = x_ref.shape[0]
    halo = SUBLANES_F32
    hb = SUBLANES_BF16
    f32 = lambda ref, r0, n: ref[r0:r0 + n, :].astype(F32)

    p_ref[halo - 1:halo, :] = f32(acp_ref, hb - 1, 1) * f32(axp_ref, hb - 1, 1)
    p_ref[halo + tm:halo + tm + 1, :] = f32(acn_ref, 0, 1) * f32(axn_ref, 0, 1)
    for r0 in range(0, tm, ROW_GROUP):
        p_ref[halo + r0:halo + r0 + ROW_GROUP, :] = f32(ac_ref, r0, ROW_GROUP) * f32(ax_ref, r0, ROW_GROUP)

    cw = cw_ref[...]
    mod = mod_ref[...]
    gate = jnp.where(_row_ids(i, tm) >= SEQ, mod[5:6, :], mod[4:5, :])
    ctx_start = SEQ % tm
    is_ctx_tile = i == SEQ // tm

    def residual(cols):
        return jnp.concatenate(
            [_token_rows(x_ref, ctx_ref, is_ctx_tile, 0, ctx_start, cols),
             _token_rows(x_ref, ctx_ref, is_ctx_tile, ctx_start, tm - ctx_start, cols)], axis=0)

    for r0 in range(0, tm, ROW_GROUP):
        rows = i * tm + r0 + lax.broadcasted_iota(jnp.int32, (ROW_GROUP, 1), 0)
        has_prev, has_next = _neighbour_masks(rows)
        at = lambda shift: p_ref[halo + r0 + shift:halo + r0 + shift + ROW_GROUP, :]
        conv = (jnp.where(has_prev, at(-1), 0.0) * cw[0:1, :] + at(0) * cw[1:2, :]
                + jnp.where(has_next, at(1), 0.0) * cw[2:3, :])
        h_ref[r0:r0 + ROW_GROUP, :CONV_CH] = (f32(ab_ref, r0, ROW_GROUP) * conv).astype(BF16)
    h_ref[:, CONV_CH:] = o_ref[...]
    y = jnp.dot(h_ref[...], w_ref[...], preferred_element_type=F32)
    out_ref[...] = residual(slice(None)) + gate * y


def _ab_out(x_lat, ctx, z, o, conv_a, mod, w_out):
    tm = TMO
    hb = SUBLANES_BF16
    nb = tm // hb
    last = T_ALL // hb - 1
    prev_map = lambda c: (lambda i: (jnp.maximum(i * nb - 1, 0), c))
    next_map = lambda c: (lambda i: (jnp.minimum((i + 1) * nb, last), c))
    return pl.pallas_call(
        _ab_out_kernel,
        grid=(T_ALL // tm,),
        in_specs=[
            pl.BlockSpec((tm, D_MODEL), lambda i: (i, 0)),
            pl.BlockSpec((CTX_LEN, D_MODEL), lambda i: (0, 0)),
            pl.BlockSpec((tm, CONV_CH), lambda i: (i, 0)),
            pl.BlockSpec((tm, CONV_CH), lambda i: (i, 1)),
            pl.BlockSpec((tm, CONV_CH), lambda i: (i, 2)),
            pl.BlockSpec((hb, CONV_CH), prev_map(0)),
            pl.BlockSpec((hb, CONV_CH), prev_map(2)),
            pl.BlockSpec((hb, CONV_CH), next_map(0)),
            pl.BlockSpec((hb, CONV_CH), next_map(2)),
            pl.BlockSpec((tm, MLA_HEADS * V_HEAD), lambda i: (i, 0)),
            pl.BlockSpec((3, CONV_CH), lambda i: (0, 0)),
            pl.BlockSpec((MOD_ROWS, D_MODEL), lambda i: (0, 0)),
            pl.BlockSpec((D_MODEL, D_MODEL), lambda i: (0, 0), pipeline_mode=pl.Buffered(1)),
        ],
        out_specs=pl.BlockSpec((tm, D_MODEL), lambda i: (i, 0)),
        out_shape=jax.ShapeDtypeStruct((T_ALL, D_MODEL), F32),
        scratch_shapes=[
            pltpu.VMEM((tm, D_MODEL), BF16),
            pltpu.VMEM((tm + 2 * SUBLANES_F32, CONV_CH), F32),
        ],
        compiler_params=pltpu.CompilerParams(
            dimension_semantics=("parallel",), vmem_limit_bytes=VMEM_LIMIT),
        name="ab_out",
    )(x_lat, ctx, z, z, z, z, z, z, z, o, conv_a, mod, w_out)


def _ffn_kernel(x_ref, xp_ref, xn_ref, mod_ref, wg_ref, wv_ref, cwg_ref, cwv_ref, wd_ref,
                out_ref, h_ref, acta_ref, actb_ref, *, final):
    i = pl.program_id(0)
    j = pl.program_id(1)
    nf = pl.num_programs(1) - 1
    halo = SUBLANES_F32
    rows = _row_ids(i, TM)

    def prologue():
        mod = mod_ref[...]
        norm = mod[NORM_ROW + 1:NORM_ROW + 2, :]
        _modulate_rows(x_ref, h_ref, norm, mod, 3, i * TM)
        xh = jnp.concatenate([xp_ref[...], xn_ref[...]], axis=0)
        off = lax.broadcasted_iota(jnp.int32, (2 * halo, 1), 0)
        rows_h = jnp.where(off < halo, i * TM - halo + off, (i + 1) * TM - halo + off)
        h_ref[TM:, :] = _modulated(xh, norm, mod, 3, rows_h >= SEQ).astype(BF16)
        out_ref[...] = jnp.zeros_like(out_ref)

    def conv(w_ref, cw_ref):
        has_prev, has_next = _neighbour_masks(rows)
        local = lax.broadcasted_iota(jnp.int32, (TM, 1), 0)
        w = w_ref[...]
        part = TM // 3
        u_a = jnp.dot(h_ref[:part, :], w, preferred_element_type=F32)
        u_b = jnp.dot(h_ref[part:2 * part, :], w, preferred_element_type=F32)
        u_c = jnp.dot(h_ref[2 * part:, :], w, preferred_element_type=F32)
        u = jnp.concatenate([u_a, u_b, u_c[:part, :]], axis=0)
        uh = u_c[part:, :]
        cw = cw_ref[...]
        u_prev = jnp.where(local == 0, uh[halo - 1:halo, :], pltpu.roll(u, 1, axis=0))
        u_next = jnp.where(local == TM - 1, uh[halo:halo + 1, :], pltpu.roll(u, TM - 1, axis=0))
        return (jnp.where(has_prev, u_prev, 0.0) * cw[0:1, :] + u * cw[1:2, :]
                + jnp.where(has_next, u_next, 0.0) * cw[2:3, :] + cw[3:4, :])

    def up(act_ref):
        act_ref[...] = (_silu(conv(wg_ref, cwg_ref)) * conv(wv_ref, cwv_ref)).astype(BF16)

    def down(act_ref):
        out_ref[...] += jnp.dot(act_ref[...], wd_ref[...], preferred_element_type=F32)

    even = lax.rem(j, 2) == 0
    inner = (j > 0) & (j < nf)

    @pl.when(j == 0)
    def _():
        prologue()
        up(acta_ref)

    @pl.when(inner & even)
    def _():
        up(acta_ref)
        down(actb_ref)

    @pl.when(inner & jnp.logical_not(even))
    def _():
        up(actb_ref)
        down(acta_ref)

    @pl.when(j == nf)
    def _():
        down(acta_ref if (D_FF // TF - 1) % 2 == 0 else actb_ref)
        mod = mod_ref[...]
        gate = jnp.where(rows >= SEQ, mod[11:12, :], mod[10:11, :])
        y = x_ref[...] + gate * out_ref[...]
        if final:
            y = y * lax.rsqrt(jnp.mean(y * y, axis=-1, keepdims=True) + EPS) * mod[NORM_ROW + 2:NORM_ROW + 3, :]
        out_ref[...] = y


def _ffn(xa, mod, w_up, conv_w, conv_b, w_down, final):
    nf = D_FF // TF
    hb = SUBLANES_F32
    nb = TM // hb
    last = T_ALL // hb - 1
    out_rows = SEQ if final else T_ALL
    cwb = jnp.concatenate([conv_w, conv_b[None, :],
                           jnp.zeros((SUBLANES_F32 - 4, 2 * D_FF), F32)], axis=0)
    up_tile = lambda j: jnp.minimum(j, nf - 1)
    return pl.pallas_call(
        functools.partial(_ffn_kernel, final=final),
        grid=(T_ALL // TM, nf + 1),
        in_specs=[
            pl.BlockSpec((TM, D_MODEL), lambda i, j: (i, 0)),
            pl.BlockSpec((hb, D_MODEL), lambda i, j: (jnp.maximum(i * nb - 1, 0), 0)),
            pl.BlockSpec((hb, D_MODEL), lambda i, j: (jnp.minimum((i + 1) * nb, last), 0)),
            pl.BlockSpec((MOD_ROWS, D_MODEL), lambda i, j: (0, 0)),
            pl.BlockSpec((D_MODEL, TF), lambda i, j: (0, up_tile(j))),
            pl.BlockSpec((D_MODEL, TF), lambda i, j: (0, up_tile(j) + nf)),
            pl.BlockSpec((SUBLANES_F32, TF), lambda i, j: (0, up_tile(j))),
            pl.BlockSpec((SUBLANES_F32, TF), lambda i, j: (0, up_tile(j) + nf)),
            pl.BlockSpec((TF, D_MODEL), lambda i, j: (jnp.maximum(j - 1, 0), 0)),
        ],
        out_specs=pl.BlockSpec((TM, D_MODEL), lambda i, j: (i, 0)),
        out_shape=jax.ShapeDtypeStruct((out_rows, D_MODEL), F32),
        scratch_shapes=[
            pltpu.VMEM((TM + 2 * hb, D_MODEL), BF16),
            pltpu.VMEM((TM, TF), BF16),
            pltpu.VMEM((TM, TF), BF16),
        ],
        compiler_params=pltpu.CompilerParams(
            dimension_semantics=("parallel", "arbitrary"), vmem_limit_bytes=VMEM_LIMIT),
        name="conv_ffn_final" if final else "conv_ffn",
    )(xa, xa, xa, mod, w_up, w_up, cwb, cwb, w_down)


def _per_chunk(rows):
    return jnp.concatenate([jnp.broadcast_to(t, (CHUNK, t.shape[-1])) for t in rows], axis=0)


def _gla_direction(q_ref, k_ref, v_refs, lr_ref, gw_ref, gb_ref, o_ref, st_ref, causal):
    n = GLA_BLOCK
    nc = n // CHUNK
    nt = (((1,), (1,)), ((), ()))
    r = lax.broadcasted_iota(jnp.int32, (n, n), 0)
    c = lax.broadcasted_iota(jnp.int32, (n, n), 1)
    shift = CHUNK.bit_length() - 1
    rc = jnp.right_shift(r, shift)
    cc = jnp.right_shift(c, shift)
    tri = (c <= r) if causal else (c >= r)
    delta = (rc - cc) if causal else (cc - rc)

    pre = jnp.dot(lr_ref[...].astype(BF16), gw_ref[...], preferred_element_type=F32) + gb_ref[...]
    g = (jnp.minimum(pre, 0.0) - jnp.log(1.0 + jnp.exp(-jnp.abs(pre)))) / GATE_NORMALIZER
    tri_b = jnp.where(tri, 1.0, 0.0).astype(BF16)
    g_hi = g.astype(BF16)
    g_lo = (g - g_hi.astype(F32)).astype(BF16)
    bb = (jnp.dot(tri_b, g_hi, preferred_element_type=F32)
          + jnp.dot(tri_b, g_lo, preferred_element_type=F32))

    zero = jnp.zeros((1, GLA_KEY), F32)
    if causal:
        edge = [zero] + [bb[ci * CHUNK - 1:ci * CHUNK, :] for ci in range(1, nc)]
        tot = [bb[ci * CHUNK + CHUNK - 1:ci * CHUNK + CHUNK, :] - edge[ci] for ci in range(nc)]
        ball = bb[n - 1:n, :]
        near = [zero] + tot[:nc - 1]
        far = [zero, zero] + tot[:nc - 2]
    else:
        edge = [bb[(ci + 1) * CHUNK:(ci + 1) * CHUNK + 1, :] for ci in range(nc - 1)] + [zero]
        tot = [bb[ci * CHUNK:ci * CHUNK + 1, :] - edge[ci] for ci in range(nc)]
        ball = bb[0:1, :]
        near = tot[1:] + [zero]
        far = tot[2:] + [zero, zero]
    b = bb - _per_chunk(edge)
    skip1 = _per_chunk([jnp.exp(t) for t in near])
    skip2 = _per_chunk([jnp.exp(t1 + t2) for t1, t2 in zip(near, far)])
    to_chunk_end = _per_chunk([jnp.exp(t) for t in tot])
    from_block_edge = _per_chunk([jnp.exp(e) for e in edge])
    to_block_end = _per_chunk([jnp.exp(ball - e - t) for e, t in zip(edge, tot)])
    dec_all = jnp.exp(ball)

    q = q_ref[...].astype(F32) * (GLA_DK ** -0.5)
    k = k_ref[...].astype(F32)
    qe_f = q * jnp.exp(b)
    ke_f = k * jnp.exp(-b)
    kd_f = ke_f * to_chunk_end
    qe = qe_f.astype(BF16)
    ke = ke_f.astype(BF16)
    kd = kd_f.astype(BF16)
    q1 = (qe_f * skip1).astype(BF16)
    q2 = (qe_f * skip2).astype(BF16)
    q_in = (qe_f * from_block_edge).astype(BF16)
    k_out = (kd_f * to_block_end).astype(BF16)

    for hd in range(GLA_HEADS):
        ks = slice(hd * GLA_DK, (hd + 1) * GLA_DK)
        v_half = v_refs[hd // (GLA_HEADS // 2)]
        v_col = (hd % (GLA_HEADS // 2)) * GLA_DV
        v = v_half[:, v_col:v_col + GLA_DV]
        q_far = jnp.concatenate([qe[:, ks], q1[:, ks], q2[:, ks]], axis=0)
        a0 = lax.dot_general(qe[:, ks], ke[:, ks], nt, preferred_element_type=F32)
        a123 = lax.dot_general(q_far, kd[:, ks], nt, preferred_element_type=F32)
        att = jnp.where((delta == 0) & tri, a0, 0.0)
        for d in range(1, nc):
            att = jnp.where(delta == d, a123[(d - 1) * n:d * n, :], att)
        st = st_ref[hd]
        o_ref[:, hd * GLA_DV:(hd + 1) * GLA_DV] = (
            jnp.dot(att.astype(BF16), v, preferred_element_type=F32)
            + lax.dot_general(q_in[:, ks], st.astype(BF16), nt, preferred_element_type=F32)
        ).astype(o_ref.dtype)
        upd = lax.dot_general(v, k_out[:, ks], (((0,), (0,)), ((), ())), preferred_element_type=F32)
        st_ref[hd] = st * dec_all[:, ks] + upd


def _gla_kernel(qf_ref, kf_ref, vf0_ref, vf1_ref, lrf_ref, qb_ref, kb_ref, vb0_ref, vb1_ref, lrb_ref,
                gwf_ref, gbf_ref, gwb_ref, gbb_ref, of_ref, ob_ref, sf_ref, sb_ref):
    @pl.when(pl.program_id(0) == 0)
    def _():
        sf_ref[...] = jnp.zeros_like(sf_ref)
        sb_ref[...] = jnp.zeros_like(sb_ref)

    _gla_direction(qf_ref, kf_ref, (vf0_ref, vf1_ref), lrf_ref, gwf_ref, gbf_ref, of_ref, sf_ref, True)
    _gla_direction(qb_ref, kb_ref, (vb0_ref, vb1_ref), lrb_ref, gwb_ref, gbb_ref, ob_ref, sb_ref, False)


def _gla(z, lr, gwf, gbf, gwb, gbb):
    nblk = T_ALL // GLA_BLOCK
    ctx_blk = SEQ // GLA_BLOCK
    fwd = lambda s: jnp.where(s == 0, ctx_blk, s - 1)
    bwd = lambda s: jnp.where(s == 0, ctx_blk, ctx_blk - s)

    def specs(order):
        return [
            pl.BlockSpec((GLA_BLOCK, GLA_KEY), lambda s: (order(s), 3)),
            pl.BlockSpec((GLA_BLOCK, GLA_KEY), lambda s: (order(s), 0)),
            pl.BlockSpec((GLA_BLOCK, GLA_KEY), lambda s: (order(s), 1)),
            pl.BlockSpec((GLA_BLOCK, GLA_KEY), lambda s: (order(s), 2)),
            pl.BlockSpec((GLA_BLOCK, LANES), lambda s: (order(s), 0)),
        ]

    gate_specs = [
        pl.BlockSpec((LANES, GLA_KEY), lambda s: (0, 0)),
        pl.BlockSpec((1, GLA_KEY), lambda s: (0, 0)),
    ]
    return pl.pallas_call(
        _gla_kernel,
        grid=(nblk,),
        in_specs=specs(fwd) + specs(bwd) + gate_specs + gate_specs,
        out_specs=[
            pl.BlockSpec((GLA_BLOCK, GLA_VAL), lambda s: (fwd(s), 0)),
            pl.BlockSpec((GLA_BLOCK, GLA_VAL), lambda s: (bwd(s), 0)),
        ],
        out_shape=[
            jax.ShapeDtypeStruct((T_ALL, GLA_VAL), BF16),
            jax.ShapeDtypeStruct((T_ALL, GLA_VAL), BF16),
        ],
        scratch_shapes=[
            pltpu.VMEM((GLA_HEADS, GLA_DV, GLA_DK), F32),
            pltpu.VMEM((GLA_HEADS, GLA_DV, GLA_DK), F32),
        ],
        compiler_params=pltpu.CompilerParams(
            dimension_semantics=("arbitrary",), vmem_limit_bytes=VMEM_LIMIT),
        name="gla_scan",
    )(z, z, z, z, lr, z, z, z, z, lr, gwf, gbf.reshape(1, GLA_KEY), gwb, gbb.reshape(1, GLA_KEY))


def _gla_out_kernel(x_ref, of_ref, ob_ref, og_ref, on_ref, mod_ref, w_ref, out_ref, h_ref):
    i = pl.program_id(0)
    tm = x_ref.shape[0]
    on = on_ref[...]
    mod = mod_ref[...]
    gate = jnp.where(_row_ids(i, tm) >= SEQ, mod[5:6, :], mod[4:5, :])
    for r0 in range(0, tm, ROW_GROUP):
        rs = slice(r0, r0 + ROW_GROUP)
        for hd in range(GLA_HEADS):
            sl = slice(hd * GLA_DV, (hd + 1) * GLA_DV)
            oh = of_ref[rs, sl].astype(F32) + ob_ref[rs, sl].astype(F32)
            y = oh * lax.rsqrt(jnp.mean(oh * oh, axis=-1, keepdims=True) + EPS) * on
            h_ref[rs, sl] = (y * _silu(og_ref[rs, sl].astype(F32))).astype(BF16)
    y = jnp.dot(h_ref[...], w_ref[...], preferred_element_type=F32)
    out_ref[...] = x_ref[...] + gate * y


def _gla_out(xa, o_f, o_b, z, o_norm, mod, w_out):
    tm = TMO
    og_blk = 2 * GLA_KEY // GLA_VAL + 1
    return pl.pallas_call(
        _gla_out_kernel,
        grid=(T_ALL // tm,),
        in_specs=[
            pl.BlockSpec((tm, D_MODEL), lambda i: (i, 0)),
            pl.BlockSpec((tm, GLA_VAL), lambda i: (i, 0)),
            pl.BlockSpec((tm, GLA_VAL), lambda i: (i, 0)),
            pl.BlockSpec((tm, GLA_VAL), lambda i: (i, og_blk)),
            pl.BlockSpec((1, GLA_DV), lambda i: (0, 0)),
            pl.BlockSpec((MOD_ROWS, D_MODEL), lambda i: (0, 0)),
            pl.BlockSpec((GLA_VAL, D_MODEL), lambda i: (0, 0), pipeline_mode=pl.Buffered(1)),
        ],
        out_specs=pl.BlockSpec((tm, D_MODEL), lambda i: (i, 0)),
        out_shape=jax.ShapeDtypeStruct((T_ALL, D_MODEL), F32),
        scratch_shapes=[pltpu.VMEM((tm, GLA_VAL), BF16)],
        compiler_params=pltpu.CompilerParams(
            dimension_semantics=("parallel",), vmem_limit_bytes=VMEM_LIMIT),
        name="gla_out",
    )(xa, o_f, o_b, z, o_norm.reshape(1, GLA_DV), mod, w_out)


def _rot_cols(w):
    a1, a2, b1, b2 = jnp.split(w, 4, axis=-1)
    return jnp.concatenate([-a2, a1, -b2, b1], axis=-1)


def _rope_table():
    t = np.arange(SEQ)
    row = (t // GRID_W).astype(np.float32)
    col = (t % GRID_W).astype(np.float32)
    inv = (np.float32(ROPE_THETA) ** (-np.arange(0, AXIS_ROPE, 2, dtype=np.float32) / AXIS_ROPE)).astype(np.float32)
    ar = row[:, None] * inv
    ac = col[:, None] * inv
    ang = np.concatenate([ar, ar, ac, ac], axis=-1).astype(np.float64)
    lat = np.concatenate([np.cos(ang), np.sin(ang)], axis=-1)
    ctx = np.concatenate([np.ones((CTX_LEN, QK_ROPE)), np.zeros((CTX_LEN, QK_ROPE))], axis=-1)
    return jnp.asarray(np.concatenate([lat, ctx], axis=0), dtype=F32)


def kernel(x, c, ctx, c_ctx, l0_ada_w, l0_ada_b, l0_norm1, l0_w_in, l0_conv_a, l0_q_norm, l0_w_qb, l0_kv_norm, l0_w_kvb, l0_w_out, l0_norm2, l0_ffn_up, l0_ffn_conv_w, l0_ffn_conv_b, l0_ffn_down, l1_ada_w, l1_ada_b, l1_norm1, l1_w_in, l1_gate_fw_w, l1_gate_fw_b, l1_gate_bw_w, l1_gate_bw_b, l1_o_norm, l1_w_out, l1_norm2, l1_ffn_up, l1_ffn_conv_w, l1_ffn_conv_b, l1_ffn_down, final_norm):
    assert x.shape == (1, SEQ, D_MODEL) and ctx.shape == (1, CTX_LEN, D_MODEL)
    x_lat, x_ctx = x[0], ctx[0]
    cvec = jnp.concatenate([c, c_ctx[None, :], jnp.zeros((SUBLANES_F32 - 2, D_MODEL), F32)], axis=0)
    mod0 = _mod_rows(_ada(cvec, l0_ada_w, l0_ada_b), l0_norm1, l0_norm2, final_norm)

    c_rope, c_q, c_conv = KV_LORA, KV_LORA + QK_ROPE, KV_LORA + QK_ROPE + Q_LORA
    w_conv0 = l0_w_in[:, c_conv:].astype(BF16)
    w_lat0 = jnp.concatenate([l0_w_in[:, c_q:c_conv], l0_w_in[:, :c_rope]], axis=1).astype(BF16)
    k_rope = l0_w_in[:, c_rope:c_q]
    w_kr = jnp.concatenate([k_rope, _rot_cols(k_rope)], axis=1).astype(BF16)
    z_conv, z_lat, kr = _inproj(x_lat, x_ctx, mod0, w_conv0, w_lat0, w_kr,
                                tn=w_conv0.shape[1] // 2, tn_b=Q_LORA + KV_LORA)

    wq = l0_w_qb.reshape(Q_LORA, MLA_HEADS, QK_DIM)
    wqn = wq[:, :, :QK_NOPE].reshape(Q_LORA, MLA_HEADS * QK_NOPE).astype(BF16)
    wqr = wq[:, :, QK_NOPE:]
    wqrr = _rot_cols(wqr).reshape(Q_LORA, MLA_HEADS * QK_ROPE).astype(BF16)
    wqr = wqr.reshape(Q_LORA, MLA_HEADS * QK_ROPE).astype(BF16)
    wkv = l0_w_kvb.reshape(KV_LORA, MLA_HEADS, QK_NOPE + V_HEAD)
    wk = wkv[:, :, :QK_NOPE].reshape(KV_LORA, MLA_HEADS * QK_NOPE).astype(BF16)
    wv = wkv[:, :, QK_NOPE:].reshape(KV_LORA, MLA_HEADS * V_HEAD).astype(BF16)
    q, k, v = _mla_prep(z_lat, kr, _rope_table(), l0_q_norm, wqn, wqr, wqrr, l0_kv_norm, wk, wv)
    o, up0, down0, up1, down1, w_out0, w_out1, w_kv1, w_qo1, w_lr, mods1 = _attention(
        q, k, v, l0_ffn_up, l0_ffn_down, l1_ffn_up, l1_ffn_down, l0_w_out, l1_w_out, l1_w_in,
        cvec, l1_ada_w, l1_ada_b)
    mod1 = _mod_rows(mods1, l1_norm1, l1_norm2, final_norm)
    xa = _ab_out(x_lat, x_ctx, z_conv, o, l0_conv_a, mod0, w_out0)
    xa = _ffn(xa, mod0, up0, l0_ffn_conv_w, l0_ffn_conv_b, down0, final=False)

    z1, lr = _inproj(xa, None, mod1, w_kv1, w_qo1, w_lr, tn=1536)
    pad_f = jnp.zeros((LANES - GATE_RANK, GLA_KEY), F32)
    gwf = jnp.concatenate([l1_gate_fw_w, pad_f], axis=0).astype(BF16)
    gwb = jnp.concatenate([jnp.zeros((GATE_RANK, GLA_KEY), F32), l1_gate_bw_w,
                           jnp.zeros((LANES - 2 * GATE_RANK, GLA_KEY), F32)], axis=0).astype(BF16)
    o_f, o_b = _gla(z1, lr, gwf, l1_gate_fw_b, gwb, l1_gate_bw_b)
    xa = _gla_out(xa, o_f, o_b, z1, l1_o_norm, mod1, w_out1)
    out = _ffn(xa, mod1, up1, l1_ffn_conv_w, l1_ffn_conv_b, down1, final=True)
    return out[None]
```

```python
import functools

import jax
import jax.numpy as jnp
import numpy as np
from jax import lax
from jax.experimental import pallas as pl
from jax.experimental.pallas import tpu as pltpu

F32 = jnp.float32
BF16 = jnp.bfloat16

D_MODEL = 2048
SEQ = 8192
GRID_W = 64
CTX_LEN = 256
T_ALL = SEQ + CTX_LEN
EPS = 1e-6
LOG2_E = 1.4426950408889634
N_MOD = 6
MLA_HEADS = 8
QK_NOPE = 128
QK_ROPE = 64
QK_DIM = QK_NOPE + QK_ROPE
V_HEAD = 128
Q_LORA = 512
KV_LORA = 256
ROPE_THETA = 10000.0
AXIS_ROPE = QK_ROPE // 2
CONV_CH = D_MODEL // 2
GLA_HEADS = 4
GLA_DK = D_MODEL // 2 // GLA_HEADS
GLA_DV = D_MODEL // GLA_HEADS
GLA_KEY = GLA_HEADS * GLA_DK
GLA_VAL = GLA_HEADS * GLA_DV
GATE_RANK = 16
GATE_NORMALIZER = 16.0
CHUNK = 64
D_FF = 5632

LANES = 128
SUBLANES_F32 = 8
SUBLANES_BF16 = 16
VMEM_LIMIT = 56 * 1024 * 1024
MOD_ROWS = 16
NORM_ROW = 12

TM = 768
TMO = 384
ROW_GROUP = 16
TQ = 256
GLA_BLOCK = 256
TF = 512


def _silu(v):
    return v / (1.0 + jnp.exp(-v))


def _row_ids(i, rows):
    return i * rows + lax.broadcasted_iota(jnp.int32, (rows, 1), 0)


def _modulated(x, norm, mod, shift_row, is_ctx):
    s = 2 * shift_row
    gain_lat = norm * (1.0 + mod[s + 2:s + 3, :])
    gain_ctx = norm * (1.0 + mod[s + 3:s + 4, :])
    y = x * lax.rsqrt(jnp.mean(x * x, axis=-1, keepdims=True) + EPS)
    return (y * jnp.where(is_ctx, gain_ctx, gain_lat)
            + jnp.where(is_ctx, mod[s + 1:s + 2, :], mod[s:s + 1, :]))


def _token_rows(x_ref, ctx_ref, is_ctx_tile, r0, n, cols=slice(None)):
    rows = x_ref[r0:r0 + n, cols]
    tile_rows = x_ref.shape[0]
    ctx_start = SEQ % tile_rows
    if ctx_ref is None or r0 + n <= ctx_start:
        return rows
    assert r0 >= ctx_start and ctx_start + CTX_LEN == tile_rows
    return jnp.where(is_ctx_tile, ctx_ref[r0 - ctx_start:r0 - ctx_start + n, cols], rows)


def _modulate_rows(x_ref, h_ref, norm, mod, shift_row, first_row, ctx_ref=None):
    tile_rows = x_ref.shape[0]
    ctx_start = SEQ % tile_rows
    assert ctx_start + CTX_LEN == tile_rows and ctx_start % ROW_GROUP == 0
    is_ctx_tile = first_row == (SEQ // tile_rows) * tile_rows
    s = 2 * shift_row
    gain_lat = norm * (1.0 + mod[s + 2:s + 3, :])
    gain_tail = jnp.where(is_ctx_tile, norm * (1.0 + mod[s + 3:s + 4, :]), gain_lat)
    shift_lat = mod[s:s + 1, :]
    shift_tail = jnp.where(is_ctx_tile, mod[s + 1:s + 2, :], shift_lat)
    for r0 in range(0, tile_rows, ROW_GROUP):
        x = _token_rows(x_ref, ctx_ref, is_ctx_tile, r0, ROW_GROUP)
        y = x * lax.rsqrt(jnp.mean(x * x, axis=-1, keepdims=True) + EPS)
        gain, shift = (gain_tail, shift_tail) if r0 >= ctx_start else (gain_lat, shift_lat)
        h_ref[r0:r0 + ROW_GROUP, :] = (y * gain + shift).astype(BF16)


def _ada_kernel(a_ref, w_ref, b_ref, o_ref):
    a = _silu(a_ref[...]).astype(BF16)
    o_ref[...] = jnp.dot(a, w_ref[...].astype(BF16), preferred_element_type=F32) + b_ref[...]


def _ada(cvec, ada_w, ada_b):
    n = ada_w.shape[1]
    tn = 1024
    return pl.pallas_call(
        _ada_kernel,
        grid=(n // tn,),
        in_specs=[
            pl.BlockSpec((SUBLANES_F32, D_MODEL), lambda j: (0, 0)),
            pl.BlockSpec((D_MODEL, tn), lambda j: (0, j)),
            pl.BlockSpec((1, tn), lambda j: (0, j)),
        ],
        out_specs=pl.BlockSpec((SUBLANES_F32, tn), lambda j: (0, j)),
        out_shape=jax.ShapeDtypeStruct((SUBLANES_F32, n), F32),
        compiler_params=pltpu.CompilerParams(
            dimension_semantics=("arbitrary",), vmem_limit_bytes=VMEM_LIMIT),
        name="ada_mod",
    )(cvec, ada_w, ada_b.reshape(1, n))


def _mod_rows(mods, norm1, norm2, final_norm):
    m = mods[:2].reshape(2, N_MOD, D_MODEL)
    m = jnp.transpose(m, (1, 0, 2)).reshape(2 * N_MOD, D_MODEL)
    gains = jnp.stack([norm1, norm2, final_norm], axis=0)
    pad = jnp.zeros((MOD_ROWS - NORM_ROW - gains.shape[0], D_MODEL), F32)
    return jnp.concatenate([m, gains, pad], axis=0)


def _inproj_kernel(x_ref, ctx_ref, mod_ref, wa_ref, wb_ref, we_ref, z_ref, *rest, na, split_tokens):
    zb_ref = rest[0] if len(rest) == 3 else z_ref
    ze_ref, h_ref = rest[-2:]
    i = pl.program_id(0)
    j = pl.program_id(1)

    @pl.when(j == 0)
    def _():
        mod = mod_ref[...]
        _modulate_rows(x_ref, h_ref, mod[NORM_ROW:NORM_ROW + 1, :], mod, 0, i * TM,
                       ctx_ref if split_tokens else None)
        ze_ref[...] = jnp.dot(h_ref[...], we_ref[...], preferred_element_type=F32)
        part = TM // 3
        for r0 in range(0, TM, part):
            z_ref[r0:r0 + part, :] = jnp.dot(h_ref[r0:r0 + part, :], wa_ref[...],
                                             preferred_element_type=F32).astype(z_ref.dtype)

    @pl.when((j > 0) & (j < na))
    def _():
        z_ref[...] = jnp.dot(h_ref[...], wa_ref[...], preferred_element_type=F32).astype(z_ref.dtype)

    @pl.when(j >= na)
    def _():
        zb_ref[...] = jnp.dot(h_ref[...], wb_ref[...], preferred_element_type=F32).astype(zb_ref.dtype)


def _inproj(tokens, ctx, mod, w_a, w_b, w_extra, tn, tn_b=None):
    split_out = tn_b is not None
    tn_b = tn_b if split_out else tn
    na = w_a.shape[1] // tn
    nb = w_b.shape[1] // tn_b
    n = (na + nb) * tn
    split_tokens = ctx is not None
    if not split_tokens:
        ctx = tokens
    wb_tile = lambda j: jnp.where(j == 0, nb - 1, jnp.maximum(j - na, 0))
    return pl.pallas_call(
        functools.partial(_inproj_kernel, na=na, split_tokens=split_tokens),
        grid=(T_ALL // TM, na + nb),
        in_specs=[
            pl.BlockSpec((TM, D_MODEL), lambda i, j: (i, 0)),
            pl.BlockSpec((CTX_LEN, D_MODEL), lambda i, j: (0, 0)),
            pl.BlockSpec((MOD_ROWS, D_MODEL), lambda i, j: (0, 0)),
            pl.BlockSpec((D_MODEL, tn), lambda i, j: (0, jnp.minimum(j, na - 1))),
            pl.BlockSpec((D_MODEL, tn_b), lambda i, j: (0, wb_tile(j))),
            pl.BlockSpec((D_MODEL, LANES), lambda i, j: (0, 0)),
        ],
        out_specs=(
            [pl.BlockSpec((TM, tn), lambda i, j: (i, jnp.minimum(j, na - 1))),
             pl.BlockSpec((TM, tn_b), lambda i, j: (i, jnp.maximum(j - na, 0)))]
            if split_out else [pl.BlockSpec((TM, tn), lambda i, j: (i, j))]
        ) + [pl.BlockSpec((TM, LANES), lambda i, j: (i, 0))],
        out_shape=(
            [jax.ShapeDtypeStruct((T_ALL, na * tn), BF16), jax.ShapeDtypeStruct((T_ALL, nb * tn_b), BF16)]
            if split_out else [jax.ShapeDtypeStruct((T_ALL, n), BF16)]
        ) + [jax.ShapeDtypeStruct((T_ALL, LANES), F32)],
        scratch_shapes=[pltpu.VMEM((TM, D_MODEL), BF16)],
        compiler_params=pltpu.CompilerParams(
            dimension_semantics=("parallel", "arbitrary"), vmem_limit_bytes=VMEM_LIMIT),
        name="in_proj",
    )(tokens, ctx, mod, w_a, w_b, w_extra)


def _mla_prep_kernel(ql_ref, kvl_ref, kr_ref, cs_ref, qn_ref, wqn_ref, wqr_ref, wqrr_ref,
                     kvn_ref, wk_ref, wv_ref, q_ref, k_ref, v_ref):
    cos = cs_ref[:, :QK_ROPE]
    sin = cs_ref[:, QK_ROPE:]

    ql = ql_ref[...].astype(F32)
    qn = (ql * lax.rsqrt(jnp.mean(ql * ql, axis=-1, keepdims=True) + EPS) * qn_ref[...]).astype(BF16)
    scale = QK_DIM ** -0.5 * LOG2_E
    q_nope = jnp.dot(qn, wqn_ref[...], preferred_element_type=F32) * scale
    q_r = jnp.dot(qn, wqr_ref[...], preferred_element_type=F32)
    q_rr = jnp.dot(qn, wqrr_ref[...], preferred_element_type=F32)

    kvl = kvl_ref[...].astype(F32)
    kvn = (kvl * lax.rsqrt(jnp.mean(kvl * kvl, axis=-1, keepdims=True) + EPS) * kvn_ref[...]).astype(BF16)
    k_nope = jnp.dot(kvn, wk_ref[...], preferred_element_type=F32)
    v = jnp.dot(kvn, wv_ref[...], preferred_element_type=F32)
    kr = kr_ref[...]
    k_rope = (kr[:, :QK_ROPE] * cos + kr[:, QK_ROPE:] * sin).astype(BF16)

    for h in range(MLA_HEADS):
        q_ref[h, :, :QK_NOPE] = q_nope[:, h * QK_NOPE:(h + 1) * QK_NOPE].astype(BF16)
        rope = q_r[:, h * QK_ROPE:(h + 1) * QK_ROPE] * cos + q_rr[:, h * QK_ROPE:(h + 1) * QK_ROPE] * sin
        q_ref[h, :, QK_NOPE:] = (rope * scale).astype(BF16)
        k_ref[h, :, :QK_NOPE] = k_nope[:, h * QK_NOPE:(h + 1) * QK_NOPE].astype(BF16)
        k_ref[h, :, QK_NOPE:] = k_rope
        v_ref[h, :, :V_HEAD] = v[:, h * V_HEAD:(h + 1) * V_HEAD].astype(BF16)
        v_ref[h, :, V_HEAD:] = jnp.ones((v.shape[0], V_HEAD), BF16)


def _mla_prep(z, kr, cs, q_norm, wqn, wqr, wqrr, kv_norm, wk, wv):
    tm = TM
    const = lambda shape: pl.BlockSpec(shape, lambda i: (0,) * len(shape))
    return pl.pallas_call(
        _mla_prep_kernel,
        grid=(T_ALL // tm,),
        in_specs=[
            pl.BlockSpec((tm, Q_LORA), lambda i: (i, 0)),
            pl.BlockSpec((tm, KV_LORA), lambda i: (i, Q_LORA // KV_LORA)),
            pl.BlockSpec((tm, LANES), lambda i: (i, 0)),
            pl.BlockSpec((tm, LANES), lambda i: (i, 0)),
            const((1, Q_LORA)),
            const((Q_LORA, MLA_HEADS * QK_NOPE)),
            const((Q_LORA, MLA_HEADS * QK_ROPE)),
            const((Q_LORA, MLA_HEADS * QK_ROPE)),
            const((1, KV_LORA)),
            const((KV_LORA, MLA_HEADS * QK_NOPE)),
            const((KV_LORA, MLA_HEADS * V_HEAD)),
        ],
        out_specs=[
            pl.BlockSpec((MLA_HEADS, tm, QK_DIM), lambda i: (0, i, 0)),
            pl.BlockSpec((MLA_HEADS, tm, QK_DIM), lambda i: (0, i, 0)),
            pl.BlockSpec((MLA_HEADS, tm, 2 * V_HEAD), lambda i: (0, i, 0)),
        ],
        out_shape=[
            jax.ShapeDtypeStruct((MLA_HEADS, T_ALL, QK_DIM), BF16),
            jax.ShapeDtypeStruct((MLA_HEADS, T_ALL, QK_DIM), BF16),
            jax.ShapeDtypeStruct((MLA_HEADS, T_ALL, 2 * V_HEAD), BF16),
        ],
        compiler_params=pltpu.CompilerParams(
            dimension_semantics=("parallel",), vmem_limit_bytes=VMEM_LIMIT),
        name="mla_prep",
    )(z, z, kr, cs, q_norm.reshape(1, Q_LORA), wqn, wqr, wqrr, kv_norm.reshape(1, KV_LORA), wk, wv)


def _attn_kernel(qa_ref, qn_ref, k_ref, v_ref, up0_ref, dn0_ref, up1_ref, dn1_ref, wo0_ref, wo1_ref,
                 win1_ref, cvec_ref, adaw_ref, adab_ref,
                 o_ref, up0_o, dn0_o, up1_o, dn1_o, wo0_o, wo1_o, wkv_o, wqo_o, wlr_o, mods_o,
                 sa_ref, sb_ref):
    j = pl.program_id(1)
    n_pairs = SEQ // (2 * TQ)

    def convert_weight_slabs():
        for src, dst in ((up0_ref, up0_o), (dn0_ref, dn0_o), (up1_ref, up1_o), (dn1_ref, dn1_o),
                         (wo0_ref, wo0_o), (wo1_ref, wo1_o)):
            dst[...] = src[...].astype(BF16)
        _ada_kernel(cvec_ref, adaw_ref, adab_ref, mods_o)
        w1 = win1_ref[...]
        kv_cols = GLA_KEY + GLA_VAL
        wkv_o[...] = w1[:, :kv_cols].astype(BF16)
        wqo_o[...] = w1[:, kv_cols + 2 * GATE_RANK:].astype(BF16)
        wlr_o[...] = jnp.concatenate(
            [w1[:, kv_cols:kv_cols + 2 * GATE_RANK],
             jnp.zeros((w1.shape[0], LANES - 2 * GATE_RANK), F32)], axis=1).astype(BF16)

    def score(q, s_ref):
        s_ref[...] = lax.dot_general(q, k_ref[0], (((1,), (1,)), ((), ())), preferred_element_type=F32)

    def attend(s, v, rows):
        p = jnp.exp2(s - jnp.max(s, axis=-1, keepdims=True)).astype(BF16)
        acc = jnp.dot(p, v, preferred_element_type=F32)
        o_ref[rows, :] = (acc[:, :V_HEAD] / acc[:, V_HEAD:V_HEAD + 1]).astype(o_ref.dtype)

    def pair_step():
        score(qa_ref[0, TQ:, :], sb_ref)
        attend(sa_ref[...], v_ref[0], slice(0, TQ))
        score(qn_ref[0], sa_ref)
        attend(sb_ref[...], v_ref[0], slice(TQ, 2 * TQ))
        convert_weight_slabs()

    @pl.when(j == 0)
    def _():
        score(qa_ref[0, :TQ, :], sa_ref)
        pair_step()

    @pl.when((j > 0) & (j < n_pairs))
    def _():
        pair_step()

    @pl.when(j == n_pairs)
    def _():
        convert_weight_slabs()
        attend(sa_ref[:, SEQ:], v_ref[0, SEQ:, :], slice(0, TQ))
        o_ref[TQ:, :] = jnp.zeros((TQ, V_HEAD), o_ref.dtype)


def _attention(q, k, v, ffn_up0, ffn_down0, ffn_up1, ffn_down1, w_out0, w_out1, w_in1,
               cvec, ada_w1, ada_b1):
    n_pairs = SEQ // (2 * TQ)
    n_j = n_pairs + 1
    n_mod = ada_w1.shape[1]
    ada_col = lambda h, j: (0, jnp.minimum(h * n_j + j, n_mod // LANES - 1))
    assert MLA_HEADS * n_j >= n_mod // LANES

    def slab(n_rows, rows, cols_in, cols_out=None):
        assert n_rows % rows == 0 and MLA_HEADS * n_j >= n_rows // rows
        index = lambda h, j: (jnp.minimum(h * n_j + j, n_rows // rows - 1), 0)
        return (pl.BlockSpec((rows, cols_in), index),
                pl.BlockSpec((rows, cols_in if cols_out is None else cols_out), index))

    up_in, up_out = slab(D_MODEL, SUBLANES_BF16, 2 * D_FF)
    dn_in, dn_out = slab(D_FF, 4 * SUBLANES_BF16, D_MODEL)
    wo_in, wo_out = slab(D_MODEL, SUBLANES_BF16, D_MODEL)
    n_in1 = w_in1.shape[1]
    kv_cols = GLA_KEY + GLA_VAL
    w1_in, w1_kv = slab(D_MODEL, SUBLANES_BF16, n_in1, kv_cols)
    _, w1_qo = slab(D_MODEL, SUBLANES_BF16, n_in1, n_in1 - kv_cols - 2 * GATE_RANK)
    _, w1_lr = slab(D_MODEL, SUBLANES_BF16, n_in1, LANES)
    bf = lambda shape: jax.ShapeDtypeStruct(shape, BF16)
    return pl.pallas_call(
        _attn_kernel,
        grid=(MLA_HEADS, n_j),
        in_specs=[
            pl.BlockSpec((1, 2 * TQ, QK_DIM), lambda h, j: (h, jnp.minimum(j, n_pairs - 1), 0)),
            pl.BlockSpec((1, TQ, QK_DIM), lambda h, j: (h, jnp.minimum(2 * j + 2, SEQ // TQ), 0)),
            pl.BlockSpec((1, T_ALL, QK_DIM), lambda h, j: (h, 0, 0)),
            pl.BlockSpec((1, T_ALL, 2 * V_HEAD), lambda h, j: (h, 0, 0)),
            up_in, dn_in, up_in, dn_in, wo_in, wo_in, w1_in,
            pl.BlockSpec((SUBLANES_F32, D_MODEL), lambda h, j: (0, 0)),
            pl.BlockSpec((D_MODEL, LANES), ada_col),
            pl.BlockSpec((1, LANES), ada_col),
        ],
        out_specs=[
            pl.BlockSpec((2 * TQ, V_HEAD), lambda h, j: (j, h)),
            up_out, dn_out, up_out, dn_out, wo_out, wo_out, w1_kv, w1_qo, w1_lr,
            pl.BlockSpec((SUBLANES_F32, LANES), ada_col),
        ],
        out_shape=[
            bf((n_j * 2 * TQ, MLA_HEADS * V_HEAD)),
            bf(ffn_up0.shape), bf(ffn_down0.shape), bf(ffn_up1.shape), bf(ffn_down1.shape),
            bf(w_out0.shape), bf(w_out1.shape),
            bf((D_MODEL, kv_cols)), bf((D_MODEL, n_in1 - kv_cols - 2 * GATE_RANK)), bf((D_MODEL, LANES)),
            jax.ShapeDtypeStruct((SUBLANES_F32, n_mod), F32),
        ],
        scratch_shapes=[pltpu.VMEM((TQ, T_ALL), F32), pltpu.VMEM((TQ, T_ALL), F32)],
        compiler_params=pltpu.CompilerParams(
            dimension_semantics=("arbitrary", "arbitrary"), vmem_limit_bytes=VMEM_LIMIT),
        name="mla_attn",
    )(q, q, k, v, ffn_up0, ffn_down0, ffn_up1, ffn_down1, w_out0, w_out1, w_in1,
      cvec, ada_w1, ada_b1.reshape(1, n_mod))


def _neighbour_masks(rows):
    has_prev = (rows != 0) & (rows != SEQ)
    has_next = (rows != SEQ - 1) & (rows != T_ALL - 1)
    return has_prev, has_next


def _ab_out_kernel(x_ref, ctx_ref, ax_ref, ab_ref, ac_ref, axp_ref, acp_ref, axn_ref, acn_ref, o_ref,
                   cw_ref, mod_ref, w_ref, out_ref, h_ref, p_ref):
    i = pl.program_id(0)
    tm = x_ref.shape[0]
    halo = SUBLANES_F32
    hb = SUBLANES_BF16
    f32 = lambda ref, r0, n: ref[r0:r0 + n, :].astype(F32)

    p_ref[halo - 1:halo, :] = f32(acp_ref, hb - 1, 1) * f32(axp_ref, hb - 1, 1)
    p_ref[halo + tm:halo + tm + 1, :] = f32(acn_ref, 0, 1) * f32(axn_ref, 0, 1)
    for r0 in range(0, tm, ROW_GROUP):
        p_ref[halo + r0:halo + r0 + ROW_GROUP, :] = f32(ac_ref, r0, ROW_GROUP) * f32(ax_ref, r0, ROW_GROUP)

    cw = cw_ref[...]
    mod = mod_ref[...]
    gate = jnp.where(_row_ids(i, tm) >= SEQ, mod[5:6, :], mod[4:5, :])
    ctx_start = SEQ % tm
    is_ctx_tile = i == SEQ // tm

    def residual(cols):
        return jnp.concatenate(
            [_token_rows(x_ref, ctx_ref, is_ctx_tile, 0, ctx_start, cols),
             _token_rows(x_ref, ctx_ref, is_ctx_tile, ctx_start, tm - ctx_start, cols)], axis=0)

    for r0 in range(0, tm, ROW_GROUP):
        rows = i * tm + r0 + lax.broadcasted_iota(jnp.int32, (ROW_GROUP, 1), 0)
        has_prev, has_next = _neighbour_masks(rows)
        at = lambda shift: p_ref[halo + r0 + shift:halo + r0 + shift + ROW_GROUP, :]
        conv = (jnp.where(has_prev, at(-1), 0.0) * cw[0:1, :] + at(0) * cw[1:2, :]
                + jnp.where(has_next, at(1), 0.0) * cw[2:3, :])
        h_ref[r0:r0 + ROW_GROUP, :CONV_CH] = (f32(ab_ref, r0, ROW_GROUP) * conv).astype(BF16)
    h_ref[:, CONV_CH:] = o_ref[...]
    y = jnp.dot(h_ref[...], w_ref[...], preferred_element_type=F32)
    out_ref[...] = residual(slice(None)) + gate * y


def _ab_out(x_lat, ctx, z, o, conv_a, mod, w_out):
    tm = TMO
    hb = SUBLANES_BF16
    nb = tm // hb
    last = T_ALL // hb - 1
    prev_map = lambda c: (lambda i: (jnp.maximum(i * nb - 1, 0), c))
    next_map = lambda c: (lambda i: (jnp.minimum((i + 1) * nb, last), c))
    return pl.pallas_call(
        _ab_out_kernel,
        grid=(T_ALL // tm,),
        in_specs=[
            pl.BlockSpec((tm, D_MODEL), lambda i: (i, 0)),
            pl.BlockSpec((CTX_LEN, D_MODEL), lambda i: (0, 0)),
            pl.BlockSpec((tm, CONV_CH), lambda i: (i, 0)),
            pl.BlockSpec((tm, CONV_CH), lambda i: (i, 1)),
            pl.BlockSpec((tm, CONV_CH), lambda i: (i, 2)),
            pl.BlockSpec((hb, CONV_CH), prev_map(0)),
            pl.BlockSpec((hb, CONV_CH), prev_map(2)),
            pl.BlockSpec((hb, CONV_CH), next_map(0)),
            pl.BlockSpec((hb, CONV_CH), next_map(2)),
            pl.BlockSpec((tm, MLA_HEADS * V_HEAD), lambda i: (i, 0)),
            pl.BlockSpec((3, CONV_CH), lambda i: (0, 0)),
            pl.BlockSpec((MOD_ROWS, D_MODEL), lambda i: (0, 0)),
            pl.BlockSpec((D_MODEL, D_MODEL), lambda i: (0, 0), pipeline_mode=pl.Buffered(1)),
        ],
        out_specs=pl.BlockSpec((tm, D_MODEL), lambda i: (i, 0)),
        out_shape=jax.ShapeDtypeStruct((T_ALL, D_MODEL), F32),
        scratch_shapes=[
            pltpu.VMEM((tm, D_MODEL), BF16),
            pltpu.VMEM((tm + 2 * SUBLANES_F32, CONV_CH), F32),
        ],
        compiler_params=pltpu.CompilerParams(
            dimension_semantics=("parallel",), vmem_limit_bytes=VMEM_LIMIT),
        name="ab_out",
    )(x_lat, ctx, z, z, z, z, z, z, z, o, conv_a, mod, w_out)


def _ffn_kernel(x_ref, xp_ref, xn_ref, mod_ref, wg_ref, wv_ref, cwg_ref, cwv_ref, wd_ref,
                out_ref, h_ref, acta_ref, actb_ref, *, final):
    i = pl.program_id(0)
    j = pl.program_id(1)
    nf = pl.num_programs(1) - 1
    halo = SUBLANES_F32
    rows = _row_ids(i, TM)

    def prologue():
        mod = mod_ref[...]
        norm = mod[NORM_ROW + 1:NORM_ROW + 2, :]
        _modulate_rows(x_ref, h_ref, norm, mod, 3, i * TM)
        xh = jnp.concatenate([xp_ref[...], xn_ref[...]], axis=0)
        off = lax.broadcasted_iota(jnp.int32, (2 * halo, 1), 0)
        rows_h = jnp.where(off < halo, i * TM - halo + off, (i + 1) * TM - halo + off)
        h_ref[TM:, :] = _modulated(xh, norm, mod, 3, rows_h >= SEQ).astype(BF16)
        out_ref[...] = jnp.zeros_like(out_ref)

    def conv(w_ref, cw_ref):
        has_prev, has_next = _neighbour_masks(rows)
        local = lax.broadcasted_iota(jnp.int32, (TM, 1), 0)
        w = w_ref[...]
        part = TM // 3
        u_a = jnp.dot(h_ref[:part, :], w, preferred_element_type=F32)
        u_b = jnp.dot(h_ref[part:2 * part, :], w, preferred_element_type=F32)
        u_c = jnp.dot(h_ref[2 * part:, :], w, preferred_element_type=F32)
        u = jnp.concatenate([u_a, u_b, u_c[:part, :]], axis=0)
        uh = u_c[part:, :]
        cw = cw_ref[...]
        u_prev = jnp.where(local == 0, uh[halo - 1:halo, :], pltpu.roll(u, 1, axis=0))
        u_next = jnp.where(local == TM - 1, uh[halo:halo + 1, :], pltpu.roll(u, TM - 1, axis=0))
        return (jnp.where(has_prev, u_prev, 0.0) * cw[0:1, :] + u * cw[1:2, :]
                + jnp.where(has_next, u_next, 0.0) * cw[2:3, :] + cw[3:4, :])

    def up(act_ref):
        act_ref[...] = (_silu(conv(wg_ref, cwg_ref)) * conv(wv_ref, cwv_ref)).astype(BF16)

    def down(act_ref):
        out_ref[...] += jnp.dot(act_ref[...], wd_ref[...], preferred_element_type=F32)

    even = lax.rem(j, 2) == 0
    inner = (j > 0) & (j < nf)

    @pl.when(j == 0)
    def _():
        prologue()
        up(acta_ref)

    @pl.when(inner & even)
    def _():
        up(acta_ref)
        down(actb_ref)

    @pl.when(inner & jnp.logical_not(even))
    def _():
        up(actb_ref)
        down(acta_ref)

    @pl.when(j == nf)
    def _():
        down(acta_ref if (D_FF // TF - 1) % 2 == 0 else actb_ref)
        mod = mod_ref[...]
        gate = jnp.where(rows >= SEQ, mod[11:12, :], mod[10:11, :])
        y = x_ref[...] + gate * out_ref[...]
        if final:
            y = y * lax.rsqrt(jnp.mean(y * y, axis=-1, keepdims=True) + EPS) * mod[NORM_ROW + 2:NORM_ROW + 3, :]
        out_ref[...] = y


def _ffn(xa, mod, w_up, conv_w, conv_b, w_down, final):
    nf = D_FF // TF
    hb = SUBLANES_F32
    nb = TM // hb
    last = T_ALL // hb - 1
    out_rows = SEQ if final else T_ALL
    cwb = jnp.concatenate([conv_w, conv_b[None, :],
                           jnp.zeros((SUBLANES_F32 - 4, 2 * D_FF), F32)], axis=0)
    up_tile = lambda j: jnp.minimum(j, nf - 1)
    return pl.pallas_call(
        functools.partial(_ffn_kernel, final=final),
        grid=(T_ALL // TM, nf + 1),
        in_specs=[
            pl.BlockSpec((TM, D_MODEL), lambda i, j: (i, 0)),
            pl.BlockSpec((hb, D_MODEL), lambda i, j: (jnp.maximum(i * nb - 1, 0), 0)),
            pl.BlockSpec((hb, D_MODEL), lambda i, j: (jnp.minimum((i + 1) * nb, last), 0)),
            pl.BlockSpec((MOD_ROWS, D_MODEL), lambda i, j: (0, 0)),
            pl.BlockSpec((D_MODEL, TF), lambda i, j: (0, up_tile(j))),
            pl.BlockSpec((D_MODEL, TF), lambda i, j: (0, up_tile(j) + nf)),
            pl.BlockSpec((SUBLANES_F32, TF), lambda i, j: (0, up_tile(j))),
            pl.BlockSpec((SUBLANES_F32, TF), lambda i, j: (0, up_tile(j) + nf)),
            pl.BlockSpec((TF, D_MODEL), lambda i, j: (jnp.maximum(j - 1, 0), 0)),
        ],
        out_specs=pl.BlockSpec((TM, D_MODEL), lambda i, j: (i, 0)),
        out_shape=jax.ShapeDtypeStruct((out_rows, D_MODEL), F32),
        scratch_shapes=[
            pltpu.VMEM((TM + 2 * hb, D_MODEL), BF16),
            pltpu.VMEM((TM, TF), BF16),
            pltpu.VMEM((TM, TF), BF16),
        ],
        compiler_params=pltpu.CompilerParams(
            dimension_semantics=("parallel", "arbitrary"), vmem_limit_bytes=VMEM_LIMIT),
        name="conv_ffn_final" if final else "conv_ffn",
    )(xa, xa, xa, mod, w_up, w_up, cwb, cwb, w_down)


def _per_chunk(rows):
    return jnp.concatenate([jnp.broadcast_to(t, (CHUNK, t.shape[-1])) for t in rows], axis=0)


def _gla_direction(q_ref, k_ref, v_refs, lr_ref, gw_ref, gb_ref, o_ref, st_ref, causal):
    n = GLA_BLOCK
    nc = n // CHUNK
    nt = (((1,), (1,)), ((), ()))
    r = lax.broadcasted_iota(jnp.int32, (n, n), 0)
    c = lax.broadcasted_iota(jnp.int32, (n, n), 1)
    shift = CHUNK.bit_length() - 1
    rc = jnp.right_shift(r, shift)
    cc = jnp.right_shift(c, shift)
    tri = (c <= r) if causal else (c >= r)
    delta = (rc - cc) if causal else (cc - rc)

    pre = jnp.dot(lr_ref[...].astype(BF16), gw_ref[...], preferred_element_type=F32) + gb_ref[...]
    g = (jnp.minimum(pre, 0.0) - jnp.log(1.0 + jnp.exp(-jnp.abs(pre)))) / GATE_NORMALIZER
    tri_b = jnp.where(tri, 1.0, 0.0).astype(BF16)
    g_hi = g.astype(BF16)
    g_lo = (g - g_hi.astype(F32)).astype(BF16)
    bb = (jnp.dot(tri_b, g_hi, preferred_element_type=F32)
          + jnp.dot(tri_b, g_lo, preferred_element_type=F32))

    zero = jnp.zeros((1, GLA_KEY), F32)
    if causal:
        edge = [zero] + [bb[ci * CHUNK - 1:ci * CHUNK, :] for ci in range(1, nc)]
        tot = [bb[ci * CHUNK + CHUNK - 1:ci * CHUNK + CHUNK, :] - edge[ci] for ci in range(nc)]
        ball = bb[n - 1:n, :]
        near = [zero] + tot[:nc - 1]
        far = [zero, zero] + tot[:nc - 2]
    else:
        edge = [bb[(ci + 1) * CHUNK:(ci + 1) * CHUNK + 1, :] for ci in range(nc - 1)] + [zero]
        tot = [bb[ci * CHUNK:ci * CHUNK + 1, :] - edge[ci] for ci in range(nc)]
        ball = bb[0:1, :]
        near = tot[1:] + [zero]
        far = tot[2:] + [zero, zero]
    b = bb - _per_chunk(edge)
    skip1 = _per_chunk([jnp.exp(t) for t in near])
    skip2 = _per_chunk([jnp.exp(t1 + t2) for t1, t2 in zip(near, far)])
    to_chunk_end = _per_chunk([jnp.exp(t) for t in tot])
    from_block_edge = _per_chunk([jnp.exp(e) for e in edge])
    to_block_end = _per_chunk([jnp.exp(ball - e - t) for e, t in zip(edge, tot)])
    dec_all = jnp.exp(ball)

    q = q_ref[...].astype(F32) * (GLA_DK ** -0.5)
    k = k_ref[...].astype(F32)
    qe_f = q * jnp.exp(b)
    ke_f = k * jnp.exp(-b)
    kd_f = ke_f * to_chunk_end
    qe = qe_f.astype(BF16)
    ke = ke_f.astype(BF16)
    kd = kd_f.astype(BF16)
    q1 = (qe_f * skip1).astype(BF16)
    q2 = (qe_f * skip2).astype(BF16)
    q_in = (qe_f * from_block_edge).astype(BF16)
    k_out = (kd_f * to_block_end).astype(BF16)

    for hd in range(GLA_HEADS):
        ks = slice(hd * GLA_DK, (hd + 1) * GLA_DK)
        v_half = v_refs[hd // (GLA_HEADS // 2)]
        v_col = (hd % (GLA_HEADS // 2)) * GLA_DV
        v = v_half[:, v_col:v_col + GLA_DV]
        q_far = jnp.concatenate([qe[:, ks], q1[:, ks], q2[:, ks]], axis=0)
        a0 = lax.dot_general(qe[:, ks], ke[:, ks], nt, preferred_element_type=F32)
        a123 = lax.dot_general(q_far, kd[:, ks], nt, preferred_element_type=F32)
        att = jnp.where((delta == 0) & tri, a0, 0.0)
        for d in range(1, nc):
            att = jnp.where(delta == d, a123[(d - 1) * n:d * n, :], att)
        st = st_ref[hd]
        o_ref[:, hd * GLA_DV:(hd + 1) * GLA_DV] = (
            jnp.dot(att.astype(BF16), v, preferred_element_type=F32)
            + lax.dot_general(q_in[:, ks], st.astype(BF16), nt, preferred_element_type=F32)
        ).astype(o_ref.dtype)
        upd = lax.dot_general(v, k_out[:, ks], (((0,), (0,)), ((), ())), preferred_element_type=F32)
        st_ref[hd] = st * dec_all[:, ks] + upd


def _gla_kernel(qf_ref, kf_ref, vf0_ref, vf1_ref, lrf_ref, qb_ref, kb_ref, vb0_ref, vb1_ref, lrb_ref,
                gwf_ref, gbf_ref, gwb_ref, gbb_ref, of_ref, ob_ref, sf_ref, sb_ref):
    @pl.when(pl.program_id(0) == 0)
    def _():
        sf_ref[...] = jnp.zeros_like(sf_ref)
        sb_ref[...] = jnp.zeros_like(sb_ref)

    _gla_direction(qf_ref, kf_ref, (vf0_ref, vf1_ref), lrf_ref, gwf_ref, gbf_ref, of_ref, sf_ref, True)
    _gla_direction(qb_ref, kb_ref, (vb0_ref, vb1_ref), lrb_ref, gwb_ref, gbb_ref, ob_ref, sb_ref, False)


def _gla(z, lr, gwf, gbf, gwb, gbb):
    nblk = T_ALL // GLA_BLOCK
    ctx_blk = SEQ // GLA_BLOCK
    fwd = lambda s: jnp.where(s == 0, ctx_blk, s - 1)
    bwd = lambda s: jnp.where(s == 0, ctx_blk, ctx_blk - s)

    def specs(order):
        return [
            pl.BlockSpec((GLA_BLOCK, GLA_KEY), lambda s: (order(s), 3)),
            pl.BlockSpec((GLA_BLOCK, GLA_KEY), lambda s: (order(s), 0)),
            pl.BlockSpec((GLA_BLOCK, GLA_KEY), lambda s: (order(s), 1)),
            pl.BlockSpec((GLA_BLOCK, GLA_KEY), lambda s: (order(s), 2)),
            pl.BlockSpec((GLA_BLOCK, LANES), lambda s: (order(s), 0)),
        ]

    gate_specs = [
        pl.BlockSpec((LANES, GLA_KEY), lambda s: (0, 0)),
        pl.BlockSpec((1, GLA_KEY), lambda s: (0, 0)),
    ]
    return pl.pallas_call(
        _gla_kernel,
        grid=(nblk,),
        in_specs=specs(fwd) + specs(bwd) + gate_specs + gate_specs,
        out_specs=[
            pl.BlockSpec((GLA_BLOCK, GLA_VAL), lambda s: (fwd(s), 0)),
            pl.BlockSpec((GLA_BLOCK, GLA_VAL), lambda s: (bwd(s), 0)),
        ],
        out_shape=[
            jax.ShapeDtypeStruct((T_ALL, GLA_VAL), BF16),
            jax.ShapeDtypeStruct((T_ALL, GLA_VAL), BF16),
        ],
        scratch_shapes=[
            pltpu.VMEM((GLA_HEADS, GLA_DV, GLA_DK), F32),
            pltpu.VMEM((GLA_HEADS, GLA_DV, GLA_DK), F32),
        ],
        compiler_params=pltpu.CompilerParams(
            dimension_semantics=("arbitrary",), vmem_limit_bytes=VMEM_LIMIT),
        name="gla_scan",
    )(z, z, z, z, lr, z, z, z, z, lr, gwf, gbf.reshape(1, GLA_KEY), gwb, gbb.reshape(1, GLA_KEY))


def _gla_out_kernel(x_ref, of_ref, ob_ref, og_ref, on_ref, mod_ref, w_ref, out_ref, h_ref):
    i = pl.program_id(0)
    tm = x_ref.shape[0]
    on = on_ref[...]
    mod = mod_ref[...]
    gate = jnp.where(_row_ids(i, tm) >= SEQ, mod[5:6, :], mod[4:5, :])
    for r0 in range(0, tm, ROW_GROUP):
        rs = slice(r0, r0 + ROW_GROUP)
        for hd in range(GLA_HEADS):
            sl = slice(hd * GLA_DV, (hd + 1) * GLA_DV)
            oh = of_ref[rs, sl].astype(F32) + ob_ref[rs, sl].astype(F32)
            y = oh * lax.rsqrt(jnp.mean(oh * oh, axis=-1, keepdims=True) + EPS) * on
            h_ref[rs, sl] = (y * _silu(og_ref[rs, sl].astype(F32))).astype(BF16)
    y = jnp.dot(h_ref[...], w_ref[...], preferred_element_type=F32)
    out_ref[...] = x_ref[...] + gate * y


def _gla_out(xa, o_f, o_b, z, o_norm, mod, w_out):
    tm = TMO
    og_blk = 2 * GLA_KEY // GLA_VAL + 1
    return pl.pallas_call(
        _gla_out_kernel,
        grid=(T_ALL // tm,),
        in_specs=[
            pl.BlockSpec((tm, D_MODEL), lambda i: (i, 0)),
            pl.BlockSpec((tm, GLA_VAL), lambda i: (i, 0)),
            pl.BlockSpec((tm, GLA_VAL), lambda i: (i, 0)),
            pl.BlockSpec((tm, GLA_VAL), lambda i: (i, og_blk)),
            pl.BlockSpec((1, GLA_DV), lambda i: (0, 0)),
            pl.BlockSpec((MOD_ROWS, D_MODEL), lambda i: (0, 0)),
            pl.BlockSpec((GLA_VAL, D_MODEL), lambda i: (0, 0), pipeline_mode=pl.Buffered(1)),
        ],
        out_specs=pl.BlockSpec((tm, D_MODEL), lambda i: (i, 0)),
        out_shape=jax.ShapeDtypeStruct((T_ALL, D_MODEL), F32),
        scratch_shapes=[pltpu.VMEM((tm, GLA_VAL), BF16)],
        compiler_params=pltpu.CompilerParams(
            dimension_semantics=("parallel",), vmem_limit_bytes=VMEM_LIMIT),
        name="gla_out",
    )(xa, o_f, o_b, z, o_norm.reshape(1, GLA_DV), mod, w_out)


def _rot_cols(w):
    a1, a2, b1, b2 = jnp.split(w, 4, axis=-1)
    return jnp.concatenate([-a2, a1, -b2, b1], axis=-1)


def _rope_table():
    t = np.arange(SEQ)
    row = (t // GRID_W).astype(np.float32)
    col = (t % GRID_W).astype(np.float32)
    inv = (np.float32(ROPE_THETA) ** (-np.arange(0, AXIS_ROPE, 2, dtype=np.float32) / AXIS_ROPE)).astype(np.float32)
    ar = row[:, None] * inv
    ac = col[:, None] * inv
    ang = np.concatenate([ar, ar, ac, ac], axis=-1).astype(np.float64)
    lat = np.concatenate([np.cos(ang), np.sin(ang)], axis=-1)
    ctx = np.concatenate([np.ones((CTX_LEN, QK_ROPE)), np.zeros((CTX_LEN, QK_ROPE))], axis=-1)
    return jnp.asarray(np.concatenate([lat, ctx], axis=0), dtype=F32)


def kernel(x, c, ctx, c_ctx, l0_ada_w, l0_ada_b, l0_norm1, l0_w_in, l0_conv_a, l0_q_norm, l0_w_qb, l0_kv_norm, l0_w_kvb, l0_w_out, l0_norm2, l0_ffn_up, l0_ffn_conv_w, l0_ffn_conv_b, l0_ffn_down, l1_ada_w, l1_ada_b, l1_norm1, l1_w_in, l1_gate_fw_w, l1_gate_fw_b, l1_gate_bw_w, l1_gate_bw_b, l1_o_norm, l1_w_out, l1_norm2, l1_ffn_up, l1_ffn_conv_w, l1_ffn_conv_b, l1_ffn_down, final_norm):
    assert x.shape == (1, SEQ, D_MODEL) and ctx.shape == (1, CTX_LEN, D_MODEL)
    x_lat, x_ctx = x[0], ctx[0]
    cvec = jnp.concatenate([c, c_ctx[None, :], jnp.zeros((SUBLANES_F32 - 2, D_MODEL), F32)], axis=0)
    mod0 = _mod_rows(_ada(cvec, l0_ada_w, l0_ada_b), l0_norm1, l0_norm2, final_norm)

    c_rope, c_q, c_conv = KV_LORA, KV_LORA + QK_ROPE, KV_LORA + QK_ROPE + Q_LORA
    w_conv0 = l0_w_in[:, c_conv:].astype(BF16)
    w_lat0 = jnp.concatenate([l0_w_in[:, c_q:c_conv], l0_w_in[:, :c_rope]], axis=1).astype(BF16)
    k_rope = l0_w_in[:, c_rope:c_q]
    w_kr = jnp.concatenate([k_rope, _rot_cols(k_rope)], axis=1).astype(BF16)
    z_conv, z_lat, kr = _inproj(x_lat, x_ctx, mod0, w_conv0, w_lat0, w_kr,
                                tn=w_conv0.shape[1] // 2, tn_b=Q_LORA + KV_LORA)

    wq = l0_w_qb.reshape(Q_LORA, MLA_HEADS, QK_DIM)
    wqn = wq[:, :, :QK_NOPE].reshape(Q_LORA, MLA_HEADS * QK_NOPE).astype(BF16)
    wqr = wq[:, :, QK_NOPE:]
    wqrr = _rot_cols(wqr).reshape(Q_LORA, MLA_HEADS * QK_ROPE).astype(BF16)
    wqr = wqr.reshape(Q_LORA, MLA_HEADS * QK_ROPE).astype(BF16)
    wkv = l0_w_kvb.reshape(KV_LORA, MLA_HEADS, QK_NOPE + V_HEAD)
    wk = wkv[:, :, :QK_NOPE].reshape(KV_LORA, MLA_HEADS * QK_NOPE).astype(BF16)
    wv = wkv[:, :, QK_NOPE:].reshape(KV_LORA, MLA_HEADS * V_HEAD).astype(BF16)
    q, k, v = _mla_prep(z_lat, kr, _rope_table(), l0_q_norm, wqn, wqr, wqrr, l0_kv_norm, wk, wv)
    o, up0, down0, up1, down1, w_out0, w_out1, w_kv1, w_qo1, w_lr, mods1 = _attention(
        q, k, v, l0_ffn_up, l0_ffn_down, l1_ffn_up, l1_ffn_down, l0_w_out, l1_w_out, l1_w_in,
        cvec, l1_ada_w, l1_ada_b)
    mod1 = _mod_rows(mods1, l1_norm1, l1_norm2, final_norm)
    xa = _ab_out(x_lat, x_ctx, z_conv, o, l0_conv_a, mod0, w_out0)
    xa = _ffn(xa, mod0, up0, l0_ffn_conv_w, l0_ffn_conv_b, down0, final=False)

    z1, lr = _inproj(xa, None, mod1, w_kv1, w_qo1, w_lr, tn=1536)
    pad_f = jnp.zeros((LANES - GATE_RANK, GLA_KEY), F32)
    gwf = jnp.concatenate([l1_gate_fw_w, pad_f], axis=0).astype(BF16)
    gwb = jnp.concatenate([jnp.zeros((GATE_RANK, GLA_KEY), F32), l1_gate_bw_w,
                           jnp.zeros((LANES - 2 * GATE_RANK, GLA_KEY), F32)], axis=0).astype(BF16)
    o_f, o_b = _gla(z1, lr, gwf, l1_gate_fw_b, gwb, l1_gate_bw_b)
    xa = _gla_out(xa, o_f, o_b, z1, l1_o_norm, mod1, w_out1)
    out = _ffn(xa, mod1, up1, l1_ffn_conv_w, l1_ffn_conv_b, down1, final=True)
    return out[None]
```
